```python
import math
import jax, jax.numpy as jnp
from jax import lax
import numpy as np

D_MODEL = 1024
BATCH = 8
SEQ = 2048
DEPTH = 2
DEC_BATCH = 128
DEC_SEQ = 1
PAST_LEN = 16384
PAGE_SIZE = 128

R_HEADS = 4
R_DK = 128
R_DV = 256
R_QK = R_HEADS * R_DK
R_VAL = R_HEADS * R_DV
ROPE_BASE = 10000.0
M_HEADS = 16
M_HEADDIM = 64
M_INNER = M_HEADS * M_HEADDIM
M_GROUPS = 2
M_STATE = 128
M_CONV_DIM = M_INNER + 2 * M_GROUPS * M_STATE
G_HEADS = 8
G_DK = 128
G_DV = 128
G_KEY = G_HEADS * G_DK
G_VAL = G_HEADS * G_DV
G_QKV = 2 * G_KEY + G_VAL
CONV_W = 4
CHUNK = 64
FFN_DIM = 2048
PLE_DIM = 256
N_BRANCH = 3
DN_ALPHA = (2 * DEPTH) ** 0.25
DN_BETA = (8 * DEPTH) ** -0.25
LN_EPS = 1e-5
NORM_EPS = 1e-6

IN_SPLITS = (R_QK, R_QK, R_VAL, R_VAL,
             M_INNER, M_CONV_DIM, M_HEADS,
             G_QKV, G_VAL, G_HEADS, G_HEADS,
             D_MODEL, D_MODEL, D_MODEL)
IN_DIM = sum(IN_SPLITS)

kernel_name = "hybrid_ret_ssd_gdn_decode_step"


def split_cols(h, sizes):
    idx = np.cumsum(sizes)[:-1].tolist()
    return jnp.split(h, idx, axis=-1)


def layer_norm(x, g, b):
    xf = x.astype(jnp.float32)
    mu = jnp.mean(xf, -1, keepdims=True)
    var = jnp.mean(jnp.square(xf - mu), -1, keepdims=True)
    return ((xf - mu) * lax.rsqrt(var + LN_EPS) * g + b).astype(x.dtype)


def rms(x):
    return x * lax.rsqrt(jnp.mean(jnp.square(x), -1, keepdims=True) + NORM_EPS)


def l2norm(x):
    return x * lax.rsqrt(jnp.sum(jnp.square(x), -1, keepdims=True) + NORM_EPS)


def swiglu(x, wg, wu, wd):
    return (jax.nn.silu(x @ wg) * (x @ wu)) @ wd


def rope(x, pos0):
    L = x.shape[1]
    half = x.shape[-1] // 2
    inv = ROPE_BASE ** (-jnp.arange(half, dtype=jnp.float32) / half)
    ang = (jnp.arange(L, dtype=jnp.float32) + pos0)[:, None] * inv[None, :]
    cos = jnp.cos(ang)[None, :, None, :]
    sin = jnp.sin(ang)[None, :, None, :]
    x1, x2 = x[..., :half], x[..., half:]
    return jnp.concatenate([x1 * cos - x2 * sin, x1 * sin + x2 * cos], axis=-1)


def causal_conv(x, buf, w, b=None):
    L = x.shape[1]
    xp = jnp.concatenate([buf.astype(x.dtype), x], axis=1)
    out = xp[:, 0:L] * w[0]
    for j in range(1, CONV_W):
        out = out + xp[:, j:j + L] * w[j]
    if b is not None:
        out = out + b
    return out, xp[:, L:]


def chunk_len(L):
    return CHUNK if L % CHUNK == 0 else L


def to_chunks(t, C):
    B, L = t.shape[:2]
    return jnp.moveaxis(t.reshape((B, L // C, C) + t.shape[2:]), 1, 0)


def from_chunks(t):
    n, B, C = t.shape[:3]
    return jnp.moveaxis(t, 0, 1).reshape((B, n * C) + t.shape[3:])


def seg_decay(cum):
    C = cum.shape[1]
    mask = jnp.tril(jnp.ones((C, C), bool))
    diff = jnp.moveaxis(cum[:, :, None, :] - cum[:, None, :, :], -1, 1)
    return jnp.exp(jnp.where(mask, diff, -jnp.inf))


def decay_linear_attn(q, k, v, log_a, s0):
    C = chunk_len(q.shape[1])

    def step(s, inp):
        qc, kc, vc, la = inp
        cum = jnp.cumsum(la, axis=1)
        scores = jnp.einsum("bihk,bjhk->bhij", qc, kc) * seg_decay(cum)
        o = (jnp.einsum("bhij,bjhv->bihv", scores, vc)
             + jnp.einsum("bihk,bhkv->bihv", qc * jnp.exp(cum)[..., None], s))
        last = cum[:, -1]
        s = (s * jnp.exp(last)[:, :, None, None]
             + jnp.einsum("bjhk,bjhv->bhkv", kc * jnp.exp(last[:, None] - cum)[..., None], vc))
        return s, o

    s, o = lax.scan(step, s0, tuple(to_chunks(t, C) for t in (q, k, v, log_a)))
    return from_chunks(o), s


def gated_delta(q, k, v, g, beta, s0):
    C = chunk_len(q.shape[1])
    eye = jnp.eye(C, dtype=jnp.float32)
    strict = jnp.tril(jnp.ones((C, C), bool), -1)

    def step(s, inp):
        qc, kc, vc, gc, bc = inp
        cum = jnp.cumsum(gc, axis=1)
        dec = seg_decay(cum)
        kb = kc * bc[..., None]
        l0 = jnp.where(strict, jnp.einsum("bihk,bjhk->bhij", kb, kc) * dec, 0.0)
        tinv = lax.linalg.triangular_solve(eye + l0, jnp.broadcast_to(eye, l0.shape),
                                           left_side=True, lower=True, unit_diagonal=True)
        rhs = vc * bc[..., None] - jnp.einsum("bjhk,bhkv->bjhv", kb * jnp.exp(cum)[..., None], s)
        u = jnp.einsum("bhij,bjhv->bihv", tinv, rhs)
        attn = jnp.einsum("bihk,bjhk->bhij", qc, kc) * dec
        o = (jnp.einsum("bihk,bhkv->bihv", qc * jnp.exp(cum)[..., None], s)
             + jnp.einsum("bhij,bjhv->bihv", attn, u))
        last = cum[:, -1]
        s = (s * jnp.exp(last)[:, :, None, None]
             + jnp.einsum("bjhk,bjhv->bhkv", kc * jnp.exp(last[:, None] - cum)[..., None], u))
        return s, o

    s, o = lax.scan(step, s0, tuple(to_chunks(t, C) for t in (q, k, v, g, beta)))
    return from_chunks(o), s


def token_mix(h, st, prm, i, pos0):
    st_ret, st_ssm, st_ssm_conv, st_gdn, st_gdn_conv = st
    B, L, _ = h.shape
    f32 = jnp.float32
    (rq, rk, rv, rg, mz, mxbc, mdt, gqkv, gz, ga, gb, m1, m2, m3) = split_cols(h @ prm["w_in"][i], IN_SPLITS)

    q = rope(rq.reshape(B, L, R_HEADS, R_DK).astype(f32), pos0)
    k = rope(rk.reshape(B, L, R_HEADS, R_DK).astype(f32), pos0) * (R_DK ** -0.5)
    v = rv.reshape(B, L, R_HEADS, R_DV).astype(f32)
    log_gamma = jnp.log1p(-jnp.exp2(-5.0 - jnp.arange(R_HEADS, dtype=f32)))
    o, s_ret = decay_linear_attn(q, k, v, jnp.broadcast_to(log_gamma, (B, L, R_HEADS)), st_ret.astype(f32))
    mu = jnp.mean(o, -1, keepdims=True)
    var = jnp.mean(jnp.square(o - mu), -1, keepdims=True)
    o = (o - mu) * lax.rsqrt(var + LN_EPS)
    y_ret = (o.reshape(B, L, R_VAL) * jax.nn.silu(rg.astype(f32))).astype(h.dtype) @ prm["w_ret_out"][i]

    xbc, conv_ssm = causal_conv(mxbc, st_ssm_conv, prm["ssm_conv_w"][i], prm["ssm_conv_b"][i])
    xbc = jax.nn.silu(xbc).astype(f32)
    xs, bm, cm = split_cols(xbc, (M_INNER, M_GROUPS * M_STATE, M_GROUPS * M_STATE))
    dt = jax.nn.softplus(mdt.astype(f32) + prm["ssm_dt_bias"][i])
    la = -jnp.exp(prm["ssm_a_log"][i]) * dt
    xh = xs.reshape(B, L, M_HEADS, M_HEADDIM)
    rep = M_HEADS // M_GROUPS
    bk = jnp.repeat(bm.reshape(B, L, M_GROUPS, M_STATE), rep, axis=2)
    cq = jnp.repeat(cm.reshape(B, L, M_GROUPS, M_STATE), rep, axis=2)
    o, s_ssm = decay_linear_attn(cq, bk, xh * dt[..., None], la, st_ssm.astype(f32))
    o = o + prm["ssm_d"][i][:, None] * xh
    o = (o.reshape(B, L, M_INNER) * jax.nn.silu(mz.astype(f32))).reshape(B, L, M_GROUPS, M_INNER // M_GROUPS)
    y_ssm = (rms(o).reshape(B, L, M_INNER) * prm["ssm_norm_w"][i]).astype(h.dtype) @ prm["w_ssm_out"][i]

    qkv, conv_gdn = causal_conv(gqkv, st_gdn_conv, prm["gdn_conv_w"][i])
    qkv = jax.nn.silu(qkv).astype(f32)
    gq, gk, gv = split_cols(qkv, (G_KEY, G_KEY, G_VAL))
    gq = l2norm(gq.reshape(B, L, G_HEADS, G_DK)) * (G_DK ** -0.5)
    gk = l2norm(gk.reshape(B, L, G_HEADS, G_DK))
    gv = gv.reshape(B, L, G_HEADS, G_DV)
    beta = jax.nn.sigmoid(gb.astype(f32))
    g = -jnp.exp(prm["gdn_a_log"][i]) * jax.nn.softplus(ga.astype(f32) + prm["gdn_dt_bias"][i])
    o, s_gdn = gated_delta(gq, gk, gv, g, beta, st_gdn.astype(f32))
    o = rms(o) * prm["gdn_norm_w"][i]
    y_gdn = (o.reshape(B, L, G_VAL) * jax.nn.silu(gz.astype(f32))).astype(h.dtype) @ prm["w_gdn_out"][i]

    mixed = jax.nn.sigmoid(m1) * y_ret + jax.nn.sigmoid(m2) * y_ssm + jax.nn.sigmoid(m3) * y_gdn
    return mixed @ prm["w_o"][i], (s_ret, s_ssm, conv_ssm, s_gdn, conv_gdn)


def layer(x, p_i, st, prm, i, pos0):
    g, b = prm["ln_g"][i], prm["ln_b"][i]
    wg, wu, wd = prm["ffn_wg"][i], prm["ffn_wu"][i], prm["ffn_wd"][i]
    x = layer_norm(DN_ALPHA * x + 0.5 * swiglu(x, wg[0], wu[0], wd[0]), g[0], b[0])
    mix, new_st = token_mix(x, st, prm, i, pos0)
    x = layer_norm(DN_ALPHA * x + mix, g[1], b[1])
    x = layer_norm(DN_ALPHA * x + 0.5 * swiglu(x, wg[1], wu[1], wd[1]), g[2], b[2])
    pe = jax.nn.sigmoid(x @ prm["pe_gate"][i]) * (p_i.astype(x.dtype) @ prm["pe_proj"][i])
    x = layer_norm(DN_ALPHA * x + pe, g[3], b[3])
    return x, new_st


def trunk(x, p, states, prm, pos0):
    new = []
    for i in range(DEPTH):
        x, st_i = layer(x, p[i], tuple(s[i] for s in states), prm, i, pos0)
        new.append(st_i)
    stacked = tuple(jnp.stack([n[j] for n in new]) for j in range(5))
    return x, stacked


def setup_inputs(seed: int = 0) -> dict:
    key = jax.random.key(seed)
    ks = iter(jax.random.split(key, 40))
    nrm = lambda shape, s: jax.random.normal(next(ks), shape, jnp.float32) * s

    def inv_softplus_dt(shape):
        u = jax.random.uniform(next(ks), shape, jnp.float32)
        dt = jnp.exp(u * (math.log(0.1) - math.log(0.001)) + math.log(0.001))
        return dt + jnp.log(-jnp.expm1(-dt))

    def a_log(shape):
        return jnp.log(jax.random.uniform(next(ks), shape, jnp.float32, 1.0, 16.0))

    return {
        "x_prompt": nrm((BATCH, SEQ, D_MODEL), 1.0),
        "x_sample": nrm((DEC_BATCH, DEC_SEQ, D_MODEL), 1.0),
        "state_ret": nrm((DEPTH, DEC_BATCH, R_HEADS, R_DK, R_DV), 1.0),
        "state_ssm": nrm((DEPTH, DEC_BATCH, M_HEADS, M_STATE, M_HEADDIM), 0.3),
        "state_ssm_conv": nrm((DEPTH, DEC_BATCH, CONV_W - 1, M_CONV_DIM), 1.0),
        "state_gdn": nrm((DEPTH, DEC_BATCH, G_HEADS, G_DK, G_DV), 0.3),
        "state_gdn_conv": nrm((DEPTH, DEC_BATCH, CONV_W - 1, G_QKV), 1.0),
        "p_prompt": nrm((DEPTH, BATCH, SEQ, PLE_DIM), 1.0),
        "p_sample": nrm((DEPTH, DEC_BATCH, DEC_SEQ, PLE_DIM), 1.0),
        "ln_g": 1.0 + nrm((DEPTH, 4, D_MODEL), 0.02),
        "ln_b": nrm((DEPTH, 4, D_MODEL), 0.02),
        "ffn_wg": nrm((DEPTH, 2, D_MODEL, FFN_DIM), D_MODEL ** -0.5),
        "ffn_wu": nrm((DEPTH, 2, D_MODEL, FFN_DIM), D_MODEL ** -0.5),
        "ffn_wd": nrm((DEPTH, 2, FFN_DIM, D_MODEL), FFN_DIM ** -0.5 * DN_BETA),
        "w_in": nrm((DEPTH, D_MODEL, IN_DIM), D_MODEL ** -0.5),
        "ssm_conv_w": nrm((DEPTH, CONV_W, M_CONV_DIM), CONV_W ** -0.5),
        "ssm_conv_b": nrm((DEPTH, M_CONV_DIM), 0.01),
        "ssm_dt_bias": inv_softplus_dt((DEPTH, M_HEADS)),
        "ssm_a_log": a_log((DEPTH, M_HEADS)),
        "ssm_d": 1.0 + nrm((DEPTH, M_HEADS), 0.1),
        "ssm_norm_w": 1.0 + nrm((DEPTH, M_INNER), 0.02),
        "gdn_conv_w": nrm((DEPTH, CONV_W, G_QKV), CONV_W ** -0.5),
        "gdn_dt_bias": inv_softplus_dt((DEPTH, G_HEADS)),
        "gdn_a_log": a_log((DEPTH, G_HEADS)),
        "gdn_norm_w": 1.0 + nrm((DEPTH, G_DV), 0.02),
        "w_ret_out": nrm((DEPTH, R_VAL, D_MODEL), R_VAL ** -0.5 * DN_BETA),
        "w_ssm_out": nrm((DEPTH, M_INNER, D_MODEL), M_INNER ** -0.5 * DN_BETA),
        "w_gdn_out": nrm((DEPTH, G_VAL, D_MODEL), G_VAL ** -0.5 * DN_BETA),
        "w_o": nrm((DEPTH, D_MODEL, D_MODEL), D_MODEL ** -0.5 * DN_BETA),
        "pe_proj": nrm((DEPTH, PLE_DIM, D_MODEL), PLE_DIM ** -0.5 * DN_BETA),
        "pe_gate": nrm((DEPTH, D_MODEL, D_MODEL), D_MODEL ** -0.5),
    }


def reference(x_prompt, x_sample, state_ret, state_ssm, state_ssm_conv, state_gdn, state_gdn_conv,
              p_prompt, p_sample, ln_g, ln_b, ffn_wg, ffn_wu, ffn_wd, w_in,
              ssm_conv_w, ssm_conv_b, ssm_dt_bias, ssm_a_log, ssm_d, ssm_norm_w,
              gdn_conv_w, gdn_dt_bias, gdn_a_log, gdn_norm_w,
              w_ret_out, w_ssm_out, w_gdn_out, w_o, pe_proj, pe_gate):
    prm = dict(ln_g=ln_g, ln_b=ln_b, ffn_wg=ffn_wg, ffn_wu=ffn_wu, ffn_wd=ffn_wd, w_in=w_in,
               ssm_conv_w=ssm_conv_w, ssm_conv_b=ssm_conv_b, ssm_dt_bias=ssm_dt_bias,
               ssm_a_log=ssm_a_log, ssm_d=ssm_d, ssm_norm_w=ssm_norm_w,
               gdn_conv_w=gdn_conv_w, gdn_dt_bias=gdn_dt_bias, gdn_a_log=gdn_a_log,
               gdn_norm_w=gdn_norm_w, w_ret_out=w_ret_out, w_ssm_out=w_ssm_out,
               w_gdn_out=w_gdn_out, w_o=w_o, pe_proj=pe_proj, pe_gate=pe_gate)
    f32 = jnp.float32
    zero_states = (jnp.zeros((DEPTH, BATCH, R_HEADS, R_DK, R_DV), f32),
                   jnp.zeros((DEPTH, BATCH, M_HEADS, M_STATE, M_HEADDIM), f32),
                   jnp.zeros((DEPTH, BATCH, CONV_W - 1, M_CONV_DIM), x_prompt.dtype),
                   jnp.zeros((DEPTH, BATCH, CONV_W - 1, G_QKV), x_prompt.dtype)[:, :, :, :0].sum() * 0 + jnp.zeros((DEPTH, BATCH, G_HEADS, G_DK, G_DV), f32) if False else jnp.zeros((DEPTH, BATCH, G_HEADS, G_DK, G_DV), f32),
                   jnp.zeros((DEPTH, BATCH, CONV_W - 1, G_QKV), x_prompt.dtype))
    y_prompt, (rp, sp, scp, gp, gcp) = trunk(x_prompt, p_prompt, zero_states, prm, 0)
    y_sample, (rs, ss, scs, gs, gcs) = trunk(
        x_sample, p_sample, (state_ret, state_ssm, state_ssm_conv, state_gdn, state_gdn_conv), prm, PAST_LEN)
    return (y_prompt, y_sample, rp, sp, scp, gp, gcp, rs, ss, scs, gs, gcs)
```

```python
import functools
import math

import numpy as np
import jax
import jax.numpy as jnp
from jax import lax
from jax.experimental import pallas as pl
from jax.experimental.pallas import tpu as pltpu

f32, bf16 = jnp.float32, jnp.bfloat16

D_MODEL = 1024
DEPTH = 2
PAST_LEN = 16384
R_HEADS, R_DK, R_DV = 4, 128, 256
R_QK, R_VAL = R_HEADS * R_DK, R_HEADS * R_DV
ROPE_BASE = 10000.0
M_HEADS, M_HEADDIM, M_GROUPS, M_STATE = 16, 64, 2, 128
M_INNER = M_HEADS * M_HEADDIM
M_CONV_DIM = M_INNER + 2 * M_GROUPS * M_STATE
M_HPG = M_HEADS // M_GROUPS
M_GW = M_HPG * M_HEADDIM
G_HEADS, G_DK, G_DV = 8, 128, 128
G_KEY, G_VAL = G_HEADS * G_DK, G_HEADS * G_DV
G_QKV = 2 * G_KEY + G_VAL
CONV_W = 4
FFN_DIM = 2048
PLE_DIM = 256
DN_ALPHA = (2 * DEPTH) ** 0.25
LN_EPS = 1e-5
NORM_EPS = 1e-6

_sizes = (R_QK, R_QK, R_VAL, R_VAL, M_INNER, M_CONV_DIM, M_HEADS, G_QKV, G_VAL, G_HEADS, G_HEADS,
          D_MODEL, D_MODEL, D_MODEL)
_off = np.concatenate([[0], np.cumsum(_sizes)]).tolist()
OFF_RET, OFF_SSD, OFF_GDN, OFF_MERGE, IN_DIM = _off[0], _off[4], _off[7], _off[11], _off[14]

LANES = 128
SUBLANES = 8
VMEM_LIMIT = 56 * 2 ** 20

TM_DENSE = 512
TL_SCAN = 512
CHUNK = 64
FFN_CHUNK = 512
DEC_BT = 8
PAD = LANES

LOG_GAMMA = [math.log1p(-(2.0 ** (-5.0 - h))) for h in range(R_HEADS)]


def _dot(a, b):
    return jnp.dot(a, b, preferred_element_type=f32)


def _dot_nt(a, b):
    return lax.dot_general(a, b, (((1,), (1,)), ((), ())), preferred_element_type=f32)


def _dot_tn(a, b):
    return lax.dot_general(a, b, (((0,), (0,)), ((), ())), preferred_element_type=f32)


def _b(x):
    return x.astype(bf16)


def _layer_norm(y, g, b):
    mu = jnp.mean(y, -1, keepdims=True)
    var = jnp.mean(jnp.square(y - mu), -1, keepdims=True)
    return (y - mu) * lax.rsqrt(var + LN_EPS) * g + b


def _rms(y):
    return y * lax.rsqrt(jnp.mean(jnp.square(y), -1, keepdims=True) + NORM_EPS)


def _const_spec(shape):
    return pl.BlockSpec(shape, lambda *_: (0,) * len(shape), pipeline_mode=pl.Buffered(1))


def _params(sem):
    return pltpu.CompilerParams(dimension_semantics=sem, vmem_limit_bytes=VMEM_LIMIT)


def _split3(x):
    a1 = _b(x)
    r1 = x - a1.astype(f32)
    a2 = _b(r1)
    a3 = _b(r1 - a2.astype(f32))
    return a1, a2, a3


def _cumsum_rows(tri_l, x):
    a1, a2, a3 = _split3(x)
    return _dot(tri_l, a1) + _dot(tri_l, a2) + _dot(tri_l, a3)


def _cumsum_cols(tri_u, xt):
    a1, a2, a3 = _split3(xt)
    return _dot(a1, tri_u) + _dot(a2, tri_u) + _dot(a3, tri_u)


def _chunk_iotas(c):
    ii = lax.broadcasted_iota(jnp.int32, (c, c), 0)
    jj = lax.broadcasted_iota(jnp.int32, (c, c), 1)
    return ii, jj


def _swiglu(x, wg_ref, wu_ref, wd_ref):
    xb = _b(x)
    acc = None
    for c in range(FFN_DIM // FFN_CHUNK):
        sl = slice(c * FFN_CHUNK, (c + 1) * FFN_CHUNK)
        a = jax.nn.silu(_dot(xb, wg_ref[:, sl])) * _dot(xb, wu_ref[:, sl])
        part = _dot(_b(a), wd_ref[sl, :])
        acc = part if acc is None else acc + part
    return acc


def _ffn_ln_kernel(x_ref, wg_ref, wu_ref, wd_ref, g_ref, b_ref, o_ref):
    x = x_ref[...]
    y = DN_ALPHA * x + 0.5 * _swiglu(x, wg_ref, wu_ref, wd_ref)
    o_ref[...] = _layer_norm(y, g_ref[0:1, :], b_ref[0:1, :])


def _ffn_ln(x, wg, wu, wd, g, b, tm):
    n = x.shape[0]
    return pl.pallas_call(
        _ffn_ln_kernel,
        out_shape=jax.ShapeDtypeStruct((n, D_MODEL), f32),
        grid=(n // tm,),
        in_specs=[pl.BlockSpec((tm, D_MODEL), lambda i: (i, 0)),
                  _const_spec(wg.shape), _const_spec(wu.shape), _const_spec(wd.shape),
                  _const_spec(g.shape), _const_spec(b.shape)],
        out_specs=pl.BlockSpec((tm, D_MODEL), lambda i: (i, 0)),
        compiler_params=_params(("parallel",)),
        name="ffn_ln",
    )(x, wg, wu, wd, g, b)


def _merge_kernel(x_ref, ar_ref, as_ref, ag_ref, wm_ref, wr_ref, ws_ref, wgd_ref, wo_ref, g_ref, b_ref, o_ref):
    x = x_ref[...]
    m = _dot(_b(x), wm_ref[...])
    yr = _dot(_b(ar_ref[...]), wr_ref[...])
    ys = _dot(_b(as_ref[...]), ws_ref[...])
    yg = _dot(_b(ag_ref[...]), wgd_ref[...])
    mixed = (jax.nn.sigmoid(m[:, 0:D_MODEL]) * yr + jax.nn.sigmoid(m[:, D_MODEL:2 * D_MODEL]) * ys
             + jax.nn.sigmoid(m[:, 2 * D_MODEL:3 * D_MODEL]) * yg)
    y = DN_ALPHA * x + _dot(_b(mixed), wo_ref[...])
    o_ref[...] = _layer_norm(y, g_ref[1:2, :], b_ref[1:2, :])


def _merge(x, ar, a_s, ag, wm, wr, ws, wgd, wo, g, b, tm):
    n = x.shape[0]
    tok = lambda i: (i, 0)
    return pl.pallas_call(
        _merge_kernel,
        out_shape=jax.ShapeDtypeStruct((n, D_MODEL), f32),
        grid=(n // tm,),
        in_specs=[pl.BlockSpec((tm, D_MODEL), tok)] * 4
                 + [_const_spec(w.shape) for w in (wm, wr, ws, wgd, wo, g, b)],
        out_specs=pl.BlockSpec((tm, D_MODEL), tok),
        compiler_params=_params(("parallel",)),
        name="merge",
    )(x, ar, a_s, ag, wm, wr, ws, wgd, wo, g, b)


def _ffn_pe_kernel(x_ref, p_ref, wg_ref, wu_ref, wd_ref, pg_ref, pp_ref, g_ref, b_ref, o_ref):
    x = x_ref[...]
    x = _layer_norm(DN_ALPHA * x + 0.5 * _swiglu(x, wg_ref, wu_ref, wd_ref), g_ref[2:3, :], b_ref[2:3, :])
    pe = jax.nn.sigmoid(_dot(_b(x), pg_ref[...])) * _dot(_b(p_ref[...]), pp_ref[...])
    o_ref[...] = _layer_norm(DN_ALPHA * x + pe, g_ref[3:4, :], b_ref[3:4, :])


def _ffn_pe(x, p, layer, wg, wu, wd, pg, pp, g, b, tm):
    n = x.shape[0]
    return pl.pallas_call(
        _ffn_pe_kernel,
        out_shape=jax.ShapeDtypeStruct((n, D_MODEL), f32),
        grid=(n // tm,),
        in_specs=[pl.BlockSpec((tm, D_MODEL), lambda i: (i, 0)),
                  pl.BlockSpec((None, tm, PLE_DIM), lambda i: (layer, i, 0))]
                 + [_const_spec(w.shape) for w in (wg, wu, wd, pg, pp, g, b)],
        out_specs=pl.BlockSpec((tm, D_MODEL), lambda i: (i, 0)),
        compiler_params=_params(("parallel",)),
        name="ffn_pe",
    )(x, p, wg, wu, wd, pg, pp, g, b)


def _rope_inplace(proj_ref, off, cos, sin, scale):
    t = proj_ref[:, off:off + R_DK]
    t = t * cos + pltpu.roll(t, R_DK // 2, axis=1) * sin
    if scale != 1.0:
        t = t * scale
    proj_ref[:, off:off + R_DK] = t


def _ret_scan_kernel(x_ref, w_ref, cos_ref, sin_ref, act_ref, st_ref, proj_ref, s_ref, *, tl, c):
    l = pl.program_id(1)

    @pl.when(l == 0)
    def _():
        s_ref[...] = jnp.zeros_like(s_ref)

    proj_ref[...] = _dot(_b(x_ref[...]), w_ref[...])
    cos, sin = cos_ref[...], sin_ref[...]
    for h in range(R_HEADS):
        _rope_inplace(proj_ref, h * R_DK, cos, sin, 1.0)
        _rope_inplace(proj_ref, R_QK + h * R_DK, cos, sin, R_DK ** -0.5)

    ii, jj = _chunk_iotas(c)
    dif = (ii - jj).astype(f32)
    ci = lax.broadcasted_iota(jnp.int32, (c, 1), 0).astype(f32)
    decay = [jnp.where(dif >= 0, jnp.exp(dif * lg), 0.0) for lg in LOG_GAMMA]
    e_col = [jnp.exp((ci + 1.0) * lg) for lg in LOG_GAMMA]
    w_col = [jnp.exp((c - 1.0 - ci) * lg) for lg in LOG_GAMMA]

    def chunk(ck, carry):
        rows = pl.ds(pl.multiple_of(ck * c, c), c)
        for h in range(R_HEADS):
            q = proj_ref[rows, h * R_DK:(h + 1) * R_DK]
            k = proj_ref[rows, R_QK + h * R_DK:R_QK + (h + 1) * R_DK]
            v = _b(proj_ref[rows, 2 * R_QK + h * R_DV:2 * R_QK + (h + 1) * R_DV])
            s = s_ref[h]
            scores = _dot_nt(_b(q), _b(k)) * decay[h]
            o = _dot(_b(scores), v) + _dot(_b(q * e_col[h]), _b(s))
            s_ref[h] = s * math.exp(c * LOG_GAMMA[h]) + _dot_tn(_b(k * w_col[h]), v)
            mu = jnp.mean(o, -1, keepdims=True)
            var = jnp.mean(jnp.square(o - mu), -1, keepdims=True)
            on = (o - mu) * lax.rsqrt(var + LN_EPS)
            g = proj_ref[rows, 2 * R_QK + R_VAL + h * R_DV:2 * R_QK + R_VAL + (h + 1) * R_DV]
            act_ref[rows, h * R_DV:(h + 1) * R_DV] = _b(on * jax.nn.silu(g))
        return carry

    lax.fori_loop(0, tl // c, chunk, 0)

    @pl.when(l == pl.num_programs(1) - 1)
    def _():
        st_ref[0] = s_ref[...]


def _ret_scan(x, w, cos, sin, nb, nl_tok, tl, c):
    nl = nl_tok // tl
    tok = lambda b, l: (b * nl + l, 0)
    return pl.pallas_call(
        functools.partial(_ret_scan_kernel, tl=tl, c=c),
        out_shape=(jax.ShapeDtypeStruct((nb * nl_tok, R_VAL), bf16),
                   jax.ShapeDtypeStruct((nb, R_HEADS, R_DK, R_DV), f32)),
        grid=(nb, nl),
        in_specs=[pl.BlockSpec((tl, D_MODEL), tok), _const_spec(w.shape),
                  pl.BlockSpec((tl, R_DK), lambda b, l: (l, 0)),
                  pl.BlockSpec((tl, R_DK), lambda b, l: (l, 0))],
        out_specs=(pl.BlockSpec((tl, R_VAL), tok),
                   pl.BlockSpec((1, R_HEADS, R_DK, R_DV), lambda b, l: (b, 0, 0, 0))),
        scratch_shapes=[pltpu.VMEM((tl, w.shape[1]), f32),
                        pltpu.VMEM((R_HEADS, R_DK, R_DV), f32)],
        compiler_params=_params(("parallel", "arbitrary")),
        name="ret_scan",
    )(x, w, cos, sin)


def _causal_conv_tile(xbuf_ref, src_ref, src_off, width, cw_ref, cb_ref, dst_ref, tl, first):
    @pl.when(first)
    def _():
        xbuf_ref[0:SUBLANES, :] = jnp.zeros((SUBLANES, width), f32)

    xbuf_ref[SUBLANES:SUBLANES + tl, :] = src_ref[:, src_off:src_off + width]
    cblk = 512
    for cb in range(width // cblk):
        cs = slice(cb * cblk, (cb + 1) * cblk)
        acc = xbuf_ref[SUBLANES:SUBLANES + tl, cs] * cw_ref[CONV_W - 1:CONV_W, cs]
        for j in range(CONV_W - 1):
            r0 = SUBLANES - (CONV_W - 1) + j
            acc = acc + xbuf_ref[r0:r0 + tl, cs] * cw_ref[j:j + 1, cs]
        if cb_ref is not None:
            acc = acc + cb_ref[:, cs]
        dst_ref[:, cs] = jax.nn.silu(acc)
    xbuf_ref[0:SUBLANES, :] = xbuf_ref[tl:tl + SUBLANES, :]


def _ssd_scan_kernel(x_ref, w_ref, cw_ref, cb_ref, dtb_ref, alog_ref, dexp_ref, nw_ref,
                     act_ref, st_ref, conv_ref,
                     proj_ref, xbuf_ref, xc_ref, dt_ref, la_ref, y_ref, xw_ref, s_ref, *, tl, c):
    l = pl.program_id(1)

    @pl.when(l == 0)
    def _():
        s_ref[...] = jnp.zeros_like(s_ref)

    proj_ref[...] = _dot(_b(x_ref[...]), w_ref[...])
    _causal_conv_tile(xbuf_ref, proj_ref, M_INNER, M_CONV_DIM, cw_ref, cb_ref, xc_ref, tl, l == 0)
    dt = jax.nn.softplus(proj_ref[:, M_INNER + M_CONV_DIM:M_INNER + M_CONV_DIM + PAD] + dtb_ref[...])
    dt_ref[...] = dt
    la_ref[...] = -jnp.exp(alog_ref[...]) * dt

    ii, jj = _chunk_iotas(c)
    tri_l = _b((ii >= jj).astype(f32))
    tri_u = _b((ii <= jj).astype(f32))
    b_off, c_off = M_INNER, M_INNER + M_GROUPS * M_STATE

    def chunk(ck, carry):
        rows = pl.ds(pl.multiple_of(ck * c, c), c)
        la_c = la_ref[rows, :]
        cum = _cumsum_rows(tri_l, la_c)
        cum_t = _cumsum_cols(tri_u, la_c.T)
        dt_c = dt_ref[rows, :]
        last = cum[c - 1:c, :]
        e_last = jnp.exp(last)
        for g in range(M_GROUPS):
            bb = _b(xc_ref[rows, b_off + g * M_STATE:b_off + (g + 1) * M_STATE])
            cb = _b(xc_ref[rows, c_off + g * M_STATE:c_off + (g + 1) * M_STATE])
            gmat = _dot_nt(cb, bb)
            sg = s_ref[g]
            inter = _dot(cb, _b(sg))
            for hh in range(M_HPG):
                h = g * M_HPG + hh
                hs = slice(h * M_HEADDIM, (h + 1) * M_HEADDIM)
                gs = slice(hh * M_HEADDIM, (hh + 1) * M_HEADDIM)
                cum_h = cum[:, h:h + 1]
                dmat = jnp.where(ii >= jj, jnp.exp(cum_h - cum_t[h:h + 1, :]), 0.0)
                xs_h = xc_ref[rows, hs]
                xdt = xs_h * dt_c[:, h:h + 1]
                o = _dot(_b(gmat * dmat), _b(xdt)) + jnp.exp(cum_h) * inter[:, gs]
                y_ref[rows, hs] = o + dexp_ref[:, hs] * xs_h
                xw_ref[:, gs] = xdt * jnp.exp(last[:, h:h + 1] - cum_h)
            el_g = jnp.concatenate(
                [jnp.broadcast_to(e_last[:, g * M_HPG + hh:g * M_HPG + hh + 1], (1, M_HEADDIM))
                 for hh in range(M_HPG)], axis=1)
            s_ref[g] = sg * el_g + _dot_tn(bb, _b(xw_ref[...]))
        return carry

    lax.fori_loop(0, tl // c, chunk, 0)

    y = y_ref[...] * jax.nn.silu(proj_ref[:, 0:M_INNER])
    for g in range(M_GROUPS):
        gs = slice(g * M_GW, (g + 1) * M_GW)
        act_ref[:, gs] = _b(_rms(y[:, gs]) * nw_ref[:, gs])

    @pl.when(l == pl.num_programs(1) - 1)
    def _():
        for h in range(M_HEADS):
            g, hh = divmod(h, M_HPG)
            st_ref[0, h] = s_ref[g][:, hh * M_HEADDIM:(hh + 1) * M_HEADDIM]
        conv_ref[0] = xbuf_ref[SUBLANES - (CONV_W - 1):SUBLANES, :]


def _ssd_scan(x, w, cw, cb, dtb, alog, dexp, nw, nb, nl_tok, tl, c):
    nl = nl_tok // tl
    tok = lambda b, l: (b * nl + l, 0)
    return pl.pallas_call(
        functools.partial(_ssd_scan_kernel, tl=tl, c=c),
        out_shape=(jax.ShapeDtypeStruct((nb * nl_tok, M_INNER), bf16),
                   jax.ShapeDtypeStruct((nb, M_HEADS, M_STATE, M_HEADDIM), f32),
                   jax.ShapeDtypeStruct((nb, CONV_W - 1, M_CONV_DIM), f32)),
        grid=(nb, nl),
        in_specs=[pl.BlockSpec((tl, D_MODEL), tok)]
                 + [_const_spec(a.shape) for a in (w, cw, cb, dtb, alog, dexp, nw)],
        out_specs=(pl.BlockSpec((tl, M_INNER), tok),
                   pl.BlockSpec((1, M_HEADS, M_STATE, M_HEADDIM), lambda b, l: (b, 0, 0, 0)),
                   pl.BlockSpec((1, CONV_W - 1, M_CONV_DIM), lambda b, l: (b, 0, 0))),
        scratch_shapes=[pltpu.VMEM((tl, w.shape[1]), f32),
                        pltpu.VMEM((tl + SUBLANES, M_CONV_DIM), f32),
                        pltpu.VMEM((tl, M_CONV_DIM), f32),
                        pltpu.VMEM((tl, PAD), f32),
                        pltpu.VMEM((tl, PAD), f32),
                        pltpu.VMEM((tl, M_INNER), f32),
                        pltpu.VMEM((c, M_GW), f32),
                        pltpu.VMEM((M_GROUPS, M_STATE, M_GW), f32)],
        compiler_params=_params(("parallel", "arbitrary")),
        name="ssd_scan",
    )(x, w, cw, cb, dtb, alog, dexp, nw)


def _unit_lower_inverse(lm, eye, c):
    p = eye - lm
    m = lm
    k = 2
    while k < c:
        mb = _b(m)
        m = _dot(mb, mb)
        p = p + _dot(_b(p), _b(m))
        k *= 2
    return p


def _gdn_scan_kernel(x_ref, w_ref, cw_ref, dtb_ref, alog_ref, nw_ref,
                     act_ref, st_ref, conv_ref,
                     proj_ref, xbuf_ref, qkv_ref, g_ref, beta_ref, s_ref, *, tl, c):
    l = pl.program_id(1)

    @pl.when(l == 0)
    def _():
        s_ref[...] = jnp.zeros_like(s_ref)

    proj_ref[...] = _dot(_b(x_ref[...]), w_ref[...])
    _causal_conv_tile(xbuf_ref, proj_ref, 0, G_QKV, cw_ref, None, qkv_ref, tl, l == 0)
    for h in range(G_HEADS):
        qs = slice(h * G_DK, (h + 1) * G_DK)
        ks = slice(G_KEY + h * G_DK, G_KEY + (h + 1) * G_DK)
        q = qkv_ref[:, qs]
        qkv_ref[:, qs] = q * lax.rsqrt(jnp.sum(jnp.square(q), -1, keepdims=True) + NORM_EPS) * (G_DK ** -0.5)
        k = qkv_ref[:, ks]
        qkv_ref[:, ks] = k * lax.rsqrt(jnp.sum(jnp.square(k), -1, keepdims=True) + NORM_EPS)
    ab = proj_ref[:, G_QKV + G_VAL:G_QKV + G_VAL + PAD]
    g_ref[...] = -jnp.exp(alog_ref[...]) * jax.nn.softplus(ab + dtb_ref[...])
    beta_ref[...] = jax.nn.sigmoid(ab)

    ii, jj = _chunk_iotas(c)
    tri_l = _b((ii >= jj).astype(f32))
    tri_u = _b((ii <= jj).astype(f32))
    eye = (ii == jj).astype(f32)

    def chunk(ck, carry):
        rows = pl.ds(pl.multiple_of(ck * c, c), c)
        g_c = g_ref[rows, :]
        cum = _cumsum_rows(tri_l, g_c)
        cum_t = _cumsum_cols(tri_u, g_c.T)
        beta_c = beta_ref[rows, :]
        last = cum[c - 1:c, :]
        for h in range(G_HEADS):
            hs = slice(h * G_DK, (h + 1) * G_DK)
            q = qkv_ref[rows, hs]
            k = qkv_ref[rows, G_KEY + h * G_DK:G_KEY + (h + 1) * G_DK]
            v = qkv_ref[rows, 2 * G_KEY + h * G_DV:2 * G_KEY + (h + 1) * G_DV]
            b_h = beta_c[:, G_HEADS + h:G_HEADS + h + 1]
            cum_h = cum[:, h:h + 1]
            last_h = last[:, h:h + 1]
            dmat = jnp.where(ii >= jj, jnp.exp(cum_h - cum_t[h:h + 1, :]), 0.0)
            kb = k * b_h
            k16 = _b(k)
            lm = jnp.where(ii > jj, _dot_nt(_b(kb), k16) * dmat, 0.0)
            tinv = _unit_lower_inverse(lm, eye, c)
            e = jnp.exp(cum_h)
            s = s_ref[h]
            s16 = _b(s)
            rhs = v * b_h - _dot(_b(kb * e), s16)
            u = _b(_dot(_b(tinv), _b(rhs)))
            attn = _dot_nt(_b(q), k16) * dmat
            o = _dot(_b(q * e), s16) + _dot(_b(attn), u)
            s_ref[h] = s * jnp.exp(last_h) + _dot_tn(_b(k * jnp.exp(last_h - cum_h)), u)
            gz = proj_ref[rows, G_QKV + h * G_DV:G_QKV + (h + 1) * G_DV]
            act_ref[rows, h * G_DV:(h + 1) * G_DV] = _b(_rms(o) * nw_ref[...] * jax.nn.silu(gz))
        return carry

    lax.fori_loop(0, tl // c, chunk, 0)

    @pl.when(l == pl.num_programs(1) - 1)
    def _():
        st_ref[0] = s_ref[...]
        conv_ref[0] = xbuf_ref[SUBLANES - (CONV_W - 1):SUBLANES, :]


def _gdn_scan(x, w, cw, dtb, alog, nw, nb, nl_tok, tl, c):
    nl = nl_tok // tl
    tok = lambda b, l: (b * nl + l, 0)
    return pl.pallas_call(
        functools.partial(_gdn_scan_kernel, tl=tl, c=c),
        out_shape=(jax.ShapeDtypeStruct((nb * nl_tok, G_VAL), bf16),
                   jax.ShapeDtypeStruct((nb, G_HEADS, G_DK, G_DV), f32),
                   jax.ShapeDtypeStruct((nb, CONV_W - 1, G_QKV), f32)),
        grid=(nb, nl),
        in_specs=[pl.BlockSpec((tl, D_MODEL), tok)]
                 + [_const_spec(a.shape) for a in (w, cw, dtb, alog, nw)],
        out_specs=(pl.BlockSpec((tl, G_VAL), tok),
                   pl.BlockSpec((1, G_HEADS, G_DK, G_DV), lambda b, l: (b, 0, 0, 0)),
                   pl.BlockSpec((1, CONV_W - 1, G_QKV), lambda b, l: (b, 0, 0))),
        scratch_shapes=[pltpu.VMEM((tl, w.shape[1]), f32),
                        pltpu.VMEM((tl + SUBLANES, G_QKV), f32),
                        pltpu.VMEM((tl, G_QKV), f32),
                        pltpu.VMEM((tl, PAD), f32),
                        pltpu.VMEM((tl, PAD), f32),
                        pltpu.VMEM((G_HEADS, G_DK, G_DV), f32)],
        compiler_params=_params(("parallel", "arbitrary")),
        name="gdn_scan",
    )(x, w, cw, dtb, alog, nw)


def _token_lanes_to_front(src_ref, dst_ref, i):
    n = src_ref.shape[1]
    dst_ref[...] = pltpu.roll(src_ref[...], (n - i * DEC_BT) % n, axis=1)


def _col_sum(x):
    return jnp.sum(x, axis=0, keepdims=True)


def _conv_step(cst_ref, x_new, width, cw_ref, cb_ref, conv_out_ref):
    acc = x_new * cw_ref[CONV_W - 1:CONV_W, :]
    for j in range(CONV_W - 1):
        acc = acc + cst_ref[:, j * width:(j + 1) * width] * cw_ref[j:j + 1, :]
    if cb_ref is not None:
        acc = acc + cb_ref[...]
    conv_out_ref[:, 0:(CONV_W - 2) * width] = cst_ref[:, width:(CONV_W - 1) * width]
    conv_out_ref[:, (CONV_W - 2) * width:(CONV_W - 1) * width] = x_new
    return jax.nn.silu(acc)


def _dec_ret_kernel(x_ref, w_ref, cos_ref, sin_ref, s_in, act_ref, s_out,
                    qt_ref, kt_ref, vg_ref, qs_ref, ks_ref, o_ref):
    i = pl.program_id(0)

    @pl.when(i == 0)
    def _():
        proj = _dot(_b(x_ref[...]), w_ref[...])
        cos, sin = cos_ref[...], sin_ref[...]
        for h in range(R_HEADS):
            for off, scale, dst in ((h * R_DK, 1.0, qt_ref), (R_QK + h * R_DK, R_DK ** -0.5, kt_ref)):
                t = proj[:, off:off + R_DK]
                t = (t * cos + pltpu.roll(t, R_DK // 2, axis=1) * sin) * scale
                dst[h * R_DK:(h + 1) * R_DK, :] = t.T
        vg_ref[...] = proj[:, 2 * R_QK:]

    _token_lanes_to_front(qt_ref, qs_ref, i)
    _token_lanes_to_front(kt_ref, ks_ref, i)
    vg = vg_ref[pl.ds(pl.multiple_of(i * DEC_BT, DEC_BT), DEC_BT), :]
    for j in range(DEC_BT):
        for h in range(R_HEADS):
            gam = math.exp(LOG_GAMMA[h])
            qc = qs_ref[h * R_DK:(h + 1) * R_DK, j:j + 1]
            kc = ks_ref[h * R_DK:(h + 1) * R_DK, j:j + 1]
            v_row = vg[j:j + 1, h * R_DV:(h + 1) * R_DV]
            s = s_in[j, h]
            o_ref[j:j + 1, h * R_DV:(h + 1) * R_DV] = gam * _col_sum(qc * s) + _col_sum(qc * kc) * v_row
            s_out[j, h] = s * gam + kc * v_row
    for h in range(R_HEADS):
        o = o_ref[:, h * R_DV:(h + 1) * R_DV]
        mu = jnp.mean(o, -1, keepdims=True)
        var = jnp.mean(jnp.square(o - mu), -1, keepdims=True)
        g = vg[:, R_VAL + h * R_DV:R_VAL + (h + 1) * R_DV]
        act_ref[:, h * R_DV:(h + 1) * R_DV] = (o - mu) * lax.rsqrt(var + LN_EPS) * jax.nn.silu(g)


def _dec_ret(x, w, cos, sin, state, layer):
    n = x.shape[0]
    sblk = (None, DEC_BT, R_HEADS, R_DK, R_DV)
    return pl.pallas_call(
        _dec_ret_kernel,
        out_shape=(jax.ShapeDtypeStruct((n, R_VAL), f32),
                   jax.ShapeDtypeStruct((n, R_HEADS, R_DK, R_DV), f32)),
        grid=(n // DEC_BT,),
        in_specs=[_const_spec(x.shape), _const_spec(w.shape), _const_spec(cos.shape), _const_spec(sin.shape),
                  pl.BlockSpec(sblk, lambda i: (layer, i, 0, 0, 0))],
        out_specs=(pl.BlockSpec((DEC_BT, R_VAL), lambda i: (i, 0)),
                   pl.BlockSpec(sblk[1:], lambda i: (i, 0, 0, 0))),
        scratch_shapes=[pltpu.VMEM((R_QK, n), f32), pltpu.VMEM((R_QK, n), f32),
                        pltpu.VMEM((n, 2 * R_VAL), f32),
                        pltpu.VMEM((R_QK, n), f32), pltpu.VMEM((R_QK, n), f32),
                        pltpu.VMEM((DEC_BT, R_VAL), f32)],
        compiler_params=_params(("arbitrary",)),
        name="dec_ret",
    )(x, w, cos, sin, state)


def _dec_ssd_kernel(x_ref, w_ref, cst_ref, cw_ref, cb_ref, dtb_ref, alog_ref, dexp_ref, nw_ref, s_in,
                    act_ref, s_out, conv_out_ref,
                    xs_ref, z_ref, bt_ref, ct_ref, dt_ref, ela_ref, bs_ref, cs_ref, o_ref):
    i = pl.program_id(0)

    @pl.when(i == 0)
    def _():
        proj = _dot(_b(x_ref[...]), w_ref[...])
        xc = _conv_step(cst_ref, proj[:, M_INNER:M_INNER + M_CONV_DIM], M_CONV_DIM, cw_ref, cb_ref, conv_out_ref)
        xs_ref[...] = xc[:, 0:M_INNER]
        z_ref[...] = proj[:, 0:M_INNER]
        bt_ref[...] = xc[:, M_INNER:M_INNER + M_GROUPS * M_STATE].T
        ct_ref[...] = xc[:, M_INNER + M_GROUPS * M_STATE:M_CONV_DIM].T
        dt = jax.nn.softplus(proj[:, M_INNER + M_CONV_DIM:M_INNER + M_CONV_DIM + PAD] + dtb_ref[...])
        dt_ref[...] = dt
        ela_ref[...] = jnp.exp(-jnp.exp(alog_ref[...]) * dt)

    _token_lanes_to_front(bt_ref, bs_ref, i)
    _token_lanes_to_front(ct_ref, cs_ref, i)
    rows = pl.ds(pl.multiple_of(i * DEC_BT, DEC_BT), DEC_BT)
    xs8, dt8, ela8 = xs_ref[rows, :], dt_ref[rows, :], ela_ref[rows, :]
    for j in range(DEC_BT):
        for g in range(M_GROUPS):
            bc = bs_ref[g * M_STATE:(g + 1) * M_STATE, j:j + 1]
            cc = cs_ref[g * M_STATE:(g + 1) * M_STATE, j:j + 1]
            cb = _col_sum(cc * bc)
            for hh in range(M_HPG):
                h = g * M_HPG + hh
                hs = slice(h * M_HEADDIM, (h + 1) * M_HEADDIM)
                x_row = xs8[j:j + 1, hs]
                xdt = x_row * dt8[j:j + 1, h:h + 1]
                eh = ela8[j:j + 1, h:h + 1]
                s = s_in[j, h]
                o_ref[j:j + 1, hs] = cb * xdt + eh * _col_sum(cc * s) + dexp_ref[:, hs] * x_row
                s_out[j, h] = s * eh + bc * xdt
    y = o_ref[...] * jax.nn.silu(z_ref[rows, :])
    for g in range(M_GROUPS):
        gs = slice(g * M_GW, (g + 1) * M_GW)
        act_ref[:, gs] = _rms(y[:, gs]) * nw_ref[:, gs]


def _dec_ssd(x, w, cst, cw, cb, dtb, alog, dexp, nw, state, layer):
    n = x.shape[0]
    sblk = (None, DEC_BT, M_HEADS, M_STATE, M_HEADDIM)
    consts = (x, w, cst, cw, cb, dtb, alog, dexp, nw)
    return pl.pallas_call(
        _dec_ssd_kernel,
        out_shape=(jax.ShapeDtypeStruct((n, M_INNER), f32),
                   jax.ShapeDtypeStruct((n, M_HEADS, M_STATE, M_HEADDIM), f32),
                   jax.ShapeDtypeStruct(cst.shape, f32)),
        grid=(n // DEC_BT,),
        in_specs=[_const_spec(a.shape) for a in consts] + [pl.BlockSpec(sblk, lambda i: (layer, i, 0, 0, 0))],
        out_specs=(pl.BlockSpec((DEC_BT, M_INNER), lambda i: (i, 0)),
                   pl.BlockSpec(sblk[1:], lambda i: (i, 0, 0, 0)),
                   pl.BlockSpec(cst.shape, lambda i: (0, 0))),
        scratch_shapes=[pltpu.VMEM((n, M_INNER), f32), pltpu.VMEM((n, M_INNER), f32),
                        pltpu.VMEM((M_GROUPS * M_STATE, n), f32), pltpu.VMEM((M_GROUPS * M_STATE, n), f32),
                        pltpu.VMEM((n, PAD), f32), pltpu.VMEM((n, PAD), f32),
                        pltpu.VMEM((M_GROUPS * M_STATE, n), f32), pltpu.VMEM((M_GROUPS * M_STATE, n), f32),
                        pltpu.VMEM((DEC_BT, M_INNER), f32)],
        compiler_params=_params(("arbitrary",)),
        name="dec_ssd",
    )(*consts, state)


def _dec_gdn_kernel(x_ref, w_ref, cst_ref, cw_ref, dtb_ref, alog_ref, nw_ref, s_in,
                    act_ref, s_out, conv_out_ref,
                    qt_ref, kt_ref, v_ref, gz_ref, eg_ref, beta_ref, qs_ref, ks_ref, o_ref):
    i = pl.program_id(0)

    @pl.when(i == 0)
    def _():
        proj = _dot(_b(x_ref[...]), w_ref[...])
        qkv = _conv_step(cst_ref, proj[:, 0:G_QKV], G_QKV, cw_ref, None, conv_out_ref)
        for h in range(G_HEADS):
            q = qkv[:, h * G_DK:(h + 1) * G_DK]
            q = q * lax.rsqrt(jnp.sum(jnp.square(q), -1, keepdims=True) + NORM_EPS) * (G_DK ** -0.5)
            qt_ref[h * G_DK:(h + 1) * G_DK, :] = q.T
            k = qkv[:, G_KEY + h * G_DK:G_KEY + (h + 1) * G_DK]
            k = k * lax.rsqrt(jnp.sum(jnp.square(k), -1, keepdims=True) + NORM_EPS)
            kt_ref[h * G_DK:(h + 1) * G_DK, :] = k.T
        v_ref[...] = qkv[:, 2 * G_KEY:]
        gz_ref[...] = proj[:, G_QKV:G_QKV + G_VAL]
        ab = proj[:, G_QKV + G_VAL:G_QKV + G_VAL + PAD]
        eg_ref[...] = jnp.exp(-jnp.exp(alog_ref[...]) * jax.nn.softplus(ab + dtb_ref[...]))
        beta_ref[...] = jax.nn.sigmoid(ab)

    _token_lanes_to_front(qt_ref, qs_ref, i)
    _token_lanes_to_front(kt_ref, ks_ref, i)
    rows = pl.ds(pl.multiple_of(i * DEC_BT, DEC_BT), DEC_BT)
    v8, eg8, beta8 = v_ref[rows, :], eg_ref[rows, :], beta_ref[rows, :]
    for j in range(DEC_BT):
        for h in range(G_HEADS):
            hs = slice(h * G_DV, (h + 1) * G_DV)
            qc = qs_ref[h * G_DK:(h + 1) * G_DK, j:j + 1]
            kc = ks_ref[h * G_DK:(h + 1) * G_DK, j:j + 1]
            bh = beta8[j:j + 1, G_HEADS + h:G_HEADS + h + 1]
            eg = eg8[j:j + 1, h:h + 1]
            s = s_in[j, h]
            u = v8[j:j + 1, hs] * bh - (bh * eg) * _col_sum(kc * s)
            o_ref[j:j + 1, hs] = eg * _col_sum(qc * s) + _col_sum(qc * kc) * u
            s_out[j, h] = s * eg + kc * u
    gz8 = gz_ref[rows, :]
    for h in range(G_HEADS):
        hs = slice(h * G_DV, (h + 1) * G_DV)
        act_ref[:, hs] = _rms(o_ref[:, hs]) * nw_ref[...] * jax.nn.silu(gz8[:, hs])


def _dec_gdn(x, w, cst, cw, dtb, alog, nw, state, layer):
    n = x.shape[0]
    sblk = (None, DEC_BT, G_HEADS, G_DK, G_DV)
    consts = (x, w, cst, cw, dtb, alog, nw)
    return pl.pallas_call(
        _dec_gdn_kernel,
        out_shape=(jax.ShapeDtypeStruct((n, G_VAL), f32),
                   jax.ShapeDtypeStruct((n, G_HEADS, G_DK, G_DV), f32),
                   jax.ShapeDtypeStruct(cst.shape, f32)),
        grid=(n // DEC_BT,),
        in_specs=[_const_spec(a.shape) for a in consts] + [pl.BlockSpec(sblk, lambda i: (layer, i, 0, 0, 0))],
        out_specs=(pl.BlockSpec((DEC_BT, G_VAL), lambda i: (i, 0)),
                   pl.BlockSpec(sblk[1:], lambda i: (i, 0, 0, 0)),
                   pl.BlockSpec(cst.shape, lambda i: (0, 0))),
        scratch_shapes=[pltpu.VMEM((G_KEY, n), f32), pltpu.VMEM((G_KEY, n), f32),
                        pltpu.VMEM((n, G_VAL), f32), pltpu.VMEM((n, G_VAL), f32),
                        pltpu.VMEM((n, PAD), f32), pltpu.VMEM((n, PAD), f32),
                        pltpu.VMEM((G_KEY, n), f32), pltpu.VMEM((G_KEY, n), f32),
                        pltpu.VMEM((DEC_BT, G_VAL), f32)],
        compiler_params=_params(("arbitrary",)),
        name="dec_gdn",
    )(*consts, state)


def _rope_tables(pos):
    half = R_DK // 2
    inv = ROPE_BASE ** (-jnp.arange(half, dtype=f32) / half)
    ang = pos[:, None] * inv[None, :]
    cos, sin = jnp.cos(ang), jnp.sin(ang)
    return jnp.concatenate([cos, cos], axis=1), jnp.concatenate([-sin, sin], axis=1)


def _lane_pad(v, start=0):
    return jnp.zeros((1, PAD), f32).at[0, start:start + v.shape[0]].set(v)


def _layer_weights(i, prm):
    w_in = prm["w_in"][i]
    zpad = jnp.zeros((D_MODEL, PAD - M_HEADS), f32)
    w_ssd = jnp.concatenate([w_in[:, OFF_SSD:OFF_GDN], zpad], axis=1)
    w_gdn = jnp.concatenate([w_in[:, OFF_GDN:OFF_MERGE], jnp.zeros((D_MODEL, PAD - 2 * G_HEADS), f32)], axis=1)
    return dict(
        ln_g=prm["ln_g"][i], ln_b=prm["ln_b"][i],
        wg=_b(prm["ffn_wg"][i]), wu=_b(prm["ffn_wu"][i]), wd=_b(prm["ffn_wd"][i]),
        w_ret=_b(w_in[:, OFF_RET:OFF_SSD]), w_ssd=_b(w_ssd), w_gdn=_b(w_gdn), w_merge=_b(w_in[:, OFF_MERGE:]),
        ssm_cw=prm["ssm_conv_w"][i], ssm_cb=prm["ssm_conv_b"][i][None, :],
        ssm_dtb=_lane_pad(prm["ssm_dt_bias"][i]), ssm_alog=_lane_pad(prm["ssm_a_log"][i]),
        ssm_dexp=jnp.repeat(prm["ssm_d"][i], M_HEADDIM)[None, :], ssm_nw=prm["ssm_norm_w"][i][None, :],
        gdn_cw=prm["gdn_conv_w"][i],
        gdn_dtb=_lane_pad(prm["gdn_dt_bias"][i]), gdn_alog=_lane_pad(prm["gdn_a_log"][i]),
        gdn_nw=prm["gdn_norm_w"][i][None, :],
        w_ro=_b(prm["w_ret_out"][i]), w_so=_b(prm["w_ssm_out"][i]), w_go=_b(prm["w_gdn_out"][i]),
        w_o=_b(prm["w_o"][i]), pe_proj=_b(prm["pe_proj"][i]), pe_gate=_b(prm["pe_gate"][i]),
    )


def _post_mix(x1, acts, p, i, w, tm):
    x2 = _merge(x1, *acts, w["w_merge"], w["w_ro"], w["w_so"], w["w_go"], w["w_o"], w["ln_g"], w["ln_b"], tm)
    return _ffn_pe(x2, p, i, w["wg"][1], w["wu"][1], w["wd"][1], w["pe_gate"], w["pe_proj"],
                   w["ln_g"], w["ln_b"], tm)


def kernel(x_prompt, x_sample, state_ret, state_ssm, state_ssm_conv, state_gdn, state_gdn_conv,
           p_prompt, p_sample, ln_g, ln_b, ffn_wg, ffn_wu, ffn_wd, w_in,
           ssm_conv_w, ssm_conv_b, ssm_dt_bias, ssm_a_log, ssm_d, ssm_norm_w,
           gdn_conv_w, gdn_dt_bias, gdn_a_log, gdn_norm_w,
           w_ret_out, w_ssm_out, w_gdn_out, w_o, pe_proj, pe_gate):
    prm = dict(ln_g=ln_g, ln_b=ln_b, ffn_wg=ffn_wg, ffn_wu=ffn_wu, ffn_wd=ffn_wd, w_in=w_in,
               ssm_conv_w=ssm_conv_w, ssm_conv_b=ssm_conv_b, ssm_dt_bias=ssm_dt_bias,
               ssm_a_log=ssm_a_log, ssm_d=ssm_d, ssm_norm_w=ssm_norm_w,
               gdn_conv_w=gdn_conv_w, gdn_dt_bias=gdn_dt_bias, gdn_a_log=gdn_a_log,
               gdn_norm_w=gdn_norm_w, w_ret_out=w_ret_out, w_ssm_out=w_ssm_out,
               w_gdn_out=w_gdn_out, w_o=w_o, pe_proj=pe_proj, pe_gate=pe_gate)
    nb, seq, _ = x_prompt.shape
    ns = x_sample.shape[0]
    depth = w_in.shape[0]
    tl = min(TL_SCAN, seq)
    tm = min(TM_DENSE, nb * seq)
    chunk = CHUNK if seq % CHUNK == 0 else seq

    cos_p, sin_p = _rope_tables(jnp.arange(seq, dtype=f32))
    cos_s, sin_s = _rope_tables(jnp.full((1,), PAST_LEN, f32))
    xp = x_prompt.reshape(nb * seq, D_MODEL)
    xs = x_sample.reshape(ns, D_MODEL)
    pp = p_prompt.reshape(depth, nb * seq, PLE_DIM)
    ps = p_sample.reshape(depth, ns, PLE_DIM)
    ssm_conv_flat = state_ssm_conv.reshape(depth, ns, (CONV_W - 1) * M_CONV_DIM)
    gdn_conv_flat = state_gdn_conv.reshape(depth, ns, (CONV_W - 1) * G_QKV)

    prompt_states, sample_states = [], []
    for i in range(depth):
        w = _layer_weights(i, prm)
        x1 = _ffn_ln(xp, w["wg"][0], w["wu"][0], w["wd"][0], w["ln_g"], w["ln_b"], tm)
        a_r, s_r = _ret_scan(x1, w["w_ret"], cos_p, sin_p, nb, seq, tl, chunk)
        a_s, s_s, c_s = _ssd_scan(x1, w["w_ssd"], w["ssm_cw"], w["ssm_cb"], w["ssm_dtb"], w["ssm_alog"],
                                  w["ssm_dexp"], w["ssm_nw"], nb, seq, tl, chunk)
        a_g, s_g, c_g = _gdn_scan(x1, w["w_gdn"], w["gdn_cw"], w["gdn_dtb"], w["gdn_alog"], w["gdn_nw"],
                                  nb, seq, tl, chunk)
        xp = _post_mix(x1, (a_r, a_s, a_g), pp, i, w, tm)
        prompt_states.append((s_r, s_s, c_s, s_g, c_g))
        y1 = _ffn_ln(xs, w["wg"][0], w["wu"][0], w["wd"][0], w["ln_g"], w["ln_b"], ns)
        b_r, t_r = _dec_ret(y1, w["w_ret"], cos_s, sin_s, state_ret, i)
        b_s, t_s, d_s = _dec_ssd(y1, w["w_ssd"], ssm_conv_flat[i], w["ssm_cw"], w["ssm_cb"], w["ssm_dtb"],
                                 w["ssm_alog"], w["ssm_dexp"], w["ssm_nw"], state_ssm, i)
        b_g, t_g, d_g = _dec_gdn(y1, w["w_gdn"], gdn_conv_flat[i], w["gdn_cw"], w["gdn_dtb"], w["gdn_alog"],
                                 w["gdn_nw"], state_gdn, i)
        xs = _post_mix(y1, (b_r, b_s, b_g), ps, i, w, ns)
        sample_states.append((t_r, t_s, d_s.reshape(ns, CONV_W - 1, M_CONV_DIM),
                              t_g, d_g.reshape(ns, CONV_W - 1, G_QKV)))

    stack = lambda sts: tuple(jnp.stack([s[j] for s in sts]) for j in range(5))
    return ((xp.reshape(nb, seq, D_MODEL), xs.reshape(ns, 1, D_MODEL))
            + stack(prompt_states) + stack(sample_states))
```

```python
import functools
import math

import numpy as np
import jax
import jax.numpy as jnp
from jax import lax
from jax.experimental import pallas as pl
from jax.experimental.pallas import tpu as pltpu

f32, bf16 = jnp.float32, jnp.bfloat16

D_MODEL = 1024
DEPTH = 2
PAST_LEN = 16384
R_HEADS, R_DK, R_DV = 4, 128, 256
R_QK, R_VAL = R_HEADS * R_DK, R_HEADS * R_DV
ROPE_BASE = 10000.0
M_HEADS, M_HEADDIM, M_GROUPS, M_STATE = 16, 64, 2, 128
M_INNER = M_HEADS * M_HEADDIM
M_CONV_DIM = M_INNER + 2 * M_GROUPS * M_STATE
M_HPG = M_HEADS // M_GROUPS
M_GW = M_HPG * M_HEADDIM
G_HEADS, G_DK, G_DV = 8, 128, 128
G_KEY, G_VAL = G_HEADS * G_DK, G_HEADS * G_DV
G_QKV = 2 * G_KEY + G_VAL
CONV_W = 4
FFN_DIM = 2048
PLE_DIM = 256
DN_ALPHA = (2 * DEPTH) ** 0.25
LN_EPS = 1e-5
NORM_EPS = 1e-6

_sizes = (R_QK, R_QK, R_VAL, R_VAL, M_INNER, M_CONV_DIM, M_HEADS, G_QKV, G_VAL, G_HEADS, G_HEADS,
          D_MODEL, D_MODEL, D_MODEL)
_off = np.concatenate([[0], np.cumsum(_sizes)]).tolist()
OFF_RET, OFF_SSD, OFF_GDN, OFF_MERGE, IN_DIM = _off[0], _off[4], _off[7], _off[11], _off[14]

LANES = 128
SUBLANES = 8
VMEM_LIMIT = 56 * 2 ** 20

TM_DENSE = 512
TL_SCAN = 512
CHUNK = 64
FFN_CHUNK = 512
DEC_BT = 8
PAD = LANES

LOG_GAMMA = [math.log1p(-(2.0 ** (-5.0 - h))) for h in range(R_HEADS)]


def _dot(a, b):
    return jnp.dot(a, b, preferred_element_type=f32)


def _dot_nt(a, b):
    return lax.dot_general(a, b, (((1,), (1,)), ((), ())), preferred_element_type=f32)


def _dot_tn(a, b):
    return lax.dot_general(a, b, (((0,), (0,)), ((), ())), preferred_element_type=f32)


def _b(x):
    return x.astype(bf16)


def _layer_norm(y, g, b):
    mu = jnp.mean(y, -1, keepdims=True)
    var = jnp.mean(jnp.square(y - mu), -1, keepdims=True)
    return (y - mu) * lax.rsqrt(var + LN_EPS) * g + b


def _rms(y):
    return y * lax.rsqrt(jnp.mean(jnp.square(y), -1, keepdims=True) + NORM_EPS)


def _const_spec(shape):
    return pl.BlockSpec(shape, lambda *_: (0,) * len(shape), pipeline_mode=pl.Buffered(1))


def _params(sem):
    return pltpu.CompilerParams(dimension_semantics=sem, vmem_limit_bytes=VMEM_LIMIT)


def _split3(x):
    a1 = _b(x)
    r1 = x - a1.astype(f32)
    a2 = _b(r1)
    a3 = _b(r1 - a2.astype(f32))
    return a1, a2, a3


def _cumsum_rows(tri_l, x):
    a1, a2, a3 = _split3(x)
    return _dot(tri_l, a1) + _dot(tri_l, a2) + _dot(tri_l, a3)


def _cumsum_cols(tri_u, xt):
    a1, a2, a3 = _split3(xt)
    return _dot(a1, tri_u) + _dot(a2, tri_u) + _dot(a3, tri_u)


def _chunk_iotas(c):
    ii = lax.broadcasted_iota(jnp.int32, (c, c), 0)
    jj = lax.broadcasted_iota(jnp.int32, (c, c), 1)
    return ii, jj


def _swiglu(x, wg_ref, wu_ref, wd_ref):
    xb = _b(x)
    acc = None
    for c in range(FFN_DIM // FFN_CHUNK):
        sl = slice(c * FFN_CHUNK, (c + 1) * FFN_CHUNK)
        a = jax.nn.silu(_dot(xb, wg_ref[:, sl])) * _dot(xb, wu_ref[:, sl])
        part = _dot(_b(a), wd_ref[sl, :])
        acc = part if acc is None else acc + part
    return acc


def _ffn_ln_kernel(x_ref, wg_ref, wu_ref, wd_ref, g_ref, b_ref, o_ref):
    x = x_ref[...]
    y = DN_ALPHA * x + 0.5 * _swiglu(x, wg_ref, wu_ref, wd_ref)
    o_ref[...] = _layer_norm(y, g_ref[0:1, :], b_ref[0:1, :])


def _ffn_ln(x, wg, wu, wd, g, b, tm):
    n = x.shape[0]
    return pl.pallas_call(
        _ffn_ln_kernel,
        out_shape=jax.ShapeDtypeStruct((n, D_MODEL), f32),
        grid=(n // tm,),
        in_specs=[pl.BlockSpec((tm, D_MODEL), lambda i: (i, 0)),
                  _const_spec(wg.shape), _const_spec(wu.shape), _const_spec(wd.shape),
                  _const_spec(g.shape), _const_spec(b.shape)],
        out_specs=pl.BlockSpec((tm, D_MODEL), lambda i: (i, 0)),
        compiler_params=_params(("parallel",)),
        name="ffn_ln",
    )(x, wg, wu, wd, g, b)


def _merge_kernel(x_ref, ar_ref, as_ref, ag_ref, wm_ref, wr_ref, ws_ref, wgd_ref, wo_ref, g_ref, b_ref, o_ref):
    x = x_ref[...]
    m = _dot(_b(x), wm_ref[...])
    yr = _dot(_b(ar_ref[...]), wr_ref[...])
    ys = _dot(_b(as_ref[...]), ws_ref[...])
    yg = _dot(_b(ag_ref[...]), wgd_ref[...])
    mixed = (jax.nn.sigmoid(m[:, 0:D_MODEL]) * yr + jax.nn.sigmoid(m[:, D_MODEL:2 * D_MODEL]) * ys
             + jax.nn.sigmoid(m[:, 2 * D_MODEL:3 * D_MODEL]) * yg)
    y = DN_ALPHA * x + _dot(_b(mixed), wo_ref[...])
    o_ref[...] = _layer_norm(y, g_ref[1:2, :], b_ref[1:2, :])


def _merge(x, ar, a_s, ag, wm, wr, ws, wgd, wo, g, b, tm):
    n = x.shape[0]
    tok = lambda i: (i, 0)
    return pl.pallas_call(
        _merge_kernel,
        out_shape=jax.ShapeDtypeStruct((n, D_MODEL), f32),
        grid=(n // tm,),
        in_specs=[pl.BlockSpec((tm, D_MODEL), tok)] * 4
                 + [_const_spec(w.shape) for w in (wm, wr, ws, wgd, wo, g, b)],
        out_specs=pl.BlockSpec((tm, D_MODEL), tok),
        compiler_params=_params(("parallel",)),
        name="merge",
    )(x, ar, a_s, ag, wm, wr, ws, wgd, wo, g, b)


def _ffn_pe_kernel(x_ref, p_ref, wg_ref, wu_ref, wd_ref, pg_ref, pp_ref, g_ref, b_ref, o_ref):
    x = x_ref[...]
    x = _layer_norm(DN_ALPHA * x + 0.5 * _swiglu(x, wg_ref, wu_ref, wd_ref), g_ref[2:3, :], b_ref[2:3, :])
    pe = jax.nn.sigmoid(_dot(_b(x), pg_ref[...])) * _dot(_b(p_ref[...]), pp_ref[...])
    o_ref[...] = _layer_norm(DN_ALPHA * x + pe, g_ref[3:4, :], b_ref[3:4, :])


def _ffn_pe(x, p, layer, wg, wu, wd, pg, pp, g, b, tm):
    n = x.shape[0]
    return pl.pallas_call(
        _ffn_pe_kernel,
        out_shape=jax.ShapeDtypeStruct((n, D_MODEL), f32),
        grid=(n // tm,),
        in_specs=[pl.BlockSpec((tm, D_MODEL), lambda i: (i, 0)),
                  pl.BlockSpec((None, tm, PLE_DIM), lambda i: (layer, i, 0))]
                 + [_const_spec(w.shape) for w in (wg, wu, wd, pg, pp, g, b)],
        out_specs=pl.BlockSpec((tm, D_MODEL), lambda i: (i, 0)),
        compiler_params=_params(("parallel",)),
        name="ffn_pe",
    )(x, p, wg, wu, wd, pg, pp, g, b)


def _rope_inplace(proj_ref, off, cos, sin, scale):
    t = proj_ref[:, off:off + R_DK]
    t = t * cos + pltpu.roll(t, R_DK // 2, axis=1) * sin
    if scale != 1.0:
        t = t * scale
    proj_ref[:, off:off + R_DK] = t


def _ret_scan_kernel(x_ref, w_ref, cos_ref, sin_ref, act_ref, st_ref, proj_ref, s_ref, *, tl, c):
    l = pl.program_id(1)

    @pl.when(l == 0)
    def _():
        s_ref[...] = jnp.zeros_like(s_ref)

    proj_ref[...] = _dot(_b(x_ref[...]), w_ref[...])
    cos, sin = cos_ref[...], sin_ref[...]
    for h in range(R_HEADS):
        _rope_inplace(proj_ref, h * R_DK, cos, sin, 1.0)
        _rope_inplace(proj_ref, R_QK + h * R_DK, cos, sin, R_DK ** -0.5)

    ii, jj = _chunk_iotas(c)
    dif = (ii - jj).astype(f32)
    ci = lax.broadcasted_iota(jnp.int32, (c, 1), 0).astype(f32)
    decay = [jnp.where(dif >= 0, jnp.exp(dif * lg), 0.0) for lg in LOG_GAMMA]
    e_col = [jnp.exp((ci + 1.0) * lg) for lg in LOG_GAMMA]
    w_col = [jnp.exp((c - 1.0 - ci) * lg) for lg in LOG_GAMMA]

    def chunk(ck, carry):
        rows = pl.ds(pl.multiple_of(ck * c, c), c)
        hd = range(R_HEADS)
        q = [proj_ref[rows, h * R_DK:(h + 1) * R_DK] for h in hd]
        k = [proj_ref[rows, R_QK + h * R_DK:R_QK + (h + 1) * R_DK] for h in hd]
        v = [_b(proj_ref[rows, 2 * R_QK + h * R_DV:2 * R_QK + (h + 1) * R_DV]) for h in hd]
        s = [s_ref[h] for h in hd]
        scores = [_dot_nt(_b(q[h]), _b(k[h])) * decay[h] for h in hd]
        inter = [_dot(_b(q[h] * e_col[h]), _b(s[h])) for h in hd]
        for h in hd:
            s_ref[h] = s[h] * math.exp(c * LOG_GAMMA[h]) + _dot_tn(_b(k[h] * w_col[h]), v[h])
        o = [_dot(_b(scores[h]), v[h]) + inter[h] for h in hd]
        for h in hd:
            mu = jnp.mean(o[h], -1, keepdims=True)
            var = jnp.mean(jnp.square(o[h] - mu), -1, keepdims=True)
            on = (o[h] - mu) * lax.rsqrt(var + LN_EPS)
            g = proj_ref[rows, 2 * R_QK + R_VAL + h * R_DV:2 * R_QK + R_VAL + (h + 1) * R_DV]
            act_ref[rows, h * R_DV:(h + 1) * R_DV] = _b(on * jax.nn.silu(g))
        return carry

    lax.fori_loop(0, tl // c, chunk, 0)

    @pl.when(l == pl.num_programs(1) - 1)
    def _():
        st_ref[0] = s_ref[...]


def _ret_scan(x, w, cos, sin, nb, nl_tok, tl, c):
    nl = nl_tok // tl
    tok = lambda b, l: (b * nl + l, 0)
    return pl.pallas_call(
        functools.partial(_ret_scan_kernel, tl=tl, c=c),
        out_shape=(jax.ShapeDtypeStruct((nb * nl_tok, R_VAL), bf16),
                   jax.ShapeDtypeStruct((nb, R_HEADS, R_DK, R_DV), f32)),
        grid=(nb, nl),
        in_specs=[pl.BlockSpec((tl, D_MODEL), tok), _const_spec(w.shape),
                  pl.BlockSpec((tl, R_DK), lambda b, l: (l, 0)),
                  pl.BlockSpec((tl, R_DK), lambda b, l: (l, 0))],
        out_specs=(pl.BlockSpec((tl, R_VAL), tok),
                   pl.BlockSpec((1, R_HEADS, R_DK, R_DV), lambda b, l: (b, 0, 0, 0))),
        scratch_shapes=[pltpu.VMEM((tl, w.shape[1]), f32),
                        pltpu.VMEM((R_HEADS, R_DK, R_DV), f32)],
        compiler_params=_params(("parallel", "arbitrary")),
        name="ret_scan",
    )(x, w, cos, sin)


def _causal_conv_tile(xbuf_ref, src_ref, src_off, width, cw_ref, cb_ref, dst_ref, tl, first):
    @pl.when(first)
    def _():
        xbuf_ref[0:SUBLANES, :] = jnp.zeros((SUBLANES, width), f32)

    xbuf_ref[SUBLANES:SUBLANES + tl, :] = src_ref[:, src_off:src_off + width]
    cblk = 512
    for cb in range(width // cblk):
        cs = slice(cb * cblk, (cb + 1) * cblk)
        acc = xbuf_ref[SUBLANES:SUBLANES + tl, cs] * cw_ref[CONV_W - 1:CONV_W, cs]
        for j in range(CONV_W - 1):
            r0 = SUBLANES - (CONV_W - 1) + j
            acc = acc + xbuf_ref[r0:r0 + tl, cs] * cw_ref[j:j + 1, cs]
        if cb_ref is not None:
            acc = acc + cb_ref[:, cs]
        dst_ref[:, cs] = jax.nn.silu(acc)
    xbuf_ref[0:SUBLANES, :] = xbuf_ref[tl:tl + SUBLANES, :]


def _expand_heads(v, e2_ref):
    hi = _b(v)
    lo = _b(v - hi.astype(f32))
    return _dot(jnp.concatenate([hi, lo], axis=1), e2_ref[...])


def _ssd_scan_kernel(x_ref, w_ref, cw_ref, cb_ref, dtb_ref, alog_ref, dexp_ref, nw_ref,
                     act_ref, st_ref, conv_ref,
                     proj_ref, xbuf_ref, xc_ref, cum_ref, xdt_ref, xw_ref, ee_ref, y_ref,
                     tri_ref, ones_ref, e2_ref, s_ref, *, tl, c):
    l = pl.program_id(1)
    log2c = c.bit_length() - 1

    @pl.when(l == 0)
    def _():
        s_ref[...] = jnp.zeros_like(s_ref)
        ti = lax.broadcasted_iota(jnp.int32, (tl, tl), 0)
        tj = lax.broadcasted_iota(jnp.int32, (tl, tl), 1)
        same = (ti >> log2c) == (tj >> log2c)
        ones_ref[...] = _b(same.astype(f32))
        tri_ref[...] = _b(jnp.where(ti >= tj, same.astype(f32), 0.0))
        er = lax.broadcasted_iota(jnp.int32, (2 * PAD, M_INNER), 0)
        el = lax.broadcasted_iota(jnp.int32, (2 * PAD, M_INNER), 1)
        e2_ref[...] = _b(((er & (PAD - 1)) == (el >> (M_HEADDIM.bit_length() - 1))).astype(f32))

    proj_ref[...] = _dot(_b(x_ref[...]), w_ref[...])
    _causal_conv_tile(xbuf_ref, proj_ref, M_INNER, M_CONV_DIM, cw_ref, cb_ref, xc_ref, tl, l == 0)
    dt = jax.nn.softplus(proj_ref[:, M_INNER + M_CONV_DIM:M_INNER + M_CONV_DIM + PAD] + dtb_ref[...])
    la = -jnp.exp(alog_ref[...]) * dt
    a1, a2, a3 = _split3(la)
    cum = _dot(tri_ref[...], a1) + _dot(tri_ref[...], a2) + _dot(tri_ref[...], a3)
    tot = _dot(ones_ref[...], a1) + _dot(ones_ref[...], a2) + _dot(ones_ref[...], a3)
    cum_ref[...] = cum
    xdt = xc_ref[:, 0:M_INNER] * _expand_heads(dt, e2_ref)
    xdt_ref[...] = xdt
    xw_ref[...] = xdt * _expand_heads(jnp.exp(tot - cum), e2_ref)
    ee_ref[...] = _expand_heads(jnp.exp(cum), e2_ref)

    half = c
    lane = lax.broadcasted_iota(jnp.int32, (c, 2 * half), 1)
    rowi = lax.broadcasted_iota(jnp.int32, (c, 2 * half), 0)
    left = lane < half
    causal2 = (lane & (half - 1)) <= rowi
    b_off, c_off = M_INNER, M_INNER + M_GROUPS * M_STATE
    gr = range(M_GROUPS)
    pairs = [(g, pp) for g in gr for pp in range(M_HPG // 2)]

    def chunk(ck, carry):
        r0 = pl.multiple_of(ck * c, c)
        rows = pl.ds(r0, c)
        cum_c = cum_ref[rows, :]
        cum_t = jnp.concatenate([cum_c, cum_c], axis=0).T
        bb = [_b(xc_ref[rows, b_off + g * M_STATE:b_off + (g + 1) * M_STATE]) for g in gr]
        cb = [_b(xc_ref[rows, c_off + g * M_STATE:c_off + (g + 1) * M_STATE]) for g in gr]
        g2 = [_dot_nt(cb[g], jnp.concatenate([bb[g], bb[g]], axis=0)) for g in gr]
        sg = [s_ref[g] for g in gr]
        inter = [ee_ref[rows, g * M_GW:(g + 1) * M_GW] * _dot(cb[g], _b(sg[g])) for g in gr]
        for g in gr:
            e_last = ee_ref[pl.ds(r0 + c - 1, 1), g * M_GW:(g + 1) * M_GW]
            s_ref[g] = sg[g] * e_last + _dot_tn(bb[g], _b(xw_ref[rows, g * M_GW:(g + 1) * M_GW]))
        a2s, rhs = [], []
        for g, pp in pairs:
            h0 = g * M_HPG + 2 * pp
            ls = slice(h0 * M_HEADDIM, (h0 + 2) * M_HEADDIM)
            colsel = jnp.where(left, cum_c[:, h0:h0 + 1], cum_c[:, h0 + 1:h0 + 2])
            rowsel = jnp.where(left[0:1, :], cum_t[h0:h0 + 1, :], cum_t[h0 + 1:h0 + 2, :])
            d2 = jnp.where(causal2, jnp.exp(colsel - rowsel), 0.0)
            a2s.append(_b(g2[g] * d2))
            xp = xdt_ref[rows, ls]
            rhs.append(_b(jnp.concatenate([jnp.where(left, xp, 0.0), jnp.where(left, 0.0, xp)], axis=0)))
        intra = [_dot(a, r) for a, r in zip(a2s, rhs)]
        for n, (g, pp) in enumerate(pairs):
            h0 = g * M_HPG + 2 * pp
            ls = slice(h0 * M_HEADDIM, (h0 + 2) * M_HEADDIM)
            y_ref[rows, ls] = (intra[n] + inter[g][:, pp * 2 * M_HEADDIM:(pp + 1) * 2 * M_HEADDIM]
                               + dexp_ref[:, ls] * xc_ref[rows, ls])
        return carry

    lax.fori_loop(0, tl // c, chunk, 0)

    y = y_ref[...] * jax.nn.silu(proj_ref[:, 0:M_INNER])
    for g in range(M_GROUPS):
        gs = slice(g * M_GW, (g + 1) * M_GW)
        act_ref[:, gs] = _b(_rms(y[:, gs]) * nw_ref[:, gs])

    @pl.when(l == pl.num_programs(1) - 1)
    def _():
        for h in range(M_HEADS):
            g, hh = divmod(h, M_HPG)
            st_ref[0, h] = s_ref[g][:, hh * M_HEADDIM:(hh + 1) * M_HEADDIM]
        conv_ref[0] = xbuf_ref[SUBLANES - (CONV_W - 1):SUBLANES, :]


def _ssd_scan(x, w, cw, cb, dtb, alog, dexp, nw, nb, nl_tok, tl, c):
    assert 2 * c == LANES and 2 * M_HEADDIM == LANES, "head pairs are packed into one 128-lane slab"
    nl = nl_tok // tl
    tok = lambda b, l: (b * nl + l, 0)
    return pl.pallas_call(
        functools.partial(_ssd_scan_kernel, tl=tl, c=c),
        out_shape=(jax.ShapeDtypeStruct((nb * nl_tok, M_INNER), bf16),
                   jax.ShapeDtypeStruct((nb, M_HEADS, M_STATE, M_HEADDIM), f32),
                   jax.ShapeDtypeStruct((nb, CONV_W - 1, M_CONV_DIM), f32)),
        grid=(nb, nl),
        in_specs=[pl.BlockSpec((tl, D_MODEL), tok)]
                 + [_const_spec(a.shape) for a in (w, cw, cb, dtb, alog, dexp, nw)],
        out_specs=(pl.BlockSpec((tl, M_INNER), tok),
                   pl.BlockSpec((1, M_HEADS, M_STATE, M_HEADDIM), lambda b, l: (b, 0, 0, 0)),
                   pl.BlockSpec((1, CONV_W - 1, M_CONV_DIM), lambda b, l: (b, 0, 0))),
        scratch_shapes=[pltpu.VMEM((tl, w.shape[1]), f32),
                        pltpu.VMEM((tl + SUBLANES, M_CONV_DIM), f32),
                        pltpu.VMEM((tl, M_CONV_DIM), f32),
                        pltpu.VMEM((tl, PAD), f32),
                        pltpu.VMEM((tl, M_INNER), f32),
                        pltpu.VMEM((tl, M_INNER), f32),
                        pltpu.VMEM((tl, M_INNER), f32),
                        pltpu.VMEM((tl, M_INNER), f32),
                        pltpu.VMEM((tl, tl), bf16),
                        pltpu.VMEM((tl, tl), bf16),
                        pltpu.VMEM((2 * PAD, M_INNER), bf16),
                        pltpu.VMEM((M_GROUPS, M_STATE, M_GW), f32)],
        compiler_params=_params(("parallel", "arbitrary")),
        name="ssd_scan",
    )(x, w, cw, cb, dtb, alog, dexp, nw)


def _unit_lower_inverse(lms, eye, c):
    ps = [eye - lm for lm in lms]
    ms = list(lms)
    k = 2
    while k < c:
        mbs = [_b(m) for m in ms]
        ms = [_dot(mb, mb) for mb in mbs]
        ps = [p + _dot(_b(p), _b(m)) for p, m in zip(ps, ms)]
        k *= 2
    return ps


def _gdn_scan_kernel(x_ref, w_ref, cw_ref, dtb_ref, alog_ref, nw_ref,
                     act_ref, st_ref, conv_ref,
                     proj_ref, xbuf_ref, qkv_ref, g_ref, beta_ref, s_ref, *, tl, c):
    l = pl.program_id(1)

    @pl.when(l == 0)
    def _():
        s_ref[...] = jnp.zeros_like(s_ref)

    proj_ref[...] = _dot(_b(x_ref[...]), w_ref[...])
    _causal_conv_tile(xbuf_ref, proj_ref, 0, G_QKV, cw_ref, None, qkv_ref, tl, l == 0)
    for h in range(G_HEADS):
        qs = slice(h * G_DK, (h + 1) * G_DK)
        ks = slice(G_KEY + h * G_DK, G_KEY + (h + 1) * G_DK)
        q = qkv_ref[:, qs]
        qkv_ref[:, qs] = q * lax.rsqrt(jnp.sum(jnp.square(q), -1, keepdims=True) + NORM_EPS) * (G_DK ** -0.5)
        k = qkv_ref[:, ks]
        qkv_ref[:, ks] = k * lax.rsqrt(jnp.sum(jnp.square(k), -1, keepdims=True) + NORM_EPS)
    ab = proj_ref[:, G_QKV + G_VAL:G_QKV + G_VAL + PAD]
    g_ref[...] = -jnp.exp(alog_ref[...]) * jax.nn.softplus(ab + dtb_ref[...])
    beta_ref[...] = jax.nn.sigmoid(ab)

    ii, jj = _chunk_iotas(c)
    tri_l = _b((ii >= jj).astype(f32))
    tri_u = _b((ii <= jj).astype(f32))
    eye = (ii == jj).astype(f32)

    def chunk(ck, carry):
        rows = pl.ds(pl.multiple_of(ck * c, c), c)
        g_c = g_ref[rows, :]
        cum = _cumsum_rows(tri_l, g_c)
        cum_t = _cumsum_cols(tri_u, g_c.T)
        beta_c = beta_ref[rows, :]
        last = cum[c - 1:c, :]
        hd = range(G_HEADS)
        q = [qkv_ref[rows, h * G_DK:(h + 1) * G_DK] for h in hd]
        k = [qkv_ref[rows, G_KEY + h * G_DK:G_KEY + (h + 1) * G_DK] for h in hd]
        v = [qkv_ref[rows, 2 * G_KEY + h * G_DV:2 * G_KEY + (h + 1) * G_DV] for h in hd]
        b_h = [beta_c[:, G_HEADS + h:G_HEADS + h + 1] for h in hd]
        cum_h = [cum[:, h:h + 1] for h in hd]
        last_h = [last[:, h:h + 1] for h in hd]
        dmat = [jnp.where(ii >= jj, jnp.exp(cum_h[h] - cum_t[h:h + 1, :]), 0.0) for h in hd]
        kb = [k[h] * b_h[h] for h in hd]
        k16 = [_b(k[h]) for h in hd]
        lm = [jnp.where(ii > jj, _dot_nt(_b(kb[h]), k16[h]) * dmat[h], 0.0) for h in hd]
        tinv = _unit_lower_inverse(lm, eye, c)
        e = [jnp.exp(cum_h[h]) for h in hd]
        s = [s_ref[h] for h in hd]
        s16 = [_b(s[h]) for h in hd]
        rhs = [v[h] * b_h[h] - _dot(_b(kb[h] * e[h]), s16[h]) for h in hd]
        u = [_b(_dot(_b(tinv[h]), _b(rhs[h]))) for h in hd]
        attn = [_dot_nt(_b(q[h]), k16[h]) * dmat[h] for h in hd]
        o = [_dot(_b(q[h] * e[h]), s16[h]) + _dot(_b(attn[h]), u[h]) for h in hd]
        for h in hd:
            s_ref[h] = s[h] * jnp.exp(last_h[h]) + _dot_tn(_b(k[h] * jnp.exp(last_h[h] - cum_h[h])), u[h])
        for h in hd:
            gz = proj_ref[rows, G_QKV + h * G_DV:G_QKV + (h + 1) * G_DV]
            act_ref[rows, h * G_DV:(h + 1) * G_DV] = _b(_rms(o[h]) * nw_ref[...] * jax.nn.silu(gz))
        return carry

    lax.fori_loop(0, tl // c, chunk, 0)

    @pl.when(l == pl.num_programs(1) - 1)
    def _():
        st_ref[0] = s_ref[...]
        conv_ref[0] = xbuf_ref[SUBLANES - (CONV_W - 1):SUBLANES, :]


def _gdn_scan(x, w, cw, dtb, alog, nw, nb, nl_tok, tl, c):
    nl = nl_tok // tl
    tok = lambda b, l: (b * nl + l, 0)
    return pl.pallas_call(
        functools.partial(_gdn_scan_kernel, tl=tl, c=c),
        out_shape=(jax.ShapeDtypeStruct((nb * nl_tok, G_VAL), bf16),
                   jax.ShapeDtypeStruct((nb, G_HEADS, G_DK, G_DV), f32),
                   jax.ShapeDtypeStruct((nb, CONV_W - 1, G_QKV), f32)),
        grid=(nb, nl),
        in_specs=[pl.BlockSpec((tl, D_MODEL), tok)]
                 + [_const_spec(a.shape) for a in (w, cw, dtb, alog, nw)],
        out_specs=(pl.BlockSpec((tl, G_VAL), tok),
                   pl.BlockSpec((1, G_HEADS, G_DK, G_DV), lambda b, l: (b, 0, 0, 0)),
                   pl.BlockSpec((1, CONV_W - 1, G_QKV), lambda b, l: (b, 0, 0))),
        scratch_shapes=[pltpu.VMEM((tl, w.shape[1]), f32),
                        pltpu.VMEM((tl + SUBLANES, G_QKV), f32),
                        pltpu.VMEM((tl, G_QKV), f32),
                        pltpu.VMEM((tl, PAD), f32),
                        pltpu.VMEM((tl, PAD), f32),
                        pltpu.VMEM((G_HEADS, G_DK, G_DV), f32)],
        compiler_params=_params(("parallel", "arbitrary")),
        name="gdn_scan",
    )(x, w, cw, dtb, alog, nw)


def _token_lanes_to_front(src_ref, dst_ref, i):
    n = src_ref.shape[1]
    dst_ref[...] = pltpu.roll(src_ref[...], (n - i * DEC_BT) % n, axis=1)


def _col_sum(x):
    return jnp.sum(x, axis=0, keepdims=True)


def _conv_step(cst_ref, x_new, width, cw_ref, cb_ref, conv_out_ref):
    acc = x_new * cw_ref[CONV_W - 1:CONV_W, :]
    for j in range(CONV_W - 1):
        acc = acc + cst_ref[:, j * width:(j + 1) * width] * cw_ref[j:j + 1, :]
    if cb_ref is not None:
        acc = acc + cb_ref[...]
    conv_out_ref[:, 0:(CONV_W - 2) * width] = cst_ref[:, width:(CONV_W - 1) * width]
    conv_out_ref[:, (CONV_W - 2) * width:(CONV_W - 1) * width] = x_new
    return jax.nn.silu(acc)


def _dec_ret_kernel(x_ref, w_ref, cos_ref, sin_ref, s_in, act_ref, s_out,
                    qt_ref, kt_ref, vg_ref, qs_ref, ks_ref, o_ref):
    i = pl.program_id(0)

    @pl.when(i == 0)
    def _():
        proj = _dot(_b(x_ref[...]), w_ref[...])
        cos, sin = cos_ref[...], sin_ref[...]
        for h in range(R_HEADS):
            for off, scale, dst in ((h * R_DK, 1.0, qt_ref), (R_QK + h * R_DK, R_DK ** -0.5, kt_ref)):
                t = proj[:, off:off + R_DK]
                t = (t * cos + pltpu.roll(t, R_DK // 2, axis=1) * sin) * scale
                dst[h * R_DK:(h + 1) * R_DK, :] = t.T
        vg_ref[...] = proj[:, 2 * R_QK:]

    _token_lanes_to_front(qt_ref, qs_ref, i)
    _token_lanes_to_front(kt_ref, ks_ref, i)
    vg = vg_ref[pl.ds(pl.multiple_of(i * DEC_BT, DEC_BT), DEC_BT), :]
    for j in range(DEC_BT):
        for h in range(R_HEADS):
            gam = math.exp(LOG_GAMMA[h])
            qc = qs_ref[h * R_DK:(h + 1) * R_DK, j:j + 1]
            kc = ks_ref[h * R_DK:(h + 1) * R_DK, j:j + 1]
            v_row = vg[j:j + 1, h * R_DV:(h + 1) * R_DV]
            s = s_in[j, h]
            o_ref[j:j + 1, h * R_DV:(h + 1) * R_DV] = gam * _col_sum(qc * s) + _col_sum(qc * kc) * v_row
            s_out[j, h] = s * gam + kc * v_row
    for h in range(R_HEADS):
        o = o_ref[:, h * R_DV:(h + 1) * R_DV]
        mu = jnp.mean(o, -1, keepdims=True)
        var = jnp.mean(jnp.square(o - mu), -1, keepdims=True)
        g = vg[:, R_VAL + h * R_DV:R_VAL + (h + 1) * R_DV]
        act_ref[:, h * R_DV:(h + 1) * R_DV] = (o - mu) * lax.rsqrt(var + LN_EPS) * jax.nn.silu(g)


def _dec_ret(x, w, cos, sin, state, layer):
    n = x.shape[0]
    sblk = (None, DEC_BT, R_HEADS, R_DK, R_DV)
    return pl.pallas_call(
        _dec_ret_kernel,
        out_shape=(jax.ShapeDtypeStruct((n, R_VAL), f32),
                   jax.ShapeDtypeStruct((n, R_HEADS, R_DK, R_DV), f32)),
        grid=(n // DEC_BT,),
        in_specs=[_const_spec(x.shape), _const_spec(w.shape), _const_spec(cos.shape), _const_spec(sin.shape),
                  pl.BlockSpec(sblk, lambda i: (layer, i, 0, 0, 0))],
        out_specs=(pl.BlockSpec((DEC_BT, R_VAL), lambda i: (i, 0)),
                   pl.BlockSpec(sblk[1:], lambda i: (i, 0, 0, 0))),
        scratch_shapes=[pltpu.VMEM((R_QK, n), f32), pltpu.VMEM((R_QK, n), f32),
                        pltpu.VMEM((n, 2 * R_VAL), f32),
                        pltpu.VMEM((R_QK, n), f32), pltpu.VMEM((R_QK, n), f32),
                        pltpu.VMEM((DEC_BT, R_VAL), f32)],
        compiler_params=_params(("arbitrary",)),
        name="dec_ret",
    )(x, w, cos, sin, state)


def _dec_ssd_kernel(x_ref, w_ref, cst_ref, cw_ref, cb_ref, dtb_ref, alog_ref, dexp_ref, nw_ref, s_in,
                    act_ref, s_out, conv_out_ref,
                    xs_ref, z_ref, bt_ref, ct_ref, dt_ref, ela_ref, bs_ref, cs_ref, o_ref):
    i = pl.program_id(0)

    @pl.when(i == 0)
    def _():
        proj = _dot(_b(x_ref[...]), w_ref[...])
        xc = _conv_step(cst_ref, proj[:, M_INNER:M_INNER + M_CONV_DIM], M_CONV_DIM, cw_ref, cb_ref, conv_out_ref)
        xs_ref[...] = xc[:, 0:M_INNER]
        z_ref[...] = proj[:, 0:M_INNER]
        bt_ref[...] = xc[:, M_INNER:M_INNER + M_GROUPS * M_STATE].T
        ct_ref[...] = xc[:, M_INNER + M_GROUPS * M_STATE:M_CONV_DIM].T
        dt = jax.nn.softplus(proj[:, M_INNER + M_CONV_DIM:M_INNER + M_CONV_DIM + PAD] + dtb_ref[...])
        dt_ref[...] = dt
        ela_ref[...] = jnp.exp(-jnp.exp(alog_ref[...]) * dt)

    _token_lanes_to_front(bt_ref, bs_ref, i)
    _token_lanes_to_front(ct_ref, cs_ref, i)
    rows = pl.ds(pl.multiple_of(i * DEC_BT, DEC_BT), DEC_BT)
    xs8, dt8, ela8 = xs_ref[rows, :], dt_ref[rows, :], ela_ref[rows, :]
    for j in range(DEC_BT):
        for g in range(M_GROUPS):
            bc = bs_ref[g * M_STATE:(g + 1) * M_STATE, j:j + 1]
            cc = cs_ref[g * M_STATE:(g + 1) * M_STATE, j:j + 1]
            cb = _col_sum(cc * bc)
            for hh in range(M_HPG):
                h = g * M_HPG + hh
                hs = slice(h * M_HEADDIM, (h + 1) * M_HEADDIM)
                x_row = xs8[j:j + 1, hs]
                xdt = x_row * dt8[j:j + 1, h:h + 1]
                eh = ela8[j:j + 1, h:h + 1]
                s = s_in[j, h]
                o_ref[j:j + 1, hs] = cb * xdt + eh * _col_sum(cc * s) + dexp_ref[:, hs] * x_row
                s_out[j, h] = s * eh + bc * xdt
    y = o_ref[...] * jax.nn.silu(z_ref[rows, :])
    for g in range(M_GROUPS):
        gs = slice(g * M_GW, (g + 1) * M_GW)
        act_ref[:, gs] = _rms(y[:, gs]) * nw_ref[:, gs]


def _dec_ssd(x, w, cst, cw, cb, dtb, alog, dexp, nw, state, layer):
    n = x.shape[0]
    sblk = (None, DEC_BT, M_HEADS, M_STATE, M_HEADDIM)
    consts = (x, w, cst, cw, cb, dtb, alog, dexp, nw)
    return pl.pallas_call(
        _dec_ssd_kernel,
        out_shape=(jax.ShapeDtypeStruct((n, M_INNER), f32),
                   jax.ShapeDtypeStruct((n, M_HEADS, M_STATE, M_HEADDIM), f32),
                   jax.ShapeDtypeStruct(cst.shape, f32)),
        grid=(n // DEC_BT,),
        in_specs=[_const_spec(a.shape) for a in consts] + [pl.BlockSpec(sblk, lambda i: (layer, i, 0, 0, 0))],
        out_specs=(pl.BlockSpec((DEC_BT, M_INNER), lambda i: (i, 0)),
                   pl.BlockSpec(sblk[1:], lambda i: (i, 0, 0, 0)),
                   pl.BlockSpec(cst.shape, lambda i: (0, 0))),
        scratch_shapes=[pltpu.VMEM((n, M_INNER), f32), pltpu.VMEM((n, M_INNER), f32),
                        pltpu.VMEM((M_GROUPS * M_STATE, n), f32), pltpu.VMEM((M_GROUPS * M_STATE, n), f32),
                        pltpu.VMEM((n, PAD), f32), pltpu.VMEM((n, PAD), f32),
                        pltpu.VMEM((M_GROUPS * M_STATE, n), f32), pltpu.VMEM((M_GROUPS * M_STATE, n), f32),
                        pltpu.VMEM((DEC_BT, M_INNER), f32)],
        compiler_params=_params(("arbitrary",)),
        name="dec_ssd",
    )(*consts, state)


def _dec_gdn_kernel(x_ref, w_ref, cst_ref, cw_ref, dtb_ref, alog_ref, nw_ref, s_in,
                    act_ref, s_out, conv_out_ref,
                    qt_ref, kt_ref, v_ref, gz_ref, eg_ref, beta_ref, qs_ref, ks_ref, o_ref):
    i = pl.program_id(0)

    @pl.when(i == 0)
    def _():
        proj = _dot(_b(x_ref[...]), w_ref[...])
        qkv = _conv_step(cst_ref, proj[:, 0:G_QKV], G_QKV, cw_ref, None, conv_out_ref)
        for h in range(G_HEADS):
            q = qkv[:, h * G_DK:(h + 1) * G_DK]
            q = q * lax.rsqrt(jnp.sum(jnp.square(q), -1, keepdims=True) + NORM_EPS) * (G_DK ** -0.5)
            qt_ref[h * G_DK:(h + 1) * G_DK, :] = q.T
            k = qkv[:, G_KEY + h * G_DK:G_KEY + (h + 1) * G_DK]
            k = k * lax.rsqrt(jnp.sum(jnp.square(k), -1, keepdims=True) + NORM_EPS)
            kt_ref[h * G_DK:(h + 1) * G_DK, :] = k.T
        v_ref[...] = qkv[:, 2 * G_KEY:]
        gz_ref[...] = proj[:, G_QKV:G_QKV + G_VAL]
        ab = proj[:, G_QKV + G_VAL:G_QKV + G_VAL + PAD]
        eg_ref[...] = jnp.exp(-jnp.exp(alog_ref[...]) * jax.nn.softplus(ab + dtb_ref[...]))
        beta_ref[...] = jax.nn.sigmoid(ab)

    _token_lanes_to_front(qt_ref, qs_ref, i)
    _token_lanes_to_front(kt_ref, ks_ref, i)
    rows = pl.ds(pl.multiple_of(i * DEC_BT, DEC_BT), DEC_BT)
    v8, eg8, beta8 = v_ref[rows, :], eg_ref[rows, :], beta_ref[rows, :]
    for j in range(DEC_BT):
        for h in range(G_HEADS):
            hs = slice(h * G_DV, (h + 1) * G_DV)
            qc = qs_ref[h * G_DK:(h + 1) * G_DK, j:j + 1]
            kc = ks_ref[h * G_DK:(h + 1) * G_DK, j:j + 1]
            bh = beta8[j:j + 1, G_HEADS + h:G_HEADS + h + 1]
            eg = eg8[j:j + 1, h:h + 1]
            s = s_in[j, h]
            u = v8[j:j + 1, hs] * bh - (bh * eg) * _col_sum(kc * s)
            o_ref[j:j + 1, hs] = eg * _col_sum(qc * s) + _col_sum(qc * kc) * u
            s_out[j, h] = s * eg + kc * u
    gz8 = gz_ref[rows, :]
    for h in range(G_HEADS):
        hs = slice(h * G_DV, (h + 1) * G_DV)
        act_ref[:, hs] = _rms(o_ref[:, hs]) * nw_ref[...] * jax.nn.silu(gz8[:, hs])


def _dec_gdn(x, w, cst, cw, dtb, alog, nw, state, layer):
    n = x.shape[0]
    sblk = (None, DEC_BT, G_HEADS, G_DK, G_DV)
    consts = (x, w, cst, cw, dtb, alog, nw)
    return pl.pallas_call(
        _dec_gdn_kernel,
        out_shape=(jax.ShapeDtypeStruct((n, G_VAL), f32),
                   jax.ShapeDtypeStruct((n, G_HEADS, G_DK, G_DV), f32),
                   jax.ShapeDtypeStruct(cst.shape, f32)),
        grid=(n // DEC_BT,),
        in_specs=[_const_spec(a.shape) for a in consts] + [pl.BlockSpec(sblk, lambda i: (layer, i, 0, 0, 0))],
        out_specs=(pl.BlockSpec((DEC_BT, G_VAL), lambda i: (i, 0)),
                   pl.BlockSpec(sblk[1:], lambda i: (i, 0, 0, 0)),
                   pl.BlockSpec(cst.shape, lambda i: (0, 0))),
        scratch_shapes=[pltpu.VMEM((G_KEY, n), f32), pltpu.VMEM((G_KEY, n), f32),
                        pltpu.VMEM((n, G_VAL), f32), pltpu.VMEM((n, G_VAL), f32),
                        pltpu.VMEM((n, PAD), f32), pltpu.VMEM((n, PAD), f32),
                        pltpu.VMEM((G_KEY, n), f32), pltpu.VMEM((G_KEY, n), f32),
                        pltpu.VMEM((DEC_BT, G_VAL), f32)],
        compiler_params=_params(("arbitrary",)),
        name="dec_gdn",
    )(*consts, state)


def _rope_tables(pos):
    half = R_DK // 2
    inv = ROPE_BASE ** (-jnp.arange(half, dtype=f32) / half)
    ang = pos[:, None] * inv[None, :]
    cos, sin = jnp.cos(ang), jnp.sin(ang)
    return jnp.concatenate([cos, cos], axis=1), jnp.concatenate([-sin, sin], axis=1)


def _lane_pad(v, start=0):
    return jnp.zeros((1, PAD), f32).at[0, start:start + v.shape[0]].set(v)


def _layer_weights(i, prm):
    w_in = prm["w_in"][i]
    zpad = jnp.zeros((D_MODEL, PAD - M_HEADS), f32)
    w_ssd = jnp.concatenate([w_in[:, OFF_SSD:OFF_GDN], zpad], axis=1)
    w_gdn = jnp.concatenate([w_in[:, OFF_GDN:OFF_MERGE], jnp.zeros((D_MODEL, PAD - 2 * G_HEADS), f32)], axis=1)
    return dict(
        ln_g=prm["ln_g"][i], ln_b=prm["ln_b"][i],
        wg=_b(prm["ffn_wg"][i]), wu=_b(prm["ffn_wu"][i]), wd=_b(prm["ffn_wd"][i]),
        w_ret=_b(w_in[:, OFF_RET:OFF_SSD]), w_ssd=_b(w_ssd), w_gdn=_b(w_gdn), w_merge=_b(w_in[:, OFF_MERGE:]),
        ssm_cw=prm["ssm_conv_w"][i], ssm_cb=prm["ssm_conv_b"][i][None, :],
        ssm_dtb=_lane_pad(prm["ssm_dt_bias"][i]), ssm_alog=_lane_pad(prm["ssm_a_log"][i]),
        ssm_dexp=jnp.repeat(prm["ssm_d"][i], M_HEADDIM)[None, :], ssm_nw=prm["ssm_norm_w"][i][None, :],
        gdn_cw=prm["gdn_conv_w"][i],
        gdn_dtb=_lane_pad(prm["gdn_dt_bias"][i]), gdn_alog=_lane_pad(prm["gdn_a_log"][i]),
        gdn_nw=prm["gdn_norm_w"][i][None, :],
        w_ro=_b(prm["w_ret_out"][i]), w_so=_b(prm["w_ssm_out"][i]), w_go=_b(prm["w_gdn_out"][i]),
        w_o=_b(prm["w_o"][i]), pe_proj=_b(prm["pe_proj"][i]), pe_gate=_b(prm["pe_gate"][i]),
    )


def _post_mix(x1, acts, p, i, w, tm):
    x2 = _merge(x1, *acts, w["w_merge"], w["w_ro"], w["w_so"], w["w_go"], w["w_o"], w["ln_g"], w["ln_b"], tm)
    return _ffn_pe(x2, p, i, w["wg"][1], w["wu"][1], w["wd"][1], w["pe_gate"], w["pe_proj"],
                   w["ln_g"], w["ln_b"], tm)


def kernel(x_prompt, x_sample, state_ret, state_ssm, state_ssm_conv, state_gdn, state_gdn_conv,
           p_prompt, p_sample, ln_g, ln_b, ffn_wg, ffn_wu, ffn_wd, w_in,
           ssm_conv_w, ssm_conv_b, ssm_dt_bias, ssm_a_log, ssm_d, ssm_norm_w,
           gdn_conv_w, gdn_dt_bias, gdn_a_log, gdn_norm_w,
           w_ret_out, w_ssm_out, w_gdn_out, w_o, pe_proj, pe_gate):
    prm = dict(ln_g=ln_g, ln_b=ln_b, ffn_wg=ffn_wg, ffn_wu=ffn_wu, ffn_wd=ffn_wd, w_in=w_in,
               ssm_conv_w=ssm_conv_w, ssm_conv_b=ssm_conv_b, ssm_dt_bias=ssm_dt_bias,
               ssm_a_log=ssm_a_log, ssm_d=ssm_d, ssm_norm_w=ssm_norm_w,
               gdn_conv_w=gdn_conv_w, gdn_dt_bias=gdn_dt_bias, gdn_a_log=gdn_a_log,
               gdn_norm_w=gdn_norm_w, w_ret_out=w_ret_out, w_ssm_out=w_ssm_out,
               w_gdn_out=w_gdn_out, w_o=w_o, pe_proj=pe_proj, pe_gate=pe_gate)
    nb, seq, _ = x_prompt.shape
    ns = x_sample.shape[0]
    depth = w_in.shape[0]
    tl = min(TL_SCAN, seq)
    tm = min(TM_DENSE, nb * seq)
    chunk = CHUNK if seq % CHUNK == 0 else seq

    cos_p, sin_p = _rope_tables(jnp.arange(seq, dtype=f32))
    cos_s, sin_s = _rope_tables(jnp.full((1,), PAST_LEN, f32))
    xp = x_prompt.reshape(nb * seq, D_MODEL)
    xs = x_sample.reshape(ns, D_MODEL)
    pp = p_prompt.reshape(depth, nb * seq, PLE_DIM)
    ps = p_sample.reshape(depth, ns, PLE_DIM)
    ssm_conv_flat = state_ssm_conv.reshape(depth, ns, (CONV_W - 1) * M_CONV_DIM)
    gdn_conv_flat = state_gdn_conv.reshape(depth, ns, (CONV_W - 1) * G_QKV)

    prompt_states, sample_states = [], []
    for i in range(depth):
        w = _layer_weights(i, prm)
        x1 = _ffn_ln(xp, w["wg"][0], w["wu"][0], w["wd"][0], w["ln_g"], w["ln_b"], tm)
        a_r, s_r = _ret_scan(x1, w["w_ret"], cos_p, sin_p, nb, seq, tl, chunk)
        a_s, s_s, c_s = _ssd_scan(x1, w["w_ssd"], w["ssm_cw"], w["ssm_cb"], w["ssm_dtb"], w["ssm_alog"],
                                  w["ssm_dexp"], w["ssm_nw"], nb, seq, tl, chunk)
        a_g, s_g, c_g = _gdn_scan(x1, w["w_gdn"], w["gdn_cw"], w["gdn_dtb"], w["gdn_alog"], w["gdn_nw"],
                                  nb, seq, tl, chunk)
        xp = _post_mix(x1, (a_r, a_s, a_g), pp, i, w, tm)
        prompt_states.append((s_r, s_s, c_s, s_g, c_g))
        y1 = _ffn_ln(xs, w["wg"][0], w["wu"][0], w["wd"][0], w["ln_g"], w["ln_b"], ns)
        b_r, t_r = _dec_ret(y1, w["w_ret"], cos_s, sin_s, state_ret, i)
        b_s, t_s, d_s = _dec_ssd(y1, w["w_ssd"], ssm_conv_flat[i], w["ssm_cw"], w["ssm_cb"], w["ssm_dtb"],
                                 w["ssm_alog"], w["ssm_dexp"], w["ssm_nw"], state_ssm, i)
        b_g, t_g, d_g = _dec_gdn(y1, w["w_gdn"], gdn_conv_flat[i], w["gdn_cw"], w["gdn_dtb"], w["gdn_alog"],
                                 w["gdn_nw"], state_gdn, i)
        xs = _post_mix(y1, (b_r, b_s, b_g), ps, i, w, ns)
        sample_states.append((t_r, t_s, d_s.reshape(ns, CONV_W - 1, M_CONV_DIM),
                              t_g, d_g.reshape(ns, CONV_W - 1, G_QKV)))

    stack = lambda sts: tuple(jnp.stack([s[j] for s in sts]) for j in range(5))
    return ((xp.reshape(nb, seq, D_MODEL), xs.reshape(ns, 1, D_MODEL))
            + stack(prompt_states) + stack(sample_states))
```

```python
import functools
import math

import numpy as np
import jax
import jax.numpy as jnp
from jax import lax
from jax.experimental import pallas as pl
from jax.experimental.pallas import tpu as pltpu

f32, bf16 = jnp.float32, jnp.bfloat16

D_MODEL = 1024
DEPTH = 2
PAST_LEN = 16384
R_HEADS, R_DK, R_DV = 4, 128, 256
R_QK, R_VAL = R_HEADS * R_DK, R_HEADS * R_DV
ROPE_BASE = 10000.0
M_HEADS, M_HEADDIM, M_GROUPS, M_STATE = 16, 64, 2, 128
M_INNER = M_HEADS * M_HEADDIM
M_CONV_DIM = M_INNER + 2 * M_GROUPS * M_STATE
M_HPG = M_HEADS // M_GROUPS
M_GW = M_HPG * M_HEADDIM
G_HEADS, G_DK, G_DV = 8, 128, 128
G_KEY, G_VAL = G_HEADS * G_DK, G_HEADS * G_DV
G_QKV = 2 * G_KEY + G_VAL
CONV_W = 4
FFN_DIM = 2048
PLE_DIM = 256
DN_ALPHA = (2 * DEPTH) ** 0.25
LN_EPS = 1e-5
NORM_EPS = 1e-6

_sizes = (R_QK, R_QK, R_VAL, R_VAL, M_INNER, M_CONV_DIM, M_HEADS, G_QKV, G_VAL, G_HEADS, G_HEADS,
          D_MODEL, D_MODEL, D_MODEL)
_off = np.concatenate([[0], np.cumsum(_sizes)]).tolist()
OFF_RET, OFF_SSD, OFF_GDN, OFF_MERGE, IN_DIM = _off[0], _off[4], _off[7], _off[11], _off[14]

LANES = 128
SUBLANES = 8
VMEM_LIMIT = 56 * 2 ** 20

TM_DENSE = 512
TL_SCAN = 512
CHUNK = 64
FFN_CHUNK = 512
DEC_BT = 8
RET_UNROLL = 4
GDN_PRE_UNROLL = 4
GDN_SCAN_UNROLL = 4
PAD = LANES

LOG_GAMMA = [math.log1p(-(2.0 ** (-5.0 - h))) for h in range(R_HEADS)]


def _dot(a, b):
    return jnp.dot(a, b, preferred_element_type=f32)


def _dot_nt(a, b):
    return lax.dot_general(a, b, (((1,), (1,)), ((), ())), preferred_element_type=f32)


def _dot_tn(a, b):
    return lax.dot_general(a, b, (((0,), (0,)), ((), ())), preferred_element_type=f32)


def _b(x):
    return x.astype(bf16)


def _layer_norm(y, g, b):
    mu = jnp.mean(y, -1, keepdims=True)
    var = jnp.mean(jnp.square(y - mu), -1, keepdims=True)
    return (y - mu) * lax.rsqrt(var + LN_EPS) * g + b


def _rms(y):
    return y * lax.rsqrt(jnp.mean(jnp.square(y), -1, keepdims=True) + NORM_EPS)


def _const_spec(shape):
    return pl.BlockSpec(shape, lambda *_: (0,) * len(shape), pipeline_mode=pl.Buffered(1))


def _params(sem):
    return pltpu.CompilerParams(dimension_semantics=sem, vmem_limit_bytes=VMEM_LIMIT)


def _split3(x):
    a1 = _b(x)
    r1 = x - a1.astype(f32)
    a2 = _b(r1)
    a3 = _b(r1 - a2.astype(f32))
    return a1, a2, a3


def _chunk_iotas(c):
    ii = lax.broadcasted_iota(jnp.int32, (c, c), 0)
    jj = lax.broadcasted_iota(jnp.int32, (c, c), 1)
    return ii, jj


def _swiglu(x, wg_ref, wu_ref, wd_ref):
    xb = _b(x)
    acc = None
    for c in range(FFN_DIM // FFN_CHUNK):
        sl = slice(c * FFN_CHUNK, (c + 1) * FFN_CHUNK)
        a = jax.nn.silu(_dot(xb, wg_ref[:, sl])) * _dot(xb, wu_ref[:, sl])
        part = _dot(_b(a), wd_ref[sl, :])
        acc = part if acc is None else acc + part
    return acc


def _ffn_ln_kernel(x_ref, wg_ref, wu_ref, wd_ref, g_ref, b_ref, o_ref):
    x = x_ref[...]
    y = DN_ALPHA * x + 0.5 * _swiglu(x, wg_ref, wu_ref, wd_ref)
    o_ref[...] = _layer_norm(y, g_ref[0:1, :], b_ref[0:1, :])


def _ffn_ln(x, wg, wu, wd, g, b, tm):
    n = x.shape[0]
    return pl.pallas_call(
        _ffn_ln_kernel,
        out_shape=jax.ShapeDtypeStruct((n, D_MODEL), f32),
        grid=(n // tm,),
        in_specs=[pl.BlockSpec((tm, D_MODEL), lambda i: (i, 0)),
                  _const_spec(wg.shape), _const_spec(wu.shape), _const_spec(wd.shape),
                  _const_spec(g.shape), _const_spec(b.shape)],
        out_specs=pl.BlockSpec((tm, D_MODEL), lambda i: (i, 0)),
        compiler_params=_params(("parallel",)),
        name="ffn_ln",
    )(x, wg, wu, wd, g, b)


def _merge_kernel(x_ref, ar_ref, as_ref, ag_ref, wm_ref, wr_ref, ws_ref, wgd_ref, wo_ref, g_ref, b_ref, o_ref):
    x = x_ref[...]
    m = _dot(_b(x), wm_ref[...])
    yr = _dot(_b(ar_ref[...]), wr_ref[...])
    ys = _dot(_b(as_ref[...]), ws_ref[...])
    yg = _dot(_b(ag_ref[...]), wgd_ref[...])
    mixed = (jax.nn.sigmoid(m[:, 0:D_MODEL]) * yr + jax.nn.sigmoid(m[:, D_MODEL:2 * D_MODEL]) * ys
             + jax.nn.sigmoid(m[:, 2 * D_MODEL:3 * D_MODEL]) * yg)
    y = DN_ALPHA * x + _dot(_b(mixed), wo_ref[...])
    o_ref[...] = _layer_norm(y, g_ref[1:2, :], b_ref[1:2, :])


def _merge(x, ar, a_s, ag, wm, wr, ws, wgd, wo, g, b, tm):
    n = x.shape[0]
    tok = lambda i: (i, 0)
    return pl.pallas_call(
        _merge_kernel,
        out_shape=jax.ShapeDtypeStruct((n, D_MODEL), f32),
        grid=(n // tm,),
        in_specs=[pl.BlockSpec((tm, D_MODEL), tok)] * 4
                 + [_const_spec(w.shape) for w in (wm, wr, ws, wgd, wo, g, b)],
        out_specs=pl.BlockSpec((tm, D_MODEL), tok),
        compiler_params=_params(("parallel",)),
        name="merge",
    )(x, ar, a_s, ag, wm, wr, ws, wgd, wo, g, b)


def _ffn_pe_kernel(x_ref, p_ref, wg_ref, wu_ref, wd_ref, pg_ref, pp_ref, g_ref, b_ref, o_ref):
    x = x_ref[...]
    x = _layer_norm(DN_ALPHA * x + 0.5 * _swiglu(x, wg_ref, wu_ref, wd_ref), g_ref[2:3, :], b_ref[2:3, :])
    pe = jax.nn.sigmoid(_dot(_b(x), pg_ref[...])) * _dot(_b(p_ref[...]), pp_ref[...])
    o_ref[...] = _layer_norm(DN_ALPHA * x + pe, g_ref[3:4, :], b_ref[3:4, :])


def _ffn_pe(x, p, layer, wg, wu, wd, pg, pp, g, b, tm):
    n = x.shape[0]
    return pl.pallas_call(
        _ffn_pe_kernel,
        out_shape=jax.ShapeDtypeStruct((n, D_MODEL), f32),
        grid=(n // tm,),
        in_specs=[pl.BlockSpec((tm, D_MODEL), lambda i: (i, 0)),
                  pl.BlockSpec((None, tm, PLE_DIM), lambda i: (layer, i, 0))]
                 + [_const_spec(w.shape) for w in (wg, wu, wd, pg, pp, g, b)],
        out_specs=pl.BlockSpec((tm, D_MODEL), lambda i: (i, 0)),
        compiler_params=_params(("parallel",)),
        name="ffn_pe",
    )(x, p, wg, wu, wd, pg, pp, g, b)


def _rope_inplace(proj_ref, off, cos, sin, scale):
    t = proj_ref[:, off:off + R_DK]
    t = t * cos + pltpu.roll(t, R_DK // 2, axis=1) * sin
    if scale != 1.0:
        t = t * scale
    proj_ref[:, off:off + R_DK] = t


def _ret_scan_kernel(x_ref, w_ref, cos_ref, sin_ref, act_ref, st_ref, proj_ref, s_ref, *, tl, c):
    l = pl.program_id(1)

    @pl.when(l == 0)
    def _():
        s_ref[...] = jnp.zeros_like(s_ref)

    proj_ref[...] = _dot(_b(x_ref[...]), w_ref[...])
    cos, sin = cos_ref[...], sin_ref[...]
    for h in range(R_HEADS):
        _rope_inplace(proj_ref, h * R_DK, cos, sin, 1.0)
        _rope_inplace(proj_ref, R_QK + h * R_DK, cos, sin, R_DK ** -0.5)

    ii, jj = _chunk_iotas(c)
    dif = (ii - jj).astype(f32)
    ci = lax.broadcasted_iota(jnp.int32, (c, 1), 0).astype(f32)
    decay = [jnp.where(dif >= 0, jnp.exp(dif * lg), 0.0) for lg in LOG_GAMMA]
    e_col = [jnp.exp((ci + 1.0) * lg) for lg in LOG_GAMMA]
    w_col = [jnp.exp((c - 1.0 - ci) * lg) for lg in LOG_GAMMA]

    def chunk(ck):
        rows = pl.ds(pl.multiple_of(ck * c, c), c)
        hd = range(R_HEADS)
        q = [proj_ref[rows, h * R_DK:(h + 1) * R_DK] for h in hd]
        k = [proj_ref[rows, R_QK + h * R_DK:R_QK + (h + 1) * R_DK] for h in hd]
        v = [_b(proj_ref[rows, 2 * R_QK + h * R_DV:2 * R_QK + (h + 1) * R_DV]) for h in hd]
        s = [s_ref[h] for h in hd]
        scores = [_dot_nt(_b(q[h]), _b(k[h])) * decay[h] for h in hd]
        inter = [_dot(_b(q[h] * e_col[h]), _b(s[h])) for h in hd]
        for h in hd:
            s_ref[h] = s[h] * math.exp(c * LOG_GAMMA[h]) + _dot_tn(_b(k[h] * w_col[h]), v[h])
        o = [_dot(_b(scores[h]), v[h]) + inter[h] for h in hd]
        for h in hd:
            mu = jnp.mean(o[h], -1, keepdims=True)
            var = jnp.mean(jnp.square(o[h] - mu), -1, keepdims=True)
            on = (o[h] - mu) * lax.rsqrt(var + LN_EPS)
            g = proj_ref[rows, 2 * R_QK + R_VAL + h * R_DV:2 * R_QK + R_VAL + (h + 1) * R_DV]
            act_ref[rows, h * R_DV:(h + 1) * R_DV] = _b(on * jax.nn.silu(g))

    def chunks(it, carry):
        for j in range(RET_UNROLL):
            chunk(it * RET_UNROLL + j)
        return carry

    lax.fori_loop(0, tl // (c * RET_UNROLL), chunks, 0)

    @pl.when(l == pl.num_programs(1) - 1)
    def _():
        st_ref[0] = s_ref[...]


def _ret_scan(x, w, cos, sin, nb, nl_tok, tl, c):
    nl = nl_tok // tl
    tok = lambda b, l: (b * nl + l, 0)
    return pl.pallas_call(
        functools.partial(_ret_scan_kernel, tl=tl, c=c),
        out_shape=(jax.ShapeDtypeStruct((nb * nl_tok, R_VAL), bf16),
                   jax.ShapeDtypeStruct((nb, R_HEADS, R_DK, R_DV), f32)),
        grid=(nb, nl),
        in_specs=[pl.BlockSpec((tl, D_MODEL), tok), _const_spec(w.shape),
                  pl.BlockSpec((tl, R_DK), lambda b, l: (l, 0)),
                  pl.BlockSpec((tl, R_DK), lambda b, l: (l, 0))],
        out_specs=(pl.BlockSpec((tl, R_VAL), tok),
                   pl.BlockSpec((1, R_HEADS, R_DK, R_DV), lambda b, l: (b, 0, 0, 0))),
        scratch_shapes=[pltpu.VMEM((tl, w.shape[1]), f32),
                        pltpu.VMEM((R_HEADS, R_DK, R_DV), f32)],
        compiler_params=_params(("parallel", "arbitrary")),
        name="ret_scan",
    )(x, w, cos, sin)


def _causal_conv_tile(xbuf_ref, width, cw_ref, cb_ref, dst_ref, tl, first):
    @pl.when(first)
    def _():
        xbuf_ref[0:SUBLANES, :] = jnp.zeros((SUBLANES, width), f32)

    cblk = 512
    for cb in range(width // cblk):
        cs = slice(cb * cblk, (cb + 1) * cblk)
        acc = xbuf_ref[SUBLANES:SUBLANES + tl, cs] * cw_ref[CONV_W - 1:CONV_W, cs]
        for j in range(CONV_W - 1):
            r0 = SUBLANES - (CONV_W - 1) + j
            acc = acc + xbuf_ref[r0:r0 + tl, cs] * cw_ref[j:j + 1, cs]
        if cb_ref is not None:
            acc = acc + cb_ref[:, cs]
        dst_ref[:, cs] = jax.nn.silu(acc)
    xbuf_ref[0:SUBLANES, :] = xbuf_ref[tl:tl + SUBLANES, :]


def _head_expander(e2_ref, width):
    er = lax.broadcasted_iota(jnp.int32, e2_ref.shape, 0)
    el = lax.broadcasted_iota(jnp.int32, e2_ref.shape, 1)
    e2_ref[...] = _b(((er & (PAD - 1)) == (el >> (width.bit_length() - 1))).astype(f32))


def _chunk_block_masks(tri_ref, ones_ref, tl, c):
    log2c = c.bit_length() - 1
    ti = lax.broadcasted_iota(jnp.int32, (tl, tl), 0)
    tj = lax.broadcasted_iota(jnp.int32, (tl, tl), 1)
    same = ((ti >> log2c) == (tj >> log2c)).astype(f32)
    ones_ref[...] = _b(same)
    tri_ref[...] = _b(jnp.where(ti >= tj, same, 0.0))


def _chunk_cumsum(tri_ref, ones_ref, la):
    a1, a2, a3 = _split3(la)
    cum = _dot(tri_ref[...], a1) + _dot(tri_ref[...], a2) + _dot(tri_ref[...], a3)
    tot = _dot(ones_ref[...], a1) + _dot(ones_ref[...], a2) + _dot(ones_ref[...], a3)
    return cum, tot


def _expand_heads(v, e2_ref):
    hi = _b(v)
    lo = _b(v - hi.astype(f32))
    return _dot(jnp.concatenate([hi, lo], axis=1), e2_ref[...])


def _ssd_scan_kernel(x_ref, w_ref, cw_ref, cb_ref, dtb_ref, alog_ref, dexp_ref, nw_ref,
                     act_ref, st_ref, conv_ref,
                     z_ref, xbuf_ref, xc_ref, cum_ref, xdt_ref, xw_ref, ee_ref, y_ref,
                     tri_ref, ones_ref, e2_ref, s_ref, *, tl, c):
    l = pl.program_id(1)

    @pl.when(l == 0)
    def _():
        s_ref[...] = jnp.zeros_like(s_ref)
        _chunk_block_masks(tri_ref, ones_ref, tl, c)
        _head_expander(e2_ref, M_HEADDIM)

    xb = _b(x_ref[...])
    z_ref[...] = _dot(xb, w_ref[:, 0:M_INNER])
    xbuf_ref[SUBLANES:SUBLANES + tl, :] = _dot(xb, w_ref[:, M_INNER:M_INNER + M_CONV_DIM])
    _causal_conv_tile(xbuf_ref, M_CONV_DIM, cw_ref, cb_ref, xc_ref, tl, l == 0)
    dt = jax.nn.softplus(_dot(xb, w_ref[:, M_INNER + M_CONV_DIM:]) + dtb_ref[...])
    la = -jnp.exp(alog_ref[...]) * dt
    cum, tot = _chunk_cumsum(tri_ref, ones_ref, la)
    cum_ref[...] = cum
    xdt = xc_ref[:, 0:M_INNER] * _expand_heads(dt, e2_ref)
    xdt_ref[...] = xdt
    xw_ref[...] = xdt * _expand_heads(jnp.exp(tot - cum), e2_ref)
    ee_ref[...] = _expand_heads(jnp.exp(cum), e2_ref)

    half = c
    lane = lax.broadcasted_iota(jnp.int32, (c, 2 * half), 1)
    rowi = lax.broadcasted_iota(jnp.int32, (c, 2 * half), 0)
    left = lane < half
    causal2 = (lane & (half - 1)) <= rowi
    b_off, c_off = M_INNER, M_INNER + M_GROUPS * M_STATE
    gr = range(M_GROUPS)
    pairs = [(g, pp) for g in gr for pp in range(M_HPG // 2)]

    def chunk(ck, carry):
        r0 = pl.multiple_of(ck * c, c)
        rows = pl.ds(r0, c)
        cum_c = cum_ref[rows, :]
        cum_t = jnp.concatenate([cum_c, cum_c], axis=0).T
        bb = [_b(xc_ref[rows, b_off + g * M_STATE:b_off + (g + 1) * M_STATE]) for g in gr]
        cb = [_b(xc_ref[rows, c_off + g * M_STATE:c_off + (g + 1) * M_STATE]) for g in gr]
        g2 = [_dot_nt(cb[g], jnp.concatenate([bb[g], bb[g]], axis=0)) for g in gr]
        sg = [s_ref[g] for g in gr]
        inter = [ee_ref[rows, g * M_GW:(g + 1) * M_GW] * _dot(cb[g], _b(sg[g])) for g in gr]
        for g in gr:
            e_last = ee_ref[pl.ds(r0 + c - 1, 1), g * M_GW:(g + 1) * M_GW]
            s_ref[g] = sg[g] * e_last + _dot_tn(bb[g], _b(xw_ref[rows, g * M_GW:(g + 1) * M_GW]))
        a2s, rhs = [], []
        for g, pp in pairs:
            h0 = g * M_HPG + 2 * pp
            ls = slice(h0 * M_HEADDIM, (h0 + 2) * M_HEADDIM)
            colsel = jnp.where(left, cum_c[:, h0:h0 + 1], cum_c[:, h0 + 1:h0 + 2])
            rowsel = jnp.where(left[0:1, :], cum_t[h0:h0 + 1, :], cum_t[h0 + 1:h0 + 2, :])
            d2 = jnp.where(causal2, jnp.exp(colsel - rowsel), 0.0)
            a2s.append(_b(g2[g] * d2))
            xp = xdt_ref[rows, ls]
            rhs.append(_b(jnp.concatenate([jnp.where(left, xp, 0.0), jnp.where(left, 0.0, xp)], axis=0)))
        intra = [_dot(a, r) for a, r in zip(a2s, rhs)]
        for n, (g, pp) in enumerate(pairs):
            h0 = g * M_HPG + 2 * pp
            ls = slice(h0 * M_HEADDIM, (h0 + 2) * M_HEADDIM)
            y_ref[rows, ls] = (intra[n] + inter[g][:, pp * 2 * M_HEADDIM:(pp + 1) * 2 * M_HEADDIM]
                               + dexp_ref[:, ls] * xc_ref[rows, ls])
        return carry

    lax.fori_loop(0, tl // c, chunk, 0)

    y = y_ref[...] * jax.nn.silu(z_ref[...])
    for g in range(M_GROUPS):
        gs = slice(g * M_GW, (g + 1) * M_GW)
        act_ref[:, gs] = _b(_rms(y[:, gs]) * nw_ref[:, gs])

    @pl.when(l == pl.num_programs(1) - 1)
    def _():
        for h in range(M_HEADS):
            g, hh = divmod(h, M_HPG)
            st_ref[0, h] = s_ref[g][:, hh * M_HEADDIM:(hh + 1) * M_HEADDIM].T
        conv_ref[0] = xbuf_ref[SUBLANES - (CONV_W - 1):SUBLANES, :]


def _ssd_scan(x, w, cw, cb, dtb, alog, dexp, nw, nb, nl_tok, tl, c):
    assert 2 * c == LANES and 2 * M_HEADDIM == LANES, "head pairs are packed into one 128-lane slab"
    nl = nl_tok // tl
    tok = lambda b, l: (b * nl + l, 0)
    return pl.pallas_call(
        functools.partial(_ssd_scan_kernel, tl=tl, c=c),
        out_shape=(jax.ShapeDtypeStruct((nb * nl_tok, M_INNER), bf16),
                   jax.ShapeDtypeStruct((nb, M_HEADS, M_HEADDIM, M_STATE), f32),
                   jax.ShapeDtypeStruct((nb, CONV_W - 1, M_CONV_DIM), f32)),
        grid=(nb, nl),
        in_specs=[pl.BlockSpec((tl, D_MODEL), tok)]
                 + [_const_spec(a.shape) for a in (w, cw, cb, dtb, alog, dexp, nw)],
        out_specs=(pl.BlockSpec((tl, M_INNER), tok),
                   pl.BlockSpec((1, M_HEADS, M_HEADDIM, M_STATE), lambda b, l: (b, 0, 0, 0)),
                   pl.BlockSpec((1, CONV_W - 1, M_CONV_DIM), lambda b, l: (b, 0, 0))),
        scratch_shapes=[pltpu.VMEM((tl, M_INNER), f32),
                        pltpu.VMEM((tl + SUBLANES, M_CONV_DIM), f32),
                        pltpu.VMEM((tl, M_CONV_DIM), f32),
                        pltpu.VMEM((tl, PAD), f32),
                        pltpu.VMEM((tl, M_INNER), f32),
                        pltpu.VMEM((tl, M_INNER), f32),
                        pltpu.VMEM((tl, M_INNER), f32),
                        pltpu.VMEM((tl, M_INNER), f32),
                        pltpu.VMEM((tl, tl), bf16),
                        pltpu.VMEM((tl, tl), bf16),
                        pltpu.VMEM((2 * PAD, M_INNER), bf16),
                        pltpu.VMEM((M_GROUPS, M_STATE, M_GW), f32)],
        compiler_params=_params(("parallel", "arbitrary")),
        name="ssd_scan",
    )(x, w, cw, cb, dtb, alog, dexp, nw)


def _pair_blockdiag(x, left):
    return jnp.concatenate([jnp.where(left, x, jnp.zeros_like(x)), jnp.where(left, jnp.zeros_like(x), x)], axis=0)


def _gdn_scan_kernel(x_ref, w_ref, cw_ref, dtb_ref, alog_ref, nw_ref,
                     act_ref, st_ref, conv_ref,
                     gz_ref, xbuf_ref, qkv_ref, cum_ref, ee_ref,
                     q16_ref, k16_ref, kb16_ref, qe16_ref, kbe16_ref, kw16_ref, vb16_ref,
                     wy_ref, u0_ref, attn_ref, tri_ref, ones_ref, e2_ref, s_ref, *, tl, c):
    l = pl.program_id(1)
    n_pairs = G_HEADS // 2
    pw = 2 * G_DK

    @pl.when(l == 0)
    def _():
        s_ref[...] = jnp.zeros_like(s_ref)
        _chunk_block_masks(tri_ref, ones_ref, tl, c)
        _head_expander(e2_ref, G_DK)

    xb = _b(x_ref[...])
    xbuf_ref[SUBLANES:SUBLANES + tl, :] = _dot(xb, w_ref[:, 0:G_QKV])
    gz_ref[...] = _dot(xb, w_ref[:, G_QKV:G_QKV + G_VAL])
    ab = _dot(xb, w_ref[:, G_QKV + G_VAL:])
    _causal_conv_tile(xbuf_ref, G_QKV, cw_ref, None, qkv_ref, tl, l == 0)
    g = -jnp.exp(alog_ref[...]) * jax.nn.softplus(ab + dtb_ref[...])
    cum, tot = _chunk_cumsum(tri_ref, ones_ref, g)
    cum_ref[...] = cum
    beta = pltpu.roll(jax.nn.sigmoid(ab), PAD - G_HEADS, axis=1)
    e_c, w_c = jnp.exp(cum), jnp.exp(tot - cum)
    for hb in range(n_pairs):
        ls = slice(hb * pw, (hb + 1) * pw)
        e_x = _expand_heads(e_c, e2_ref.at[:, ls])
        w_x = _expand_heads(w_c, e2_ref.at[:, ls])
        b_x = _expand_heads(beta, e2_ref.at[:, ls])
        ee_ref[:, ls] = e_x
        qn, kn = [], []
        for t in range(2):
            hs = slice((2 * hb + t) * G_DK, (2 * hb + t + 1) * G_DK)
            qt, kt = qkv_ref[:, hs], qkv_ref[:, G_KEY + hs.start:G_KEY + hs.stop]
            qn.append(qt * lax.rsqrt(jnp.sum(jnp.square(qt), -1, keepdims=True) + NORM_EPS) * (G_DK ** -0.5))
            kn.append(kt * lax.rsqrt(jnp.sum(jnp.square(kt), -1, keepdims=True) + NORM_EPS))
        q, k = jnp.concatenate(qn, axis=1), jnp.concatenate(kn, axis=1)
        kb = k * b_x
        q16_ref[:, ls] = _b(q)
        k16_ref[:, ls] = _b(k)
        kb16_ref[:, ls] = _b(kb)
        qe16_ref[:, ls] = _b(q * e_x)
        kbe16_ref[:, ls] = _b(kb * e_x)
        kw16_ref[:, ls] = _b(k * w_x)
        vb16_ref[:, ls] = _b(qkv_ref[:, 2 * G_KEY + hb * pw:2 * G_KEY + (hb + 1) * pw] * b_x)

    lane = lax.broadcasted_iota(jnp.int32, (c, 2 * c), 1)
    rowi = lax.broadcasted_iota(jnp.int32, (c, 2 * c), 0)
    left = lane < c
    jloc = lane & (c - 1)
    causal2, strict2 = jloc <= rowi, jloc < rowi
    eye2 = (jloc == rowi).astype(f32)
    left_w = lax.broadcasted_iota(jnp.int32, (c, pw), 1) < G_DK
    left_s = lax.broadcasted_iota(jnp.int32, (G_DK, pw), 1) < G_DK
    pr = range(n_pairs)

    def precompute(it, carry):
        cks = [it * GDN_PRE_UNROLL + j for j in range(GDN_PRE_UNROLL)]
        rows = [pl.ds(pl.multiple_of(ck * c, c), c) for ck in cks]
        lsl = [slice(p * pw, (p + 1) * pw) for p in pr]
        cp = [(j, p) for j in range(GDN_PRE_UNROLL) for p in pr]
        cum_c = [cum_ref[r, :] for r in rows]
        cum_t = [jnp.concatenate([x, x], axis=0).T for x in cum_c]
        d2 = []
        for j, p in cp:
            colsel = jnp.where(left, cum_c[j][:, 2 * p:2 * p + 1], cum_c[j][:, 2 * p + 1:2 * p + 2])
            rowsel = jnp.where(left[0:1, :], cum_t[j][2 * p:2 * p + 1, :], cum_t[j][2 * p + 1:2 * p + 2, :])
            d2.append(jnp.where(causal2, jnp.exp(colsel - rowsel), 0.0))
        kbd = [_pair_blockdiag(k16_ref[rows[j], lsl[p]], left_w) for j, p in cp]
        lm = [jnp.where(strict2, _dot_nt(kb16_ref[rows[j], lsl[p]], kbd[n]) * d2[n], 0.0)
              for n, (j, p) in enumerate(cp)]
        attn = [_dot_nt(q16_ref[rows[j], lsl[p]], kbd[n]) * d2[n] for n, (j, p) in enumerate(cp)]
        for n, (j, p) in enumerate(cp):
            attn_ref[rows[j], p * 2 * c:(p + 1) * 2 * c] = _b(attn[n])
        ps = [eye2 - x for x in lm]
        ms = lm
        kpow = 2
        while kpow < c:
            ms = [_dot(_b(m), _b(_pair_blockdiag(m, left))) for m in ms]
            ps = [x + _dot(_b(x), _b(_pair_blockdiag(m, left))) for x, m in zip(ps, ms)]
            kpow *= 2
        rhs = [jnp.concatenate([_pair_blockdiag(kbe16_ref[rows[j], lsl[p]], left_w),
                                _pair_blockdiag(vb16_ref[rows[j], lsl[p]], left_w)], axis=1) for j, p in cp]
        wu = [_dot(_b(ps[n]), rhs[n]) for n in range(len(cp))]
        for n, (j, p) in enumerate(cp):
            wy_ref[rows[j], lsl[p]] = _b(wu[n][:, 0:pw])
            u0_ref[rows[j], lsl[p]] = wu[n][:, pw:2 * pw]
        return carry

    lax.fori_loop(0, tl // (c * GDN_PRE_UNROLL), precompute, 0)

    def scan_chunk(ck):
        r0 = pl.multiple_of(ck * c, c)
        rows = pl.ds(r0, c)
        lsl = [slice(p * pw, (p + 1) * pw) for p in pr]
        sp = [s_ref[p] for p in pr]
        sbd = [_b(_pair_blockdiag(sp[p], left_s)) for p in pr]
        r = [_dot(jnp.concatenate([wy_ref[rows, lsl[p]], qe16_ref[rows, lsl[p]]], axis=0), sbd[p]) for p in pr]
        u = [u0_ref[rows, lsl[p]] - r[p][0:c, :] for p in pr]
        ubd = [_b(_pair_blockdiag(u[p], left_w)) for p in pr]
        for p in pr:
            kw = kw16_ref[rows, lsl[p]]
            kw_stack = jnp.concatenate([kw[:, 0:G_DK], kw[:, G_DK:pw]], axis=0)
            e_last = ee_ref[pl.ds(r0 + c - 1, 1), lsl[p]]
            s_ref[p] = sp[p] * e_last + _dot_tn(kw_stack, ubd[p])
        o = [r[p][c:2 * c, :] + _dot(attn_ref[rows, p * 2 * c:(p + 1) * 2 * c], ubd[p]) for p in pr]
        for p in pr:
            for t in range(2):
                hs = slice((2 * p + t) * G_DV, (2 * p + t + 1) * G_DV)
                act_ref[rows, hs] = _b(_rms(o[p][:, t * G_DV:(t + 1) * G_DV]) * nw_ref[...]
                                       * jax.nn.silu(gz_ref[rows, hs]))

    def scan(it, carry):
        for j in range(GDN_SCAN_UNROLL):
            scan_chunk(it * GDN_SCAN_UNROLL + j)
        return carry

    lax.fori_loop(0, tl // (c * GDN_SCAN_UNROLL), scan, 0)

    @pl.when(l == pl.num_programs(1) - 1)
    def _():
        for h in range(G_HEADS):
            st_ref[0, h] = s_ref[h // 2][:, (h % 2) * G_DV:(h % 2 + 1) * G_DV]
        conv_ref[0] = xbuf_ref[SUBLANES - (CONV_W - 1):SUBLANES, :]


def _gdn_scan(x, w, cw, dtb, alog, nw, nb, nl_tok, tl, c):
    assert 2 * c == LANES and G_DK == G_DV == LANES, "two heads' (c, c) blocks share one 128-lane slab"
    nl = nl_tok // tl
    tok = lambda b, l: (b * nl + l, 0)
    return pl.pallas_call(
        functools.partial(_gdn_scan_kernel, tl=tl, c=c),
        out_shape=(jax.ShapeDtypeStruct((nb * nl_tok, G_VAL), bf16),
                   jax.ShapeDtypeStruct((nb, G_HEADS, G_DK, G_DV), f32),
                   jax.ShapeDtypeStruct((nb, CONV_W - 1, G_QKV), f32)),
        grid=(nb, nl),
        in_specs=[pl.BlockSpec((tl, D_MODEL), tok)]
                 + [_const_spec(a.shape) for a in (w, cw, dtb, alog, nw)],
        out_specs=(pl.BlockSpec((tl, G_VAL), tok),
                   pl.BlockSpec((1, G_HEADS, G_DK, G_DV), lambda b, l: (b, 0, 0, 0)),
                   pl.BlockSpec((1, CONV_W - 1, G_QKV), lambda b, l: (b, 0, 0))),
        scratch_shapes=[pltpu.VMEM((tl, G_VAL), f32),
                        pltpu.VMEM((tl + SUBLANES, G_QKV), f32),
                        pltpu.VMEM((tl, G_QKV), f32),
                        pltpu.VMEM((tl, PAD), f32),
                        pltpu.VMEM((tl, G_KEY), f32),
                        ] + [pltpu.VMEM((tl, G_KEY), bf16)] * 7 + [
                        pltpu.VMEM((tl, G_KEY), bf16),
                        pltpu.VMEM((tl, G_VAL), f32),
                        pltpu.VMEM((tl, G_HEADS * c), bf16),
                        pltpu.VMEM((tl, tl), bf16),
                        pltpu.VMEM((tl, tl), bf16),
                        pltpu.VMEM((2 * PAD, G_KEY), bf16),
                        pltpu.VMEM((G_HEADS // 2, G_DK, 2 * G_DV), f32)],
        compiler_params=_params(("parallel", "arbitrary")),
        name="gdn_scan",
    )(x, w, cw, dtb, alog, nw)


def _token_lanes_to_front(src_ref, dst_ref, i):
    n = src_ref.shape[1]
    dst_ref[...] = pltpu.roll(src_ref[...], (n - i * DEC_BT) % n, axis=1)


def _conv_step(cst_ref, x_new, cw_ref, cb_ref, conv_out_ref):
    acc = x_new * cw_ref[CONV_W - 1:CONV_W, :]
    for j in range(CONV_W - 1):
        acc = acc + cst_ref[j] * cw_ref[j:j + 1, :]
    if cb_ref is not None:
        acc = acc + cb_ref[...]
    for j in range(CONV_W - 2):
        conv_out_ref[j] = cst_ref[j + 1]
    conv_out_ref[CONV_W - 2] = x_new
    return jax.nn.silu(acc)


def _state_specs(state, prev, layer, blk):
    zeros = (0,) * (len(blk) - 2)
    spec = pl.BlockSpec(blk, lambda i: (layer, i) + zeros)
    if prev is None:
        prev = jnp.zeros((SUBLANES, LANES), f32)
        alias = {}
    else:
        alias = None
    return spec, prev, alias


def _dec_ret_kernel(x_ref, w_ref, cos_ref, sin_ref, s_in, prev_ref, act_ref, s_out,
                    q_ref, kt_ref, vg_ref, ks_ref, o_ref):
    del prev_ref
    i = pl.program_id(0)

    @pl.when(i == 0)
    def _():
        proj = _dot(_b(x_ref[...]), w_ref[...])
        cos, sin = cos_ref[...], sin_ref[...]
        for h in range(R_HEADS):
            hs = slice(h * R_DK, (h + 1) * R_DK)
            t = proj[:, hs]
            q_ref[:, hs] = t * cos + pltpu.roll(t, R_DK // 2, axis=1) * sin
            t = proj[:, R_QK + h * R_DK:R_QK + (h + 1) * R_DK]
            q_ref[:, R_QK + h * R_DK:R_QK + (h + 1) * R_DK] = t = (
                t * cos + pltpu.roll(t, R_DK // 2, axis=1) * sin) * (R_DK ** -0.5)
            kt_ref[hs, :] = t.T
        vg_ref[...] = proj[:, 2 * R_QK:]

    _token_lanes_to_front(kt_ref, ks_ref, i)
    rows = pl.ds(pl.multiple_of(i * DEC_BT, DEC_BT), DEC_BT)
    vg, qk8 = vg_ref[rows, :], q_ref[rows, :]
    for h in range(R_HEADS):
        q8 = qk8[:, h * R_DK:(h + 1) * R_DK]
        k8 = qk8[:, R_QK + h * R_DK:R_QK + (h + 1) * R_DK]
        qk = jnp.sum(q8 * k8, axis=1, keepdims=True)
        lhs = _b(jnp.concatenate([q8, q8], axis=0))
        gam = math.exp(LOG_GAMMA[h])
        for j in range(DEC_BT):
            kc = ks_ref[h * R_DK:(h + 1) * R_DK, j:j + 1]
            v_row = vg[j:j + 1, h * R_DV:(h + 1) * R_DV]
            s = s_in[j, h]
            qs = _dot(lhs, _b(s))[j:j + 1, :]
            o_ref[j:j + 1, h * R_DV:(h + 1) * R_DV] = gam * qs + qk[j:j + 1, :] * v_row
            s_out[j, h] = s * gam + kc * v_row
    for h in range(R_HEADS):
        o = o_ref[:, h * R_DV:(h + 1) * R_DV]
        mu = jnp.mean(o, -1, keepdims=True)
        var = jnp.mean(jnp.square(o - mu), -1, keepdims=True)
        g = vg[:, R_VAL + h * R_DV:R_VAL + (h + 1) * R_DV]
        act_ref[:, h * R_DV:(h + 1) * R_DV] = (o - mu) * lax.rsqrt(var + LN_EPS) * jax.nn.silu(g)


def _dec_ret(x, w, cos, sin, state, layer, prev):
    n = x.shape[0]
    consts = (x, w, cos, sin)
    sspec, prev, alias = _state_specs(state, prev, layer, (None, DEC_BT, R_HEADS, R_DK, R_DV))
    return pl.pallas_call(
        _dec_ret_kernel,
        out_shape=(jax.ShapeDtypeStruct((n, R_VAL), f32), jax.ShapeDtypeStruct(state.shape, f32)),
        grid=(n // DEC_BT,),
        in_specs=[_const_spec(a.shape) for a in consts] + [sspec, pl.BlockSpec(memory_space=pl.ANY)],
        out_specs=(pl.BlockSpec((DEC_BT, R_VAL), lambda i: (i, 0)), sspec),
        scratch_shapes=[pltpu.VMEM((n, 2 * R_QK), f32),
                        pltpu.VMEM((R_QK, n), f32),
                        pltpu.VMEM((n, 2 * R_VAL), f32),
                        pltpu.VMEM((R_QK, n), f32),
                        pltpu.VMEM((DEC_BT, R_VAL), f32)],
        input_output_aliases={len(consts) + 1: 1} if alias is None else alias,
        compiler_params=_params(("arbitrary",)),
        name="dec_ret",
    )(*consts, state, prev)


def _dec_ssd_kernel(x_ref, w_ref, cst_ref, cw_ref, cb_ref, dtb_ref, alog_ref, dexp_ref, nw_ref, s_in, prev_ref,
                    act_ref, s_out, conv_out_ref,
                    xc_ref, z_ref, xdt_ref, xdtt_ref, ela_ref, elax_ref, xts_ref, e2_ref, o_ref):
    del prev_ref
    i = pl.program_id(0)

    @pl.when(i == 0)
    def _():
        _head_expander(e2_ref, M_HEADDIM)
        proj = _dot(_b(x_ref[...]), w_ref[...])
        xc = _conv_step(cst_ref, proj[:, M_INNER:M_INNER + M_CONV_DIM], cw_ref, cb_ref, conv_out_ref)
        xc_ref[...] = xc
        z_ref[...] = proj[:, 0:M_INNER]
        dt = jax.nn.softplus(proj[:, M_INNER + M_CONV_DIM:M_INNER + M_CONV_DIM + PAD] + dtb_ref[...])
        ela = jnp.exp(-jnp.exp(alog_ref[...]) * dt)
        ela_ref[...] = ela
        elax_ref[...] = _expand_heads(ela, e2_ref)
        xdt = xc[:, 0:M_INNER] * _expand_heads(dt, e2_ref)
        xdt_ref[...] = xdt
        xdtt_ref[...] = xdt.T

    _token_lanes_to_front(xdtt_ref, xts_ref, i)
    rows = pl.ds(pl.multiple_of(i * DEC_BT, DEC_BT), DEC_BT)
    xc8, xdt8, ela8, elax8 = xc_ref[rows, :], xdt_ref[rows, :], ela_ref[rows, :], elax_ref[rows, :]
    b_off, c_off = M_INNER, M_INNER + M_GROUPS * M_STATE
    for g in range(M_GROUPS):
        gs = slice(g * M_GW, (g + 1) * M_GW)
        b8 = xc8[:, b_off + g * M_STATE:b_off + (g + 1) * M_STATE]
        c8 = xc8[:, c_off + g * M_STATE:c_off + (g + 1) * M_STATE]
        cb = jnp.sum(c8 * b8, axis=1, keepdims=True)
        lhs = _b(jnp.concatenate([c8, c8], axis=0))
        for j in range(DEC_BT):
            st = s_in[j, g * M_HPG:(g + 1) * M_HPG]
            cs = _dot_nt(lhs, _b(st.reshape(M_GW, M_STATE)))[j:j + 1, :]
            o_ref[j:j + 1, gs] = cb[j:j + 1, :] * xdt8[j:j + 1, gs] + elax8[j:j + 1, gs] * cs
            b_row = b8[j:j + 1, :]
            for hh in range(M_HPG):
                h = g * M_HPG + hh
                xdt_col = xts_ref[h * M_HEADDIM:(h + 1) * M_HEADDIM, j:j + 1]
                s_out[j, h] = st[hh] * ela8[j:j + 1, h:h + 1] + xdt_col * b_row
    y = (o_ref[...] + dexp_ref[...] * xc8[:, 0:M_INNER]) * jax.nn.silu(z_ref[rows, :])
    for g in range(M_GROUPS):
        gs = slice(g * M_GW, (g + 1) * M_GW)
        act_ref[:, gs] = _rms(y[:, gs]) * nw_ref[:, gs]


def _dec_ssd(x, w, cst, cw, cb, dtb, alog, dexp, nw, state, layer, prev):
    n = x.shape[0]
    consts = (x, w, cst, cw, cb, dtb, alog, dexp, nw)
    sspec, prev, alias = _state_specs(state, prev, layer, (None, DEC_BT, M_HEADS, M_HEADDIM, M_STATE))
    cspecs = [_const_spec(a.shape) for a in consts]
    cspecs[2] = pl.BlockSpec((None,) + cst.shape[1:], lambda i: (layer, 0, 0, 0), pipeline_mode=pl.Buffered(1))
    return pl.pallas_call(
        _dec_ssd_kernel,
        out_shape=(jax.ShapeDtypeStruct((n, M_INNER), f32), jax.ShapeDtypeStruct(state.shape, f32),
                   jax.ShapeDtypeStruct(cst.shape[1:], f32)),
        grid=(n // DEC_BT,),
        in_specs=cspecs + [sspec, pl.BlockSpec(memory_space=pl.ANY)],
        out_specs=(pl.BlockSpec((DEC_BT, M_INNER), lambda i: (i, 0)), sspec,
                   pl.BlockSpec(cst.shape[1:], lambda i: (0, 0, 0))),
        scratch_shapes=[pltpu.VMEM((n, M_CONV_DIM), f32),
                        pltpu.VMEM((n, M_INNER), f32),
                        pltpu.VMEM((n, M_INNER), f32),
                        pltpu.VMEM((M_INNER, n), f32),
                        pltpu.VMEM((n, PAD), f32),
                        pltpu.VMEM((n, M_INNER), f32),
                        pltpu.VMEM((M_INNER, n), f32),
                        pltpu.VMEM((2 * PAD, M_INNER), bf16),
                        pltpu.VMEM((DEC_BT, M_INNER), f32)],
        input_output_aliases={len(consts) + 1: 1} if alias is None else alias,
        compiler_params=_params(("arbitrary",)),
        name="dec_ssd",
    )(*consts, state, prev)


def _dec_gdn_kernel(x_ref, w_ref, cst_ref, cw_ref, dtb_ref, alog_ref, nw_ref, s_in, prev_ref,
                    act_ref, s_out, conv_out_ref,
                    qkv_ref, kt_ref, gz_ref, eg_ref, beta_ref, ks_ref, o_ref):
    del prev_ref
    i = pl.program_id(0)

    @pl.when(i == 0)
    def _():
        proj = _dot(_b(x_ref[...]), w_ref[...])
        qkv = _conv_step(cst_ref, proj[:, 0:G_QKV], cw_ref, None, conv_out_ref)
        for h in range(G_HEADS):
            hs = slice(h * G_DK, (h + 1) * G_DK)
            q = qkv[:, hs]
            qkv_ref[:, hs] = q * lax.rsqrt(jnp.sum(jnp.square(q), -1, keepdims=True) + NORM_EPS) * (G_DK ** -0.5)
            k = qkv[:, G_KEY + h * G_DK:G_KEY + (h + 1) * G_DK]
            k = k * lax.rsqrt(jnp.sum(jnp.square(k), -1, keepdims=True) + NORM_EPS)
            qkv_ref[:, G_KEY + h * G_DK:G_KEY + (h + 1) * G_DK] = k
            kt_ref[hs, :] = k.T
        qkv_ref[:, 2 * G_KEY:] = qkv[:, 2 * G_KEY:]
        gz_ref[...] = proj[:, G_QKV:G_QKV + G_VAL]
        ab = proj[:, G_QKV + G_VAL:G_QKV + G_VAL + PAD]
        eg_ref[...] = jnp.exp(-jnp.exp(alog_ref[...]) * jax.nn.softplus(ab + dtb_ref[...]))
        beta_ref[...] = jax.nn.sigmoid(ab)

    _token_lanes_to_front(kt_ref, ks_ref, i)
    rows = pl.ds(pl.multiple_of(i * DEC_BT, DEC_BT), DEC_BT)
    qkv8, eg8, beta8 = qkv_ref[rows, :], eg_ref[rows, :], beta_ref[rows, :]
    for h in range(G_HEADS):
        hs = slice(h * G_DV, (h + 1) * G_DV)
        q8 = qkv8[:, h * G_DK:(h + 1) * G_DK]
        k8 = qkv8[:, G_KEY + h * G_DK:G_KEY + (h + 1) * G_DK]
        v8 = qkv8[:, 2 * G_KEY + h * G_DV:2 * G_KEY + (h + 1) * G_DV]
        qk = jnp.sum(q8 * k8, axis=1, keepdims=True)
        lhs = _b(jnp.concatenate([q8, k8], axis=0))
        bh = beta8[:, G_HEADS + h:G_HEADS + h + 1]
        eg = eg8[:, h:h + 1]
        for j in range(DEC_BT):
            kc = ks_ref[h * G_DK:(h + 1) * G_DK, j:j + 1]
            s = s_in[j, h]
            qks = _dot(lhs, _b(s))
            bj, ej = bh[j:j + 1, :], eg[j:j + 1, :]
            u = v8[j:j + 1, :] * bj - (bj * ej) * qks[DEC_BT + j:DEC_BT + j + 1, :]
            o_ref[j:j + 1, hs] = ej * qks[j:j + 1, :] + qk[j:j + 1, :] * u
            s_out[j, h] = s * ej + kc * u
    gz8 = gz_ref[rows, :]
    for h in range(G_HEADS):
        hs = slice(h * G_DV, (h + 1) * G_DV)
        act_ref[:, hs] = _rms(o_ref[:, hs]) * nw_ref[...] * jax.nn.silu(gz8[:, hs])


def _dec_gdn(x, w, cst, cw, dtb, alog, nw, state, layer, prev):
    n = x.shape[0]
    consts = (x, w, cst, cw, dtb, alog, nw)
    sspec, prev, alias = _state_specs(state, prev, layer, (None, DEC_BT, G_HEADS, G_DK, G_DV))
    cspecs = [_const_spec(a.shape) for a in consts]
    cspecs[2] = pl.BlockSpec((None,) + cst.shape[1:], lambda i: (layer, 0, 0, 0), pipeline_mode=pl.Buffered(1))
    return pl.pallas_call(
        _dec_gdn_kernel,
        out_shape=(jax.ShapeDtypeStruct((n, G_VAL), f32), jax.ShapeDtypeStruct(state.shape, f32),
                   jax.ShapeDtypeStruct(cst.shape[1:], f32)),
        grid=(n // DEC_BT,),
        in_specs=cspecs + [sspec, pl.BlockSpec(memory_space=pl.ANY)],
        out_specs=(pl.BlockSpec((DEC_BT, G_VAL), lambda i: (i, 0)), sspec,
                   pl.BlockSpec(cst.shape[1:], lambda i: (0, 0, 0))),
        scratch_shapes=[pltpu.VMEM((n, G_QKV), f32),
                        pltpu.VMEM((G_KEY, n), f32),
                        pltpu.VMEM((n, G_VAL), f32),
                        pltpu.VMEM((n, PAD), f32), pltpu.VMEM((n, PAD), f32),
                        pltpu.VMEM((G_KEY, n), f32),
                        pltpu.VMEM((DEC_BT, G_VAL), f32)],
        input_output_aliases={len(consts) + 1: 1} if alias is None else alias,
        compiler_params=_params(("arbitrary",)),
        name="dec_gdn",
    )(*consts, state, prev)


def _rope_tables(pos):
    half = R_DK // 2
    inv = ROPE_BASE ** (-jnp.arange(half, dtype=f32) / half)
    ang = pos[:, None] * inv[None, :]
    cos, sin = jnp.cos(ang), jnp.sin(ang)
    return jnp.concatenate([cos, cos], axis=1), jnp.concatenate([-sin, sin], axis=1)


def _lane_pad(v, start=0):
    return jnp.zeros((1, PAD), f32).at[0, start:start + v.shape[0]].set(v)


def _layer_weights(i, prm):
    w_in = prm["w_in"][i]
    zpad = jnp.zeros((D_MODEL, PAD - M_HEADS), f32)
    w_ssd = jnp.concatenate([w_in[:, OFF_SSD:OFF_GDN], zpad], axis=1)
    w_gdn = jnp.concatenate([w_in[:, OFF_GDN:OFF_MERGE], jnp.zeros((D_MODEL, PAD - 2 * G_HEADS), f32)], axis=1)
    return dict(
        ln_g=prm["ln_g"][i], ln_b=prm["ln_b"][i],
        wg=_b(prm["ffn_wg"][i]), wu=_b(prm["ffn_wu"][i]), wd=_b(prm["ffn_wd"][i]),
        w_ret=_b(w_in[:, OFF_RET:OFF_SSD]), w_ssd=_b(w_ssd), w_gdn=_b(w_gdn), w_merge=_b(w_in[:, OFF_MERGE:]),
        ssm_cw=prm["ssm_conv_w"][i], ssm_cb=prm["ssm_conv_b"][i][None, :],
        ssm_dtb=_lane_pad(prm["ssm_dt_bias"][i]), ssm_alog=_lane_pad(prm["ssm_a_log"][i]),
        ssm_dexp=jnp.repeat(prm["ssm_d"][i], M_HEADDIM)[None, :], ssm_nw=prm["ssm_norm_w"][i][None, :],
        gdn_cw=prm["gdn_conv_w"][i],
        gdn_dtb=_lane_pad(prm["gdn_dt_bias"][i]), gdn_alog=_lane_pad(prm["gdn_a_log"][i]),
        gdn_nw=prm["gdn_norm_w"][i][None, :],
        w_ro=_b(prm["w_ret_out"][i]), w_so=_b(prm["w_ssm_out"][i]), w_go=_b(prm["w_gdn_out"][i]),
        w_o=_b(prm["w_o"][i]), pe_proj=_b(prm["pe_proj"][i]), pe_gate=_b(prm["pe_gate"][i]),
    )


def _post_mix(x1, acts, p, i, w, tm):
    x2 = _merge(x1, *acts, w["w_merge"], w["w_ro"], w["w_so"], w["w_go"], w["w_o"], w["ln_g"], w["ln_b"], tm)
    return _ffn_pe(x2, p, i, w["wg"][1], w["wu"][1], w["wd"][1], w["pe_gate"], w["pe_proj"],
                   w["ln_g"], w["ln_b"], tm)


def kernel(x_prompt, x_sample, state_ret, state_ssm, state_ssm_conv, state_gdn, state_gdn_conv,
           p_prompt, p_sample, ln_g, ln_b, ffn_wg, ffn_wu, ffn_wd, w_in,
           ssm_conv_w, ssm_conv_b, ssm_dt_bias, ssm_a_log, ssm_d, ssm_norm_w,
           gdn_conv_w, gdn_dt_bias, gdn_a_log, gdn_norm_w,
           w_ret_out, w_ssm_out, w_gdn_out, w_o, pe_proj, pe_gate):
    prm = dict(ln_g=ln_g, ln_b=ln_b, ffn_wg=ffn_wg, ffn_wu=ffn_wu, ffn_wd=ffn_wd, w_in=w_in,
               ssm_conv_w=ssm_conv_w, ssm_conv_b=ssm_conv_b, ssm_dt_bias=ssm_dt_bias,
               ssm_a_log=ssm_a_log, ssm_d=ssm_d, ssm_norm_w=ssm_norm_w,
               gdn_conv_w=gdn_conv_w, gdn_dt_bias=gdn_dt_bias, gdn_a_log=gdn_a_log,
               gdn_norm_w=gdn_norm_w, w_ret_out=w_ret_out, w_ssm_out=w_ssm_out,
               w_gdn_out=w_gdn_out, w_o=w_o, pe_proj=pe_proj, pe_gate=pe_gate)
    nb, seq, _ = x_prompt.shape
    ns = x_sample.shape[0]
    depth = w_in.shape[0]
    tl = min(TL_SCAN, seq)
    tm = min(TM_DENSE, nb * seq)
    chunk = CHUNK if seq % CHUNK == 0 else seq

    cos_p, sin_p = _rope_tables(jnp.arange(seq, dtype=f32))
    cos_s, sin_s = _rope_tables(jnp.full((1,), PAST_LEN, f32))
    xp = x_prompt.reshape(nb * seq, D_MODEL)
    xs = x_sample.reshape(ns, D_MODEL)
    pp = p_prompt.reshape(depth, nb * seq, PLE_DIM)
    ps = p_sample.reshape(depth, ns, PLE_DIM)
    ssm_state_t = jnp.swapaxes(state_ssm, 3, 4)
    ssm_conv_t = jnp.transpose(state_ssm_conv, (0, 2, 1, 3))
    gdn_conv_t = jnp.transpose(state_gdn_conv, (0, 2, 1, 3))

    prompt_states, sample_convs = [], []
    t_r = t_s = t_g = None
    for i in range(depth):
        w = _layer_weights(i, prm)
        x1 = _ffn_ln(xp, w["wg"][0], w["wu"][0], w["wd"][0], w["ln_g"], w["ln_b"], tm)
        a_r, s_r = _ret_scan(x1, w["w_ret"], cos_p, sin_p, nb, seq, tl, chunk)
        a_s, s_s, c_s = _ssd_scan(x1, w["w_ssd"], w["ssm_cw"], w["ssm_cb"], w["ssm_dtb"], w["ssm_alog"],
                                  w["ssm_dexp"], w["ssm_nw"], nb, seq, tl, chunk)
        a_g, s_g, c_g = _gdn_scan(x1, w["w_gdn"], w["gdn_cw"], w["gdn_dtb"], w["gdn_alog"], w["gdn_nw"],
                                  nb, seq, tl, chunk)
        xp = _post_mix(x1, (a_r, a_s, a_g), pp, i, w, tm)
        prompt_states.append((s_r, s_s, c_s, s_g, c_g))
        y1 = _ffn_ln(xs, w["wg"][0], w["wu"][0], w["wd"][0], w["ln_g"], w["ln_b"], ns)
        b_r, t_r = _dec_ret(y1, w["w_ret"], cos_s, sin_s, state_ret, i, t_r)
        b_s, t_s, d_s = _dec_ssd(y1, w["w_ssd"], ssm_conv_t, w["ssm_cw"], w["ssm_cb"], w["ssm_dtb"],
                                 w["ssm_alog"], w["ssm_dexp"], w["ssm_nw"], ssm_state_t, i, t_s)
        b_g, t_g, d_g = _dec_gdn(y1, w["w_gdn"], gdn_conv_t, w["gdn_cw"], w["gdn_dtb"], w["gdn_alog"],
                                 w["gdn_nw"], state_gdn, i, t_g)
        xs = _post_mix(y1, (b_r, b_s, b_g), ps, i, w, ns)
        sample_convs.append((d_s, d_g))

    r_p, s_p, sc_p, g_p, gc_p = (jnp.stack([s[j] for s in prompt_states]) for j in range(5))
    sc_s, gc_s = (jnp.transpose(jnp.stack([c[j] for c in sample_convs]), (0, 2, 1, 3)) for j in range(2))
    return (xp.reshape(nb, seq, D_MODEL), xs.reshape(ns, 1, D_MODEL),
            r_p, jnp.swapaxes(s_p, 3, 4), sc_p, g_p, gc_p,
            t_r, jnp.swapaxes(t_s, 3, 4), sc_s, t_g, gc_s)
```

```python
import functools
import math

import numpy as np
import jax
import jax.numpy as jnp
from jax import lax
from jax.experimental import pallas as pl
from jax.experimental.pallas import tpu as pltpu

f32, bf16 = jnp.float32, jnp.bfloat16

D_MODEL = 1024
DEPTH = 2
PAST_LEN = 16384
R_HEADS, R_DK, R_DV = 4, 128, 256
R_QK, R_VAL = R_HEADS * R_DK, R_HEADS * R_DV
ROPE_BASE = 10000.0
M_HEADS, M_HEADDIM, M_GROUPS, M_STATE = 16, 64, 2, 128
M_INNER = M_HEADS * M_HEADDIM
M_CONV_DIM = M_INNER + 2 * M_GROUPS * M_STATE
M_HPG = M_HEADS // M_GROUPS
M_GW = M_HPG * M_HEADDIM
G_HEADS, G_DK, G_DV = 8, 128, 128
G_KEY, G_VAL = G_HEADS * G_DK, G_HEADS * G_DV
G_QKV = 2 * G_KEY + G_VAL
CONV_W = 4
FFN_DIM = 2048
PLE_DIM = 256
DN_ALPHA = (2 * DEPTH) ** 0.25
LN_EPS = 1e-5
NORM_EPS = 1e-6

_sizes = (R_QK, R_QK, R_VAL, R_VAL, M_INNER, M_CONV_DIM, M_HEADS, G_QKV, G_VAL, G_HEADS, G_HEADS,
          D_MODEL, D_MODEL, D_MODEL)
_off = np.concatenate([[0], np.cumsum(_sizes)]).tolist()
OFF_RET, OFF_SSD, OFF_GDN, OFF_MERGE, IN_DIM = _off[0], _off[4], _off[7], _off[11], _off[14]

LANES = 128
SUBLANES = 8
VMEM_LIMIT = 56 * 2 ** 20

TM_DENSE = 512
TL_SCAN = 512
CHUNK = 64
FFN_CHUNK = 512
DEC_BT = 8
RET_CHUNK = 128
RET_UNROLL = 2
SSD_UNROLL = 4
GDN_PRE_UNROLL = 4
GDN_SCAN_UNROLL = 4
PAD = LANES

LOG_GAMMA = [math.log1p(-(2.0 ** (-5.0 - h))) for h in range(R_HEADS)]


def _dot(a, b):
    return jnp.dot(a, b, preferred_element_type=f32)


def _dot_nt(a, b):
    return lax.dot_general(a, b, (((1,), (1,)), ((), ())), preferred_element_type=f32)


def _dot_tn(a, b):
    return lax.dot_general(a, b, (((0,), (0,)), ((), ())), preferred_element_type=f32)


def _b(x):
    return x.astype(bf16)


def _layer_norm(y, g, b):
    mu = jnp.mean(y, -1, keepdims=True)
    var = jnp.mean(jnp.square(y - mu), -1, keepdims=True)
    return (y - mu) * lax.rsqrt(var + LN_EPS) * g + b


def _rms(y):
    return y * lax.rsqrt(jnp.mean(jnp.square(y), -1, keepdims=True) + NORM_EPS)


def _const_spec(shape):
    return pl.BlockSpec(shape, lambda *_: (0,) * len(shape), pipeline_mode=pl.Buffered(1))


def _params(sem):
    return pltpu.CompilerParams(dimension_semantics=sem, vmem_limit_bytes=VMEM_LIMIT)


def _split3(x):
    a1 = _b(x)
    r1 = x - a1.astype(f32)
    a2 = _b(r1)
    a3 = _b(r1 - a2.astype(f32))
    return a1, a2, a3


def _chunk_iotas(c):
    ii = lax.broadcasted_iota(jnp.int32, (c, c), 0)
    jj = lax.broadcasted_iota(jnp.int32, (c, c), 1)
    return ii, jj


def _swiglu(x, wg_ref, wu_ref, wd_ref):
    xb = _b(x)
    acc = None
    for c in range(FFN_DIM // FFN_CHUNK):
        sl = slice(c * FFN_CHUNK, (c + 1) * FFN_CHUNK)
        a = jax.nn.silu(_dot(xb, wg_ref[:, sl])) * _dot(xb, wu_ref[:, sl])
        part = _dot(_b(a), wd_ref[sl, :])
        acc = part if acc is None else acc + part
    return acc


def _ffn_ln_kernel(x_ref, wg_ref, wu_ref, wd_ref, g_ref, b_ref, o_ref):
    x = x_ref[...]
    y = DN_ALPHA * x + 0.5 * _swiglu(x, wg_ref, wu_ref, wd_ref)
    o_ref[...] = _layer_norm(y, g_ref[0:1, :], b_ref[0:1, :])


def _ffn_ln(x, wg, wu, wd, g, b, tm):
    n = x.shape[0]
    return pl.pallas_call(
        _ffn_ln_kernel,
        out_shape=jax.ShapeDtypeStruct((n, D_MODEL), f32),
        grid=(n // tm,),
        in_specs=[pl.BlockSpec((tm, D_MODEL), lambda i: (i, 0)),
                  _const_spec(wg.shape), _const_spec(wu.shape), _const_spec(wd.shape),
                  _const_spec(g.shape), _const_spec(b.shape)],
        out_specs=pl.BlockSpec((tm, D_MODEL), lambda i: (i, 0)),
        compiler_params=_params(("parallel",)),
        name="ffn_ln",
    )(x, wg, wu, wd, g, b)


def _merge_kernel(x_ref, ar_ref, as_ref, ag_ref, wm_ref, wr_ref, ws_ref, wgd_ref, wo_ref, g_ref, b_ref, o_ref):
    x = x_ref[...]
    m = _dot(_b(x), wm_ref[...])
    yr = _dot(_b(ar_ref[...]), wr_ref[...])
    ys = _dot(_b(as_ref[...]), ws_ref[...])
    yg = _dot(_b(ag_ref[...]), wgd_ref[...])
    mixed = (jax.nn.sigmoid(m[:, 0:D_MODEL]) * yr + jax.nn.sigmoid(m[:, D_MODEL:2 * D_MODEL]) * ys
             + jax.nn.sigmoid(m[:, 2 * D_MODEL:3 * D_MODEL]) * yg)
    y = DN_ALPHA * x + _dot(_b(mixed), wo_ref[...])
    o_ref[...] = _layer_norm(y, g_ref[1:2, :], b_ref[1:2, :])


def _merge(x, ar, a_s, ag, wm, wr, ws, wgd, wo, g, b, tm):
    n = x.shape[0]
    tok = lambda i: (i, 0)
    return pl.pallas_call(
        _merge_kernel,
        out_shape=jax.ShapeDtypeStruct((n, D_MODEL), f32),
        grid=(n // tm,),
        in_specs=[pl.BlockSpec((tm, D_MODEL), tok)] * 4
                 + [_const_spec(w.shape) for w in (wm, wr, ws, wgd, wo, g, b)],
        out_specs=pl.BlockSpec((tm, D_MODEL), tok),
        compiler_params=_params(("parallel",)),
        name="merge",
    )(x, ar, a_s, ag, wm, wr, ws, wgd, wo, g, b)


def _ffn_pe_kernel(x_ref, p_ref, wg_ref, wu_ref, wd_ref, pg_ref, pp_ref, g_ref, b_ref, o_ref):
    x = x_ref[...]
    x = _layer_norm(DN_ALPHA * x + 0.5 * _swiglu(x, wg_ref, wu_ref, wd_ref), g_ref[2:3, :], b_ref[2:3, :])
    pe = jax.nn.sigmoid(_dot(_b(x), pg_ref[...])) * _dot(_b(p_ref[...]), pp_ref[...])
    o_ref[...] = _layer_norm(DN_ALPHA * x + pe, g_ref[3:4, :], b_ref[3:4, :])


def _ffn_pe(x, p, layer, wg, wu, wd, pg, pp, g, b, tm):
    n = x.shape[0]
    return pl.pallas_call(
        _ffn_pe_kernel,
        out_shape=jax.ShapeDtypeStruct((n, D_MODEL), f32),
        grid=(n // tm,),
        in_specs=[pl.BlockSpec((tm, D_MODEL), lambda i: (i, 0)),
                  pl.BlockSpec((None, tm, PLE_DIM), lambda i: (layer, i, 0))]
                 + [_const_spec(w.shape) for w in (wg, wu, wd, pg, pp, g, b)],
        out_specs=pl.BlockSpec((tm, D_MODEL), lambda i: (i, 0)),
        compiler_params=_params(("parallel",)),
        name="ffn_pe",
    )(x, p, wg, wu, wd, pg, pp, g, b)


def _rope_inplace(proj_ref, off, cos, sin, scale):
    t = proj_ref[:, off:off + R_DK]
    t = t * cos + pltpu.roll(t, R_DK // 2, axis=1) * sin
    if scale != 1.0:
        t = t * scale
    proj_ref[:, off:off + R_DK] = t


def _ret_scan_kernel(x_ref, w_ref, cos_ref, sin_ref, act_ref, st_ref, proj_ref, s_ref, *, tl, c):
    l = pl.program_id(1)

    @pl.when(l == 0)
    def _():
        s_ref[...] = jnp.zeros_like(s_ref)

    proj_ref[...] = _dot(_b(x_ref[...]), w_ref[...])
    cos, sin = cos_ref[...], sin_ref[...]
    for h in range(R_HEADS):
        _rope_inplace(proj_ref, h * R_DK, cos, sin, 1.0)
        _rope_inplace(proj_ref, R_QK + h * R_DK, cos, sin, R_DK ** -0.5)

    ii, jj = _chunk_iotas(c)
    dif = (ii - jj).astype(f32)
    ci = lax.broadcasted_iota(jnp.int32, (c, 1), 0).astype(f32)
    decay = [jnp.where(dif >= 0, jnp.exp(dif * lg), 0.0) for lg in LOG_GAMMA]
    e_col = [jnp.exp((ci + 1.0) * lg) for lg in LOG_GAMMA]
    w_col = [jnp.exp((c - 1.0 - ci) * lg) for lg in LOG_GAMMA]

    def chunk(ck):
        rows = pl.ds(pl.multiple_of(ck * c, c), c)
        hd = range(R_HEADS)
        q = [proj_ref[rows, h * R_DK:(h + 1) * R_DK] for h in hd]
        k = [proj_ref[rows, R_QK + h * R_DK:R_QK + (h + 1) * R_DK] for h in hd]
        v = [_b(proj_ref[rows, 2 * R_QK + h * R_DV:2 * R_QK + (h + 1) * R_DV]) for h in hd]
        s = [s_ref[h] for h in hd]
        scores = [_dot_nt(_b(q[h]), _b(k[h])) * decay[h] for h in hd]
        inter = [_dot(_b(q[h] * e_col[h]), _b(s[h])) for h in hd]
        for h in hd:
            s_ref[h] = s[h] * math.exp(c * LOG_GAMMA[h]) + _dot_tn(_b(k[h] * w_col[h]), v[h])
        o = [_dot(_b(scores[h]), v[h]) + inter[h] for h in hd]
        for h in hd:
            mu = jnp.mean(o[h], -1, keepdims=True)
            var = jnp.mean(jnp.square(o[h] - mu), -1, keepdims=True)
            on = (o[h] - mu) * lax.rsqrt(var + LN_EPS)
            g = proj_ref[rows, 2 * R_QK + R_VAL + h * R_DV:2 * R_QK + R_VAL + (h + 1) * R_DV]
            act_ref[rows, h * R_DV:(h + 1) * R_DV] = _b(on * jax.nn.silu(g))

    def chunks(it, carry):
        for j in range(RET_UNROLL):
            chunk(it * RET_UNROLL + j)
        return carry

    lax.fori_loop(0, tl // (c * RET_UNROLL), chunks, 0)

    @pl.when(l == pl.num_programs(1) - 1)
    def _():
        st_ref[0] = s_ref[...]


def _ret_scan(x, w, cos, sin, nb, nl_tok, tl, c):
    nl = nl_tok // tl
    tok = lambda b, l: (b * nl + l, 0)
    return pl.pallas_call(
        functools.partial(_ret_scan_kernel, tl=tl, c=c),
        out_shape=(jax.ShapeDtypeStruct((nb * nl_tok, R_VAL), bf16),
                   jax.ShapeDtypeStruct((nb, R_HEADS, R_DK, R_DV), f32)),
        grid=(nb, nl),
        in_specs=[pl.BlockSpec((tl, D_MODEL), tok), _const_spec(w.shape),
                  pl.BlockSpec((tl, R_DK), lambda b, l: (l, 0)),
                  pl.BlockSpec((tl, R_DK), lambda b, l: (l, 0))],
        out_specs=(pl.BlockSpec((tl, R_VAL), tok),
                   pl.BlockSpec((1, R_HEADS, R_DK, R_DV), lambda b, l: (b, 0, 0, 0))),
        scratch_shapes=[pltpu.VMEM((tl, w.shape[1]), f32),
                        pltpu.VMEM((R_HEADS, R_DK, R_DV), f32)],
        compiler_params=_params(("parallel", "arbitrary")),
        name="ret_scan",
    )(x, w, cos, sin)


CONV_BLK = 512


def _project_and_conv(xb, w_ref, w_off, width, xbuf_ref, cw_ref, cb_ref, dst_ref, tl, first):
    @pl.when(first)
    def _():
        xbuf_ref[0:SUBLANES, :] = jnp.zeros((SUBLANES, width), f32)

    def project(n):
        cs = slice(n * CONV_BLK, (n + 1) * CONV_BLK)
        xbuf_ref[SUBLANES:SUBLANES + tl, cs] = _dot(xb, w_ref[:, w_off + cs.start:w_off + cs.stop])

    def conv(n):
        cs = slice(n * CONV_BLK, (n + 1) * CONV_BLK)
        acc = xbuf_ref[SUBLANES:SUBLANES + tl, cs] * cw_ref[CONV_W - 1:CONV_W, cs]
        for j in range(CONV_W - 1):
            r0 = SUBLANES - (CONV_W - 1) + j
            acc = acc + xbuf_ref[r0:r0 + tl, cs] * cw_ref[j:j + 1, cs]
        if cb_ref is not None:
            acc = acc + cb_ref[:, cs]
        dst_ref[:, cs] = jax.nn.silu(acc)
        xbuf_ref[0:SUBLANES, cs] = xbuf_ref[tl:tl + SUBLANES, cs]

    nblk = width // CONV_BLK
    project(0)
    for n in range(nblk):
        if n + 1 < nblk:
            project(n + 1)
        conv(n)


def _head_expander(e2_ref, width):
    er = lax.broadcasted_iota(jnp.int32, e2_ref.shape, 0)
    el = lax.broadcasted_iota(jnp.int32, e2_ref.shape, 1)
    e2_ref[...] = _b(((er & (PAD - 1)) == (el >> (width.bit_length() - 1))).astype(f32))


def _chunk_block_masks(tri_ref, ones_ref, tl, c):
    log2c = c.bit_length() - 1
    ti = lax.broadcasted_iota(jnp.int32, (tl, tl), 0)
    tj = lax.broadcasted_iota(jnp.int32, (tl, tl), 1)
    same = ((ti >> log2c) == (tj >> log2c)).astype(f32)
    ones_ref[...] = _b(same)
    tri_ref[...] = _b(jnp.where(ti >= tj, same, 0.0))


def _chunk_cumsum(tri_ref, ones_ref, la):
    a1, a2, a3 = _split3(la)
    cum = _dot(tri_ref[...], a1) + _dot(tri_ref[...], a2) + _dot(tri_ref[...], a3)
    tot = _dot(ones_ref[...], a1) + _dot(ones_ref[...], a2) + _dot(ones_ref[...], a3)
    return cum, tot


def _expand_heads(v, e2_ref):
    hi = _b(v)
    lo = _b(v - hi.astype(f32))
    return _dot(jnp.concatenate([hi, lo], axis=1), e2_ref[...])


def _expand_heads_exact(v, e3_ref):
    return _dot(jnp.concatenate(_split3(v), axis=1), e3_ref[...])


def _ssd_scan_kernel(x_ref, w_ref, cw_ref, cb_ref, dtb_ref, alog_ref, dexp_ref, nw_ref,
                     act_ref, st_ref, conv_ref,
                     z_ref, xbuf_ref, xc_ref, cum_ref, cumx_ref, xdt_ref, xw_ref, ee_ref,
                     tri_ref, ones_ref, e3_ref, s_ref, *, tl, c):
    l = pl.program_id(1)

    @pl.when(l == 0)
    def _():
        s_ref[...] = jnp.zeros_like(s_ref)
        _chunk_block_masks(tri_ref, ones_ref, tl, c)
        _head_expander(e3_ref, M_HEADDIM)

    xb = _b(x_ref[...])
    dt = jax.nn.softplus(_dot(xb, w_ref[:, M_INNER + M_CONV_DIM:]) + dtb_ref[...])
    _project_and_conv(xb, w_ref, M_INNER, M_CONV_DIM, xbuf_ref, cw_ref, cb_ref, xc_ref, tl, l == 0)
    z_ref[...] = _dot(xb, w_ref[:, 0:M_INNER])
    la = -jnp.exp(alog_ref[...]) * dt
    cum, tot = _chunk_cumsum(tri_ref, ones_ref, la)
    cum_ref[...] = cum
    cumx_ref[...] = _expand_heads_exact(cum, e3_ref)
    e2_ref = e3_ref.at[0:2 * PAD, :]
    xdt = xc_ref[:, 0:M_INNER] * _expand_heads(dt, e2_ref)
    xdt_ref[...] = xdt
    xw_ref[...] = xdt * _expand_heads(jnp.exp(tot - cum), e2_ref)
    ee_ref[...] = _expand_heads(jnp.exp(cum), e2_ref)

    half = c
    lane = lax.broadcasted_iota(jnp.int32, (c, 2 * half), 1)
    rowi = lax.broadcasted_iota(jnp.int32, (c, 2 * half), 0)
    left = lane < half
    causal2 = (lane & (half - 1)) <= rowi
    b_off, c_off = M_INNER, M_INNER + M_GROUPS * M_STATE
    gr = range(M_GROUPS)
    pairs = [(g, pp) for g in gr for pp in range(M_HPG // 2)]

    def chunk(ck):
        r0 = pl.multiple_of(ck * c, c)
        rows = pl.ds(r0, c)
        cum_c = cum_ref[rows, :]
        cum_t = jnp.concatenate([cum_c, cum_c], axis=0).T
        bb = [_b(xc_ref[rows, b_off + g * M_STATE:b_off + (g + 1) * M_STATE]) for g in gr]
        cb = [_b(xc_ref[rows, c_off + g * M_STATE:c_off + (g + 1) * M_STATE]) for g in gr]
        g2 = [_dot_nt(cb[g], jnp.concatenate([bb[g], bb[g]], axis=0)) for g in gr]
        sg = [s_ref[g] for g in gr]
        inter = [ee_ref[rows, g * M_GW:(g + 1) * M_GW] * _dot(cb[g], _b(sg[g])) for g in gr]
        for g in gr:
            e_last = ee_ref[pl.ds(r0 + c - 1, 1), g * M_GW:(g + 1) * M_GW]
            s_ref[g] = sg[g] * e_last + _dot_tn(bb[g], _b(xw_ref[rows, g * M_GW:(g + 1) * M_GW]))
        a2s, rhs = [], []
        for g, pp in pairs:
            h0 = g * M_HPG + 2 * pp
            ls = slice(h0 * M_HEADDIM, (h0 + 2) * M_HEADDIM)
            colsel = cumx_ref[rows, ls]
            rowsel = jnp.where(left[0:1, :], cum_t[h0:h0 + 1, :], cum_t[h0 + 1:h0 + 2, :])
            d2 = jnp.where(causal2, jnp.exp(colsel - rowsel), 0.0)
            a2s.append(_b(g2[g] * d2))
            xp = xdt_ref[rows, ls]
            rhs.append(_b(jnp.concatenate([jnp.where(left, xp, 0.0), jnp.where(left, 0.0, xp)], axis=0)))
        intra = [_dot(a, r) for a, r in zip(a2s, rhs)]
        ys = []
        for n, (g, pp) in enumerate(pairs):
            h0 = g * M_HPG + 2 * pp
            ls = slice(h0 * M_HEADDIM, (h0 + 2) * M_HEADDIM)
            y = intra[n] + inter[g][:, pp * 2 * M_HEADDIM:(pp + 1) * 2 * M_HEADDIM] + dexp_ref[:, ls] * xc_ref[rows, ls]
            ys.append(y * jax.nn.silu(z_ref[rows, ls]))
        for g in gr:
            mine = [n for n, (gg, _) in enumerate(pairs) if gg == g]
            ms = sum(jnp.sum(jnp.square(ys[n]), -1, keepdims=True) for n in mine) * (1.0 / M_GW)
            r = lax.rsqrt(ms + NORM_EPS)
            for n in mine:
                h0 = g * M_HPG + 2 * pairs[n][1]
                ls = slice(h0 * M_HEADDIM, (h0 + 2) * M_HEADDIM)
                act_ref[rows, ls] = _b(ys[n] * r * nw_ref[:, ls])

    def chunks(it, carry):
        for j in range(SSD_UNROLL):
            chunk(it * SSD_UNROLL + j)
        return carry

    lax.fori_loop(0, tl // (c * SSD_UNROLL), chunks, 0)

    @pl.when(l == pl.num_programs(1) - 1)
    def _():
        for h in range(M_HEADS):
            g, hh = divmod(h, M_HPG)
            st_ref[0, h] = s_ref[g][:, hh * M_HEADDIM:(hh + 1) * M_HEADDIM].T
        conv_ref[0] = xbuf_ref[SUBLANES - (CONV_W - 1):SUBLANES, :]


def _ssd_scan(x, w, cw, cb, dtb, alog, dexp, nw, nb, nl_tok, tl, c):
    assert 2 * c == LANES and 2 * M_HEADDIM == LANES, "head pairs are packed into one 128-lane slab"
    nl = nl_tok // tl
    tok = lambda b, l: (b * nl + l, 0)
    return pl.pallas_call(
        functools.partial(_ssd_scan_kernel, tl=tl, c=c),
        out_shape=(jax.ShapeDtypeStruct((nb * nl_tok, M_INNER), bf16),
                   jax.ShapeDtypeStruct((nb, M_HEADS, M_HEADDIM, M_STATE), f32),
                   jax.ShapeDtypeStruct((nb, CONV_W - 1, M_CONV_DIM), f32)),
        grid=(nb, nl),
        in_specs=[pl.BlockSpec((tl, D_MODEL), tok)]
                 + [_const_spec(a.shape) for a in (w, cw, cb, dtb, alog, dexp, nw)],
        out_specs=(pl.BlockSpec((tl, M_INNER), tok),
                   pl.BlockSpec((1, M_HEADS, M_HEADDIM, M_STATE), lambda b, l: (b, 0, 0, 0)),
                   pl.BlockSpec((1, CONV_W - 1, M_CONV_DIM), lambda b, l: (b, 0, 0))),
        scratch_shapes=[pltpu.VMEM((tl, M_INNER), f32),
                        pltpu.VMEM((tl + SUBLANES, M_CONV_DIM), f32),
                        pltpu.VMEM((tl, M_CONV_DIM), f32),
                        pltpu.VMEM((tl, PAD), f32),
                        pltpu.VMEM((tl, M_INNER), f32),
                        pltpu.VMEM((tl, M_INNER), f32),
                        pltpu.VMEM((tl, M_INNER), f32),
                        pltpu.VMEM((tl, M_INNER), f32),
                        pltpu.VMEM((tl, tl), bf16),
                        pltpu.VMEM((tl, tl), bf16),
                        pltpu.VMEM((3 * PAD, M_INNER), bf16),
                        pltpu.VMEM((M_GROUPS, M_STATE, M_GW), f32)],
        compiler_params=_params(("parallel", "arbitrary")),
        name="ssd_scan",
    )(x, w, cw, cb, dtb, alog, dexp, nw)


def _pair_blockdiag(x, left):
    return jnp.concatenate([jnp.where(left, x, jnp.zeros_like(x)), jnp.where(left, jnp.zeros_like(x), x)], axis=0)


def _gdn_scan_kernel(x_ref, w_ref, cw_ref, dtb_ref, alog_ref, nw_ref,
                     act_ref, st_ref, conv_ref,
                     gz_ref, xbuf_ref, qkv_ref, cum_ref, ee_ref,
                     q16_ref, k16_ref, kb16_ref, qe16_ref, kbe16_ref, kw16_ref, vb16_ref,
                     wy_ref, u0_ref, attn_ref, tri_ref, ones_ref, e2_ref, s_ref, *, tl, c):
    l = pl.program_id(1)
    n_pairs = G_HEADS // 2
    pw = 2 * G_DK

    @pl.when(l == 0)
    def _():
        s_ref[...] = jnp.zeros_like(s_ref)
        _chunk_block_masks(tri_ref, ones_ref, tl, c)
        _head_expander(e2_ref, G_DK)

    xb = _b(x_ref[...])
    ab = _dot(xb, w_ref[:, G_QKV + G_VAL:])
    _project_and_conv(xb, w_ref, 0, G_QKV, xbuf_ref, cw_ref, None, qkv_ref, tl, l == 0)
    gz_ref[...] = _dot(xb, w_ref[:, G_QKV:G_QKV + G_VAL])
    g = -jnp.exp(alog_ref[...]) * jax.nn.softplus(ab + dtb_ref[...])
    cum, tot = _chunk_cumsum(tri_ref, ones_ref, g)
    cum_ref[...] = cum
    beta = pltpu.roll(jax.nn.sigmoid(ab), PAD - G_HEADS, axis=1)
    e_c, w_c = jnp.exp(cum), jnp.exp(tot - cum)
    for hb in range(n_pairs):
        ls = slice(hb * pw, (hb + 1) * pw)
        e_x = _expand_heads(e_c, e2_ref.at[:, ls])
        w_x = _expand_heads(w_c, e2_ref.at[:, ls])
        b_x = _expand_heads(beta, e2_ref.at[:, ls])
        ee_ref[:, ls] = e_x
        qn, kn = [], []
        for t in range(2):
            hs = slice((2 * hb + t) * G_DK, (2 * hb + t + 1) * G_DK)
            qt, kt = qkv_ref[:, hs], qkv_ref[:, G_KEY + hs.start:G_KEY + hs.stop]
            qn.append(qt * lax.rsqrt(jnp.sum(jnp.square(qt), -1, keepdims=True) + NORM_EPS) * (G_DK ** -0.5))
            kn.append(kt * lax.rsqrt(jnp.sum(jnp.square(kt), -1, keepdims=True) + NORM_EPS))
        q, k = jnp.concatenate(qn, axis=1), jnp.concatenate(kn, axis=1)
        kb = k * b_x
        q16_ref[:, ls] = _b(q)
        k16_ref[:, ls] = _b(k)
        kb16_ref[:, ls] = _b(kb)
        qe16_ref[:, ls] = _b(q * e_x)
        kbe16_ref[:, ls] = _b(kb * e_x)
        kw16_ref[:, ls] = _b(k * w_x)
        vb16_ref[:, ls] = _b(qkv_ref[:, 2 * G_KEY + hb * pw:2 * G_KEY + (hb + 1) * pw] * b_x)

    lane = lax.broadcasted_iota(jnp.int32, (c, 2 * c), 1)
    rowi = lax.broadcasted_iota(jnp.int32, (c, 2 * c), 0)
    left = lane < c
    jloc = lane & (c - 1)
    causal2, strict2 = jloc <= rowi, jloc < rowi
    eye2 = (jloc == rowi).astype(f32)
    left_w = lax.broadcasted_iota(jnp.int32, (c, pw), 1) < G_DK
    left_s = lax.broadcasted_iota(jnp.int32, (G_DK, pw), 1) < G_DK
    pr = range(n_pairs)

    def precompute(it, carry):
        cks = [it * GDN_PRE_UNROLL + j for j in range(GDN_PRE_UNROLL)]
        rows = [pl.ds(pl.multiple_of(ck * c, c), c) for ck in cks]
        lsl = [slice(p * pw, (p + 1) * pw) for p in pr]
        cp = [(j, p) for j in range(GDN_PRE_UNROLL) for p in pr]
        cum_c = [cum_ref[r, :] for r in rows]
        cum_t = [jnp.concatenate([x, x], axis=0).T for x in cum_c]
        d2 = []
        for j, p in cp:
            colsel = jnp.where(left, cum_c[j][:, 2 * p:2 * p + 1], cum_c[j][:, 2 * p + 1:2 * p + 2])
            rowsel = jnp.where(left[0:1, :], cum_t[j][2 * p:2 * p + 1, :], cum_t[j][2 * p + 1:2 * p + 2, :])
            d2.append(jnp.where(causal2, jnp.exp(colsel - rowsel), 0.0))
        kbd = [_pair_blockdiag(k16_ref[rows[j], lsl[p]], left_w) for j, p in cp]
        lm = [jnp.where(strict2, _dot_nt(kb16_ref[rows[j], lsl[p]], kbd[n]) * d2[n], 0.0)
              for n, (j, p) in enumerate(cp)]
        attn = [_dot_nt(q16_ref[rows[j], lsl[p]], kbd[n]) * d2[n] for n, (j, p) in enumerate(cp)]
        for n, (j, p) in enumerate(cp):
            attn_ref[rows[j], p * 2 * c:(p + 1) * 2 * c] = _b(attn[n])
        ps = [eye2 - x for x in lm]
        ms = lm
        kpow = 2
        while kpow < c:
            ms = [_dot(_b(m), _b(_pair_blockdiag(m, left))) for m in ms]
            ps = [x + _dot(_b(x), _b(_pair_blockdiag(m, left))) for x, m in zip(ps, ms)]
            kpow *= 2
        rhs = [jnp.concatenate([_pair_blockdiag(kbe16_ref[rows[j], lsl[p]], left_w),
                                _pair_blockdiag(vb16_ref[rows[j], lsl[p]], left_w)], axis=1) for j, p in cp]
        wu = [_dot(_b(ps[n]), rhs[n]) for n in range(len(cp))]
        for n, (j, p) in enumerate(cp):
            wy_ref[rows[j], lsl[p]] = _b(wu[n][:, 0:pw])
            u0_ref[rows[j], lsl[p]] = wu[n][:, pw:2 * pw]
        return carry

    lax.fori_loop(0, tl // (c * GDN_PRE_UNROLL), precompute, 0)

    def scan_chunk(ck):
        r0 = pl.multiple_of(ck * c, c)
        rows = pl.ds(r0, c)
        lsl = [slice(p * pw, (p + 1) * pw) for p in pr]
        sp = [s_ref[p] for p in pr]
        sbd = [_b(_pair_blockdiag(sp[p], left_s)) for p in pr]
        r = [_dot(jnp.concatenate([wy_ref[rows, lsl[p]], qe16_ref[rows, lsl[p]]], axis=0), sbd[p]) for p in pr]
        u = [u0_ref[rows, lsl[p]] - r[p][0:c, :] for p in pr]
        ubd = [_b(_pair_blockdiag(u[p], left_w)) for p in pr]
        for p in pr:
            kw = kw16_ref[rows, lsl[p]]
            kw_stack = jnp.concatenate([kw[:, 0:G_DK], kw[:, G_DK:pw]], axis=0)
            e_last = ee_ref[pl.ds(r0 + c - 1, 1), lsl[p]]
            s_ref[p] = sp[p] * e_last + _dot_tn(kw_stack, ubd[p])
        o = [r[p][c:2 * c, :] + _dot(attn_ref[rows, p * 2 * c:(p + 1) * 2 * c], ubd[p]) for p in pr]
        for p in pr:
            for t in range(2):
                hs = slice((2 * p + t) * G_DV, (2 * p + t + 1) * G_DV)
                act_ref[rows, hs] = _b(_rms(o[p][:, t * G_DV:(t + 1) * G_DV]) * nw_ref[...]
                                       * jax.nn.silu(gz_ref[rows, hs]))

    def scan(it, carry):
        for j in range(GDN_SCAN_UNROLL):
            scan_chunk(it * GDN_SCAN_UNROLL + j)
        return carry

    lax.fori_loop(0, tl // (c * GDN_SCAN_UNROLL), scan, 0)

    @pl.when(l == pl.num_programs(1) - 1)
    def _():
        for h in range(G_HEADS):
            st_ref[0, h] = s_ref[h // 2][:, (h % 2) * G_DV:(h % 2 + 1) * G_DV]
        conv_ref[0] = xbuf_ref[SUBLANES - (CONV_W - 1):SUBLANES, :]


def _gdn_scan(x, w, cw, dtb, alog, nw, nb, nl_tok, tl, c):
    assert 2 * c == LANES and G_DK == G_DV == LANES, "two heads' (c, c) blocks share one 128-lane slab"
    nl = nl_tok // tl
    tok = lambda b, l: (b * nl + l, 0)
    return pl.pallas_call(
        functools.partial(_gdn_scan_kernel, tl=tl, c=c),
        out_shape=(jax.ShapeDtypeStruct((nb * nl_tok, G_VAL), bf16),
                   jax.ShapeDtypeStruct((nb, G_HEADS, G_DK, G_DV), f32),
                   jax.ShapeDtypeStruct((nb, CONV_W - 1, G_QKV), f32)),
        grid=(nb, nl),
        in_specs=[pl.BlockSpec((tl, D_MODEL), tok)]
                 + [_const_spec(a.shape) for a in (w, cw, dtb, alog, nw)],
        out_specs=(pl.BlockSpec((tl, G_VAL), tok),
                   pl.BlockSpec((1, G_HEADS, G_DK, G_DV), lambda b, l: (b, 0, 0, 0)),
                   pl.BlockSpec((1, CONV_W - 1, G_QKV), lambda b, l: (b, 0, 0))),
        scratch_shapes=[pltpu.VMEM((tl, G_VAL), f32),
                        pltpu.VMEM((tl + SUBLANES, G_QKV), f32),
                        pltpu.VMEM((tl, G_QKV), f32),
                        pltpu.VMEM((tl, PAD), f32),
                        pltpu.VMEM((tl, G_KEY), f32),
                        ] + [pltpu.VMEM((tl, G_KEY), bf16)] * 7 + [
                        pltpu.VMEM((tl, G_KEY), bf16),
                        pltpu.VMEM((tl, G_VAL), f32),
                        pltpu.VMEM((tl, G_HEADS * c), bf16),
                        pltpu.VMEM((tl, tl), bf16),
                        pltpu.VMEM((tl, tl), bf16),
                        pltpu.VMEM((2 * PAD, G_KEY), bf16),
                        pltpu.VMEM((G_HEADS // 2, G_DK, 2 * G_DV), f32)],
        compiler_params=_params(("parallel", "arbitrary")),
        name="gdn_scan",
    )(x, w, cw, dtb, alog, nw)


def _token_lanes_to_front(src_ref, dst_ref, i):
    n = src_ref.shape[1]
    dst_ref[...] = pltpu.roll(src_ref[...], (n - i * DEC_BT) % n, axis=1)


def _conv_step(cst_ref, x_new, cw_ref, cb_ref, conv_out_ref):
    acc = x_new * cw_ref[CONV_W - 1:CONV_W, :]
    for j in range(CONV_W - 1):
        acc = acc + cst_ref[j] * cw_ref[j:j + 1, :]
    if cb_ref is not None:
        acc = acc + cb_ref[...]
    for j in range(CONV_W - 2):
        conv_out_ref[j] = cst_ref[j + 1]
    conv_out_ref[CONV_W - 2] = x_new
    return jax.nn.silu(acc)


def _state_specs(state, prev, layer, blk):
    zeros = (0,) * (len(blk) - 2)
    spec = pl.BlockSpec(blk, lambda i: (layer, i) + zeros)
    if prev is None:
        prev = jnp.zeros((SUBLANES, LANES), f32)
        alias = {}
    else:
        alias = None
    return spec, prev, alias


def _dec_ret_kernel(x_ref, w_ref, cos_ref, sin_ref, s_in, prev_ref, act_ref, s_out,
                    q_ref, kt_ref, vg_ref, ks_ref, o_ref):
    del prev_ref
    i = pl.program_id(0)

    @pl.when(i == 0)
    def _():
        proj = _dot(_b(x_ref[...]), w_ref[...])
        cos, sin = cos_ref[...], sin_ref[...]
        for h in range(R_HEADS):
            hs = slice(h * R_DK, (h + 1) * R_DK)
            t = proj[:, hs]
            q_ref[:, hs] = t * cos + pltpu.roll(t, R_DK // 2, axis=1) * sin
            t = proj[:, R_QK + h * R_DK:R_QK + (h + 1) * R_DK]
            q_ref[:, R_QK + h * R_DK:R_QK + (h + 1) * R_DK] = t = (
                t * cos + pltpu.roll(t, R_DK // 2, axis=1) * sin) * (R_DK ** -0.5)
            kt_ref[hs, :] = t.T
        vg_ref[...] = proj[:, 2 * R_QK:]

    _token_lanes_to_front(kt_ref, ks_ref, i)
    rows = pl.ds(pl.multiple_of(i * DEC_BT, DEC_BT), DEC_BT)
    vg, qk8 = vg_ref[rows, :], q_ref[rows, :]
    for h in range(R_HEADS):
        q8 = qk8[:, h * R_DK:(h + 1) * R_DK]
        k8 = qk8[:, R_QK + h * R_DK:R_QK + (h + 1) * R_DK]
        qk = jnp.sum(q8 * k8, axis=1, keepdims=True)
        lhs = _b(jnp.concatenate([q8, q8], axis=0))
        gam = math.exp(LOG_GAMMA[h])
        for j in range(DEC_BT):
            kc = ks_ref[h * R_DK:(h + 1) * R_DK, j:j + 1]
            v_row = vg[j:j + 1, h * R_DV:(h + 1) * R_DV]
            s = s_in[j, h]
            qs = _dot(lhs, _b(s))[j:j + 1, :]
            o_ref[j:j + 1, h * R_DV:(h + 1) * R_DV] = gam * qs + qk[j:j + 1, :] * v_row
            s_out[j, h] = s * gam + kc * v_row
    for h in range(R_HEADS):
        o = o_ref[:, h * R_DV:(h + 1) * R_DV]
        mu = jnp.mean(o, -1, keepdims=True)
        var = jnp.mean(jnp.square(o - mu), -1, keepdims=True)
        g = vg[:, R_VAL + h * R_DV:R_VAL + (h + 1) * R_DV]
        act_ref[:, h * R_DV:(h + 1) * R_DV] = (o - mu) * lax.rsqrt(var + LN_EPS) * jax.nn.silu(g)


def _dec_ret(x, w, cos, sin, state, layer, prev):
    n = x.shape[0]
    consts = (x, w, cos, sin)
    sspec, prev, alias = _state_specs(state, prev, layer, (None, DEC_BT, R_HEADS, R_DK, R_DV))
    return pl.pallas_call(
        _dec_ret_kernel,
        out_shape=(jax.ShapeDtypeStruct((n, R_VAL), f32), jax.ShapeDtypeStruct(state.shape, f32)),
        grid=(n // DEC_BT,),
        in_specs=[_const_spec(a.shape) for a in consts] + [sspec, pl.BlockSpec(memory_space=pl.ANY)],
        out_specs=(pl.BlockSpec((DEC_BT, R_VAL), lambda i: (i, 0)), sspec),
        scratch_shapes=[pltpu.VMEM((n, 2 * R_QK), f32),
                        pltpu.VMEM((R_QK, n), f32),
                        pltpu.VMEM((n, 2 * R_VAL), f32),
                        pltpu.VMEM((R_QK, n), f32),
                        pltpu.VMEM((DEC_BT, R_VAL), f32)],
        input_output_aliases={len(consts) + 1: 1} if alias is None else alias,
        compiler_params=_params(("arbitrary",)),
        name="dec_ret",
    )(*consts, state, prev)


def _dec_ssd_kernel(x_ref, w_ref, cst_ref, cw_ref, cb_ref, dtb_ref, alog_ref, dexp_ref, nw_ref, s_in, prev_ref,
                    act_ref, s_out, conv_out_ref,
                    xc_ref, z_ref, xdt_ref, xdtt_ref, ela_ref, elax_ref, xts_ref, e2_ref, o_ref):
    del prev_ref
    i = pl.program_id(0)

    @pl.when(i == 0)
    def _():
        _head_expander(e2_ref, M_HEADDIM)
        proj = _dot(_b(x_ref[...]), w_ref[...])
        xc = _conv_step(cst_ref, proj[:, M_INNER:M_INNER + M_CONV_DIM], cw_ref, cb_ref, conv_out_ref)
        xc_ref[...] = xc
        z_ref[...] = proj[:, 0:M_INNER]
        dt = jax.nn.softplus(proj[:, M_INNER + M_CONV_DIM:M_INNER + M_CONV_DIM + PAD] + dtb_ref[...])
        ela = jnp.exp(-jnp.exp(alog_ref[...]) * dt)
        ela_ref[...] = ela
        elax_ref[...] = _expand_heads(ela, e2_ref)
        xdt = xc[:, 0:M_INNER] * _expand_heads(dt, e2_ref)
        xdt_ref[...] = xdt
        xdtt_ref[...] = xdt.T

    _token_lanes_to_front(xdtt_ref, xts_ref, i)
    rows = pl.ds(pl.multiple_of(i * DEC_BT, DEC_BT), DEC_BT)
    xc8, xdt8, ela8, elax8 = xc_ref[rows, :], xdt_ref[rows, :], ela_ref[rows, :], elax_ref[rows, :]
    b_off, c_off = M_INNER, M_INNER + M_GROUPS * M_STATE
    for g in range(M_GROUPS):
        gs = slice(g * M_GW, (g + 1) * M_GW)
        b8 = xc8[:, b_off + g * M_STATE:b_off + (g + 1) * M_STATE]
        c8 = xc8[:, c_off + g * M_STATE:c_off + (g + 1) * M_STATE]
        cb = jnp.sum(c8 * b8, axis=1, keepdims=True)
        lhs = _b(jnp.concatenate([c8, c8], axis=0))
        for j in range(DEC_BT):
            st = s_in[j, g * M_HPG:(g + 1) * M_HPG]
            cs = _dot_nt(lhs, _b(st.reshape(M_GW, M_STATE)))[j:j + 1, :]
            o_ref[j:j + 1, gs] = cb[j:j + 1, :] * xdt8[j:j + 1, gs] + elax8[j:j + 1, gs] * cs
            b_row = b8[j:j + 1, :]
            for hh in range(M_HPG):
                h = g * M_HPG + hh
                xdt_col = xts_ref[h * M_HEADDIM:(h + 1) * M_HEADDIM, j:j + 1]
                s_out[j, h] = st[hh] * ela8[j:j + 1, h:h + 1] + xdt_col * b_row
    y = (o_ref[...] + dexp_ref[...] * xc8[:, 0:M_INNER]) * jax.nn.silu(z_ref[rows, :])
    for g in range(M_GROUPS):
        gs = slice(g * M_GW, (g + 1) * M_GW)
        act_ref[:, gs] = _rms(y[:, gs]) * nw_ref[:, gs]


def _dec_ssd(x, w, cst, cw, cb, dtb, alog, dexp, nw, state, layer, prev):
    n = x.shape[0]
    consts = (x, w, cst, cw, cb, dtb, alog, dexp, nw)
    sspec, prev, alias = _state_specs(state, prev, layer, (None, DEC_BT, M_HEADS, M_HEADDIM, M_STATE))
    cspecs = [_const_spec(a.shape) for a in consts]
    cspecs[2] = pl.BlockSpec((None,) + cst.shape[1:], lambda i: (layer, 0, 0, 0), pipeline_mode=pl.Buffered(1))
    return pl.pallas_call(
        _dec_ssd_kernel,
        out_shape=(jax.ShapeDtypeStruct((n, M_INNER), f32), jax.ShapeDtypeStruct(state.shape, f32),
                   jax.ShapeDtypeStruct(cst.shape[1:], f32)),
        grid=(n // DEC_BT,),
        in_specs=cspecs + [sspec, pl.BlockSpec(memory_space=pl.ANY)],
        out_specs=(pl.BlockSpec((DEC_BT, M_INNER), lambda i: (i, 0)), sspec,
                   pl.BlockSpec(cst.shape[1:], lambda i: (0, 0, 0))),
        scratch_shapes=[pltpu.VMEM((n, M_CONV_DIM), f32),
                        pltpu.VMEM((n, M_INNER), f32),
                        pltpu.VMEM((n, M_INNER), f32),
                        pltpu.VMEM((M_INNER, n), f32),
                        pltpu.VMEM((n, PAD), f32),
                        pltpu.VMEM((n, M_INNER), f32),
                        pltpu.VMEM((M_INNER, n), f32),
                        pltpu.VMEM((2 * PAD, M_INNER), bf16),
                        pltpu.VMEM((DEC_BT, M_INNER), f32)],
        input_output_aliases={len(consts) + 1: 1} if alias is None else alias,
        compiler_params=_params(("arbitrary",)),
        name="dec_ssd",
    )(*consts, state, prev)


def _dec_gdn_kernel(x_ref, w_ref, cst_ref, cw_ref, dtb_ref, alog_ref, nw_ref, s_in, prev_ref,
                    act_ref, s_out, conv_out_ref,
                    qkv_ref, kt_ref, gz_ref, eg_ref, beta_ref, ks_ref, o_ref):
    del prev_ref
    i = pl.program_id(0)

    @pl.when(i == 0)
    def _():
        proj = _dot(_b(x_ref[...]), w_ref[...])
        qkv = _conv_step(cst_ref, proj[:, 0:G_QKV], cw_ref, None, conv_out_ref)
        for h in range(G_HEADS):
            hs = slice(h * G_DK, (h + 1) * G_DK)
            q = qkv[:, hs]
            qkv_ref[:, hs] = q * lax.rsqrt(jnp.sum(jnp.square(q), -1, keepdims=True) + NORM_EPS) * (G_DK ** -0.5)
            k = qkv[:, G_KEY + h * G_DK:G_KEY + (h + 1) * G_DK]
            k = k * lax.rsqrt(jnp.sum(jnp.square(k), -1, keepdims=True) + NORM_EPS)
            qkv_ref[:, G_KEY + h * G_DK:G_KEY + (h + 1) * G_DK] = k
            kt_ref[hs, :] = k.T
        qkv_ref[:, 2 * G_KEY:] = qkv[:, 2 * G_KEY:]
        gz_ref[...] = proj[:, G_QKV:G_QKV + G_VAL]
        ab = proj[:, G_QKV + G_VAL:G_QKV + G_VAL + PAD]
        eg_ref[...] = jnp.exp(-jnp.exp(alog_ref[...]) * jax.nn.softplus(ab + dtb_ref[...]))
        beta_ref[...] = jax.nn.sigmoid(ab)

    _token_lanes_to_front(kt_ref, ks_ref, i)
    rows = pl.ds(pl.multiple_of(i * DEC_BT, DEC_BT), DEC_BT)
    qkv8, eg8, beta8 = qkv_ref[rows, :], eg_ref[rows, :], beta_ref[rows, :]
    for h in range(G_HEADS):
        hs = slice(h * G_DV, (h + 1) * G_DV)
        q8 = qkv8[:, h * G_DK:(h + 1) * G_DK]
        k8 = qkv8[:, G_KEY + h * G_DK:G_KEY + (h + 1) * G_DK]
        v8 = qkv8[:, 2 * G_KEY + h * G_DV:2 * G_KEY + (h + 1) * G_DV]
        qk = jnp.sum(q8 * k8, axis=1, keepdims=True)
        lhs = _b(jnp.concatenate([q8, k8], axis=0))
        bh = beta8[:, G_HEADS + h:G_HEADS + h + 1]
        eg = eg8[:, h:h + 1]
        for j in range(DEC_BT):
            kc = ks_ref[h * G_DK:(h + 1) * G_DK, j:j + 1]
            s = s_in[j, h]
            qks = _dot(lhs, _b(s))
            bj, ej = bh[j:j + 1, :], eg[j:j + 1, :]
            u = v8[j:j + 1, :] * bj - (bj * ej) * qks[DEC_BT + j:DEC_BT + j + 1, :]
            o_ref[j:j + 1, hs] = ej * qks[j:j + 1, :] + qk[j:j + 1, :] * u
            s_out[j, h] = s * ej + kc * u
    gz8 = gz_ref[rows, :]
    for h in range(G_HEADS):
        hs = slice(h * G_DV, (h + 1) * G_DV)
        act_ref[:, hs] = _rms(o_ref[:, hs]) * nw_ref[...] * jax.nn.silu(gz8[:, hs])


def _dec_gdn(x, w, cst, cw, dtb, alog, nw, state, layer, prev):
    n = x.shape[0]
    consts = (x, w, cst, cw, dtb, alog, nw)
    sspec, prev, alias = _state_specs(state, prev, layer, (None, DEC_BT, G_HEADS, G_DK, G_DV))
    cspecs = [_const_spec(a.shape) for a in consts]
    cspecs[2] = pl.BlockSpec((None,) + cst.shape[1:], lambda i: (layer, 0, 0, 0), pipeline_mode=pl.Buffered(1))
    return pl.pallas_call(
        _dec_gdn_kernel,
        out_shape=(jax.ShapeDtypeStruct((n, G_VAL), f32), jax.ShapeDtypeStruct(state.shape, f32),
                   jax.ShapeDtypeStruct(cst.shape[1:], f32)),
        grid=(n // DEC_BT,),
        in_specs=cspecs + [sspec, pl.BlockSpec(memory_space=pl.ANY)],
        out_specs=(pl.BlockSpec((DEC_BT, G_VAL), lambda i: (i, 0)), sspec,
                   pl.BlockSpec(cst.shape[1:], lambda i: (0, 0, 0))),
        scratch_shapes=[pltpu.VMEM((n, G_QKV), f32),
                        pltpu.VMEM((G_KEY, n), f32),
                        pltpu.VMEM((n, G_VAL), f32),
                        pltpu.VMEM((n, PAD), f32), pltpu.VMEM((n, PAD), f32),
                        pltpu.VMEM((G_KEY, n), f32),
                        pltpu.VMEM((DEC_BT, G_VAL), f32)],
        input_output_aliases={len(consts) + 1: 1} if alias is None else alias,
        compiler_params=_params(("arbitrary",)),
        name="dec_gdn",
    )(*consts, state, prev)


def _rope_tables(pos):
    half = R_DK // 2
    inv = ROPE_BASE ** (-jnp.arange(half, dtype=f32) / half)
    ang = pos[:, None] * inv[None, :]
    cos, sin = jnp.cos(ang), jnp.sin(ang)
    return jnp.concatenate([cos, cos], axis=1), jnp.concatenate([-sin, sin], axis=1)


def _lane_pad(v, start=0):
    return jnp.zeros((1, PAD), f32).at[0, start:start + v.shape[0]].set(v)


def _layer_weights(i, prm):
    w_in = prm["w_in"][i]
    zpad = jnp.zeros((D_MODEL, PAD - M_HEADS), f32)
    w_ssd = jnp.concatenate([w_in[:, OFF_SSD:OFF_GDN], zpad], axis=1)
    w_gdn = jnp.concatenate([w_in[:, OFF_GDN:OFF_MERGE], jnp.zeros((D_MODEL, PAD - 2 * G_HEADS), f32)], axis=1)
    return dict(
        ln_g=prm["ln_g"][i], ln_b=prm["ln_b"][i],
        wg=_b(prm["ffn_wg"][i]), wu=_b(prm["ffn_wu"][i]), wd=_b(prm["ffn_wd"][i]),
        w_ret=_b(w_in[:, OFF_RET:OFF_SSD]), w_ssd=_b(w_ssd), w_gdn=_b(w_gdn), w_merge=_b(w_in[:, OFF_MERGE:]),
        ssm_cw=prm["ssm_conv_w"][i], ssm_cb=prm["ssm_conv_b"][i][None, :],
        ssm_dtb=_lane_pad(prm["ssm_dt_bias"][i]), ssm_alog=_lane_pad(prm["ssm_a_log"][i]),
        ssm_dexp=jnp.repeat(prm["ssm_d"][i], M_HEADDIM)[None, :], ssm_nw=prm["ssm_norm_w"][i][None, :],
        gdn_cw=prm["gdn_conv_w"][i],
        gdn_dtb=_lane_pad(prm["gdn_dt_bias"][i]), gdn_alog=_lane_pad(prm["gdn_a_log"][i]),
        gdn_nw=prm["gdn_norm_w"][i][None, :],
        w_ro=_b(prm["w_ret_out"][i]), w_so=_b(prm["w_ssm_out"][i]), w_go=_b(prm["w_gdn_out"][i]),
        w_o=_b(prm["w_o"][i]), pe_proj=_b(prm["pe_proj"][i]), pe_gate=_b(prm["pe_gate"][i]),
    )


def _post_mix(x1, acts, p, i, w, tm):
    x2 = _merge(x1, *acts, w["w_merge"], w["w_ro"], w["w_so"], w["w_go"], w["w_o"], w["ln_g"], w["ln_b"], tm)
    return _ffn_pe(x2, p, i, w["wg"][1], w["wu"][1], w["wd"][1], w["pe_gate"], w["pe_proj"],
                   w["ln_g"], w["ln_b"], tm)


def kernel(x_prompt, x_sample, state_ret, state_ssm, state_ssm_conv, state_gdn, state_gdn_conv,
           p_prompt, p_sample, ln_g, ln_b, ffn_wg, ffn_wu, ffn_wd, w_in,
           ssm_conv_w, ssm_conv_b, ssm_dt_bias, ssm_a_log, ssm_d, ssm_norm_w,
           gdn_conv_w, gdn_dt_bias, gdn_a_log, gdn_norm_w,
           w_ret_out, w_ssm_out, w_gdn_out, w_o, pe_proj, pe_gate):
    prm = dict(ln_g=ln_g, ln_b=ln_b, ffn_wg=ffn_wg, ffn_wu=ffn_wu, ffn_wd=ffn_wd, w_in=w_in,
               ssm_conv_w=ssm_conv_w, ssm_conv_b=ssm_conv_b, ssm_dt_bias=ssm_dt_bias,
               ssm_a_log=ssm_a_log, ssm_d=ssm_d, ssm_norm_w=ssm_norm_w,
               gdn_conv_w=gdn_conv_w, gdn_dt_bias=gdn_dt_bias, gdn_a_log=gdn_a_log,
               gdn_norm_w=gdn_norm_w, w_ret_out=w_ret_out, w_ssm_out=w_ssm_out,
               w_gdn_out=w_gdn_out, w_o=w_o, pe_proj=pe_proj, pe_gate=pe_gate)
    nb, seq, _ = x_prompt.shape
    ns = x_sample.shape[0]
    depth = w_in.shape[0]
    tl = min(TL_SCAN, seq)
    tm = min(TM_DENSE, nb * seq)
    chunk = CHUNK if seq % CHUNK == 0 else seq

    cos_p, sin_p = _rope_tables(jnp.arange(seq, dtype=f32))
    cos_s, sin_s = _rope_tables(jnp.full((1,), PAST_LEN, f32))
    xp = x_prompt.reshape(nb * seq, D_MODEL)
    xs = x_sample.reshape(ns, D_MODEL)
    pp = p_prompt.reshape(depth, nb * seq, PLE_DIM)
    ps = p_sample.reshape(depth, ns, PLE_DIM)
    ssm_state_t = jnp.swapaxes(state_ssm, 3, 4)
    ssm_conv_t = jnp.transpose(state_ssm_conv, (0, 2, 1, 3))
    gdn_conv_t = jnp.transpose(state_gdn_conv, (0, 2, 1, 3))

    prompt_states, sample_convs = [], []
    t_r = t_s = t_g = None
    for i in range(depth):
        w = _layer_weights(i, prm)
        x1 = _ffn_ln(xp, w["wg"][0], w["wu"][0], w["wd"][0], w["ln_g"], w["ln_b"], tm)
        a_r, s_r = _ret_scan(x1, w["w_ret"], cos_p, sin_p, nb, seq, tl, RET_CHUNK if seq % RET_CHUNK == 0 else chunk)
        a_s, s_s, c_s = _ssd_scan(x1, w["w_ssd"], w["ssm_cw"], w["ssm_cb"], w["ssm_dtb"], w["ssm_alog"],
                                  w["ssm_dexp"], w["ssm_nw"], nb, seq, tl, chunk)
        a_g, s_g, c_g = _gdn_scan(x1, w["w_gdn"], w["gdn_cw"], w["gdn_dtb"], w["gdn_alog"], w["gdn_nw"],
                                  nb, seq, tl, chunk)
        xp = _post_mix(x1, (a_r, a_s, a_g), pp, i, w, tm)
        prompt_states.append((s_r, s_s, c_s, s_g, c_g))
        y1 = _ffn_ln(xs, w["wg"][0], w["wu"][0], w["wd"][0], w["ln_g"], w["ln_b"], ns)
        b_r, t_r = _dec_ret(y1, w["w_ret"], cos_s, sin_s, state_ret, i, t_r)
        b_s, t_s, d_s = _dec_ssd(y1, w["w_ssd"], ssm_conv_t, w["ssm_cw"], w["ssm_cb"], w["ssm_dtb"],
                                 w["ssm_alog"], w["ssm_dexp"], w["ssm_nw"], ssm_state_t, i, t_s)
        b_g, t_g, d_g = _dec_gdn(y1, w["w_gdn"], gdn_conv_t, w["gdn_cw"], w["gdn_dtb"], w["gdn_alog"],
                                 w["gdn_nw"], state_gdn, i, t_g)
        xs = _post_mix(y1, (b_r, b_s, b_g), ps, i, w, ns)
        sample_convs.append((d_s, d_g))

    r_p, s_p, sc_p, g_p, gc_p = (jnp.stack([s[j] for s in prompt_states]) for j in range(5))
    sc_s, gc_s = (jnp.transpose(jnp.stack([c[j] for c in sample_convs]), (0, 2, 1, 3)) for j in range(2))
    return (xp.reshape(nb, seq, D_MODEL), xs.reshape(ns, 1, D_MODEL),
            r_p, jnp.swapaxes(s_p, 3, 4), sc_p, g_p, gc_p,
            t_r, jnp.swapaxes(t_s, 3, 4), sc_s, t_g, gc_s)
```

```python
import functools
import math

import numpy as np
import jax
import jax.numpy as jnp
from jax import lax
from jax.experimental import pallas as pl
from jax.experimental.pallas import tpu as pltpu

f32, bf16 = jnp.float32, jnp.bfloat16

D_MODEL = 1024
DEPTH = 2
PAST_LEN = 16384
R_HEADS, R_DK, R_DV = 4, 128, 256
R_QK, R_VAL = R_HEADS * R_DK, R_HEADS * R_DV
ROPE_BASE = 10000.0
M_HEADS, M_HEADDIM, M_GROUPS, M_STATE = 16, 64, 2, 128
M_INNER = M_HEADS * M_HEADDIM
M_CONV_DIM = M_INNER + 2 * M_GROUPS * M_STATE
M_HPG = M_HEADS // M_GROUPS
M_GW = M_HPG * M_HEADDIM
G_HEADS, G_DK, G_DV = 8, 128, 128
G_KEY, G_VAL = G_HEADS * G_DK, G_HEADS * G_DV
G_QKV = 2 * G_KEY + G_VAL
CONV_W = 4
FFN_DIM = 2048
PLE_DIM = 256
DN_ALPHA = (2 * DEPTH) ** 0.25
LN_EPS = 1e-5
NORM_EPS = 1e-6

_sizes = (R_QK, R_QK, R_VAL, R_VAL, M_INNER, M_CONV_DIM, M_HEADS, G_QKV, G_VAL, G_HEADS, G_HEADS,
          D_MODEL, D_MODEL, D_MODEL)
_off = np.concatenate([[0], np.cumsum(_sizes)]).tolist()
OFF_RET, OFF_SSD, OFF_GDN, OFF_MERGE, IN_DIM = _off[0], _off[4], _off[7], _off[11], _off[14]

LANES = 128
SUBLANES = 8
VMEM_LIMIT = 56 * 2 ** 20

TM_DENSE = 512
TL_SCAN = 512
CHUNK = 64
FFN_CHUNK = 512
DEC_BT = 8
RET_CHUNK = 128
RET_UNROLL = 2
SSD_UNROLL = 4
GDN_PRE_UNROLL = 4
GDN_SCAN_UNROLL = 4
PAD = LANES

LOG_GAMMA = [math.log1p(-(2.0 ** (-5.0 - h))) for h in range(R_HEADS)]


def _dot(a, b):
    return jnp.dot(a, b, preferred_element_type=f32)


def _dot_nt(a, b):
    return lax.dot_general(a, b, (((1,), (1,)), ((), ())), preferred_element_type=f32)


def _dot_tn(a, b):
    return lax.dot_general(a, b, (((0,), (0,)), ((), ())), preferred_element_type=f32)


def _b(x):
    return x.astype(bf16)


def _layer_norm(y, g, b):
    mu = jnp.mean(y, -1, keepdims=True)
    var = jnp.mean(jnp.square(y - mu), -1, keepdims=True)
    return (y - mu) * lax.rsqrt(var + LN_EPS) * g + b


def _rms(y):
    return y * lax.rsqrt(jnp.mean(jnp.square(y), -1, keepdims=True) + NORM_EPS)


def _const_spec(shape):
    return pl.BlockSpec(shape, lambda *_: (0,) * len(shape), pipeline_mode=pl.Buffered(1))


def _layer_spec(arr, *lead):
    blk = (None,) * len(lead) + arr.shape[len(lead):]
    tail = (0,) * (arr.ndim - len(lead))
    return pl.BlockSpec(blk, lambda *_: tuple(lead) + tail, pipeline_mode=pl.Buffered(1))


W_IN_BLK = 3072


def _w_in_block(w_all, layer, blk):
    return w_all, pl.BlockSpec((None, D_MODEL, W_IN_BLK), lambda *_: (layer, 0, blk), pipeline_mode=pl.Buffered(1))


def _weight(w):
    return w if isinstance(w, tuple) else (w, _const_spec(w.shape))


def _params(sem):
    return pltpu.CompilerParams(dimension_semantics=sem, vmem_limit_bytes=VMEM_LIMIT)


def _split3(x):
    a1 = _b(x)
    r1 = x - a1.astype(f32)
    a2 = _b(r1)
    a3 = _b(r1 - a2.astype(f32))
    return a1, a2, a3


def _chunk_iotas(c):
    ii = lax.broadcasted_iota(jnp.int32, (c, c), 0)
    jj = lax.broadcasted_iota(jnp.int32, (c, c), 1)
    return ii, jj


def _swiglu(x, wg_ref, wu_ref, wd_ref):
    xb = _b(x)
    acc = None
    for c in range(FFN_DIM // FFN_CHUNK):
        sl = slice(c * FFN_CHUNK, (c + 1) * FFN_CHUNK)
        a = jax.nn.silu(_dot(xb, wg_ref[:, sl])) * _dot(xb, wu_ref[:, sl])
        part = _dot(_b(a), wd_ref[sl, :])
        acc = part if acc is None else acc + part
    return acc


def _ffn_ln_kernel(x_ref, wg_ref, wu_ref, wd_ref, g_ref, b_ref, o_ref):
    x = x_ref[...]
    y = DN_ALPHA * x + 0.5 * _swiglu(x, wg_ref, wu_ref, wd_ref)
    o_ref[...] = _layer_norm(y, g_ref[0:1, :], b_ref[0:1, :])


def _ffn_ln(x, wg, wu, wd, g, b, layer, tm):
    n = x.shape[0]
    return pl.pallas_call(
        _ffn_ln_kernel,
        out_shape=jax.ShapeDtypeStruct((n, D_MODEL), f32),
        grid=(n // tm,),
        in_specs=[pl.BlockSpec((tm, D_MODEL), lambda i: (i, 0)),
                  _layer_spec(wg, layer, 0), _layer_spec(wu, layer, 0), _layer_spec(wd, layer, 0),
                  _const_spec(g.shape), _const_spec(b.shape)],
        out_specs=pl.BlockSpec((tm, D_MODEL), lambda i: (i, 0)),
        compiler_params=_params(("parallel",)),
        name="ffn_ln",
    )(x, wg, wu, wd, g, b)


def _merge_kernel(x_ref, ar_ref, as_ref, ag_ref, wm_ref, wr_ref, ws_ref, wgd_ref, wo_ref, g_ref, b_ref, o_ref):
    x = x_ref[...]
    m = _dot(_b(x), wm_ref[...])
    yr = _dot(_b(ar_ref[...]), wr_ref[...])
    ys = _dot(_b(as_ref[...]), ws_ref[...])
    yg = _dot(_b(ag_ref[...]), wgd_ref[...])
    mixed = (jax.nn.sigmoid(m[:, 0:D_MODEL]) * yr + jax.nn.sigmoid(m[:, D_MODEL:2 * D_MODEL]) * ys
             + jax.nn.sigmoid(m[:, 2 * D_MODEL:3 * D_MODEL]) * yg)
    y = DN_ALPHA * x + _dot(_b(mixed), wo_ref[...])
    o_ref[...] = _layer_norm(y, g_ref[1:2, :], b_ref[1:2, :])


def _merge(x, ar, a_s, ag, wm, wr, ws, wgd, wo, g, b, layer, tm):
    n = x.shape[0]
    tok = lambda i: (i, 0)
    return pl.pallas_call(
        _merge_kernel,
        out_shape=jax.ShapeDtypeStruct((n, D_MODEL), f32),
        grid=(n // tm,),
        in_specs=[pl.BlockSpec((tm, D_MODEL), tok)] * 4
                 + [_const_spec(wm.shape)] + [_layer_spec(w, layer) for w in (wr, ws, wgd, wo)]
                 + [_const_spec(g.shape), _const_spec(b.shape)],
        out_specs=pl.BlockSpec((tm, D_MODEL), tok),
        compiler_params=_params(("parallel",)),
        name="merge",
    )(x, ar, a_s, ag, wm, wr, ws, wgd, wo, g, b)


def _ffn_pe_kernel(x_ref, p_ref, wg_ref, wu_ref, wd_ref, pg_ref, pp_ref, g_ref, b_ref, o_ref):
    x = x_ref[...]
    x = _layer_norm(DN_ALPHA * x + 0.5 * _swiglu(x, wg_ref, wu_ref, wd_ref), g_ref[2:3, :], b_ref[2:3, :])
    pe = jax.nn.sigmoid(_dot(_b(x), pg_ref[...])) * _dot(_b(p_ref[...]), pp_ref[...])
    o_ref[...] = _layer_norm(DN_ALPHA * x + pe, g_ref[3:4, :], b_ref[3:4, :])


def _ffn_pe(x, p, layer, wg, wu, wd, pg, pp, g, b, tm):
    n = x.shape[0]
    return pl.pallas_call(
        _ffn_pe_kernel,
        out_shape=jax.ShapeDtypeStruct((n, D_MODEL), f32),
        grid=(n // tm,),
        in_specs=[pl.BlockSpec((tm, D_MODEL), lambda i: (i, 0)),
                  pl.BlockSpec((None, tm, PLE_DIM), lambda i: (layer, i, 0)),
                  _layer_spec(wg, layer, 1), _layer_spec(wu, layer, 1), _layer_spec(wd, layer, 1),
                  _layer_spec(pg, layer), _layer_spec(pp, layer),
                  _const_spec(g.shape), _const_spec(b.shape)],
        out_specs=pl.BlockSpec((tm, D_MODEL), lambda i: (i, 0)),
        compiler_params=_params(("parallel",)),
        name="ffn_pe",
    )(x, p, wg, wu, wd, pg, pp, g, b)


def _rope_inplace(proj_ref, off, cos, sin, scale):
    t = proj_ref[:, off:off + R_DK]
    t = t * cos + pltpu.roll(t, R_DK // 2, axis=1) * sin
    if scale != 1.0:
        t = t * scale
    proj_ref[:, off:off + R_DK] = t


def _ret_scan_kernel(x_ref, w_ref, cos_ref, sin_ref, act_ref, st_ref, proj_ref, s_ref, *, tl, c):
    l = pl.program_id(1)

    @pl.when(l == 0)
    def _():
        s_ref[...] = jnp.zeros_like(s_ref)

    proj_ref[...] = _dot(_b(x_ref[...]), w_ref[...])
    cos, sin = cos_ref[...], sin_ref[...]
    for h in range(R_HEADS):
        _rope_inplace(proj_ref, h * R_DK, cos, sin, 1.0)
        _rope_inplace(proj_ref, R_QK + h * R_DK, cos, sin, R_DK ** -0.5)

    ii, jj = _chunk_iotas(c)
    dif = (ii - jj).astype(f32)
    ci = lax.broadcasted_iota(jnp.int32, (c, 1), 0).astype(f32)
    decay = [jnp.where(dif >= 0, jnp.exp(dif * lg), 0.0) for lg in LOG_GAMMA]
    e_col = [jnp.exp((ci + 1.0) * lg) for lg in LOG_GAMMA]
    w_col = [jnp.exp((c - 1.0 - ci) * lg) for lg in LOG_GAMMA]

    def chunk(ck):
        rows = pl.ds(pl.multiple_of(ck * c, c), c)
        hd = range(R_HEADS)
        q = [proj_ref[rows, h * R_DK:(h + 1) * R_DK] for h in hd]
        k = [proj_ref[rows, R_QK + h * R_DK:R_QK + (h + 1) * R_DK] for h in hd]
        v = [_b(proj_ref[rows, 2 * R_QK + h * R_DV:2 * R_QK + (h + 1) * R_DV]) for h in hd]
        s = [s_ref[h] for h in hd]
        scores = [_dot_nt(_b(q[h]), _b(k[h])) * decay[h] for h in hd]
        inter = [_dot(_b(q[h] * e_col[h]), _b(s[h])) for h in hd]
        for h in hd:
            s_ref[h] = s[h] * math.exp(c * LOG_GAMMA[h]) + _dot_tn(_b(k[h] * w_col[h]), v[h])
        o = [_dot(_b(scores[h]), v[h]) + inter[h] for h in hd]
        for h in hd:
            mu = jnp.mean(o[h], -1, keepdims=True)
            var = jnp.mean(jnp.square(o[h] - mu), -1, keepdims=True)
            on = (o[h] - mu) * lax.rsqrt(var + LN_EPS)
            g = proj_ref[rows, 2 * R_QK + R_VAL + h * R_DV:2 * R_QK + R_VAL + (h + 1) * R_DV]
            act_ref[rows, h * R_DV:(h + 1) * R_DV] = _b(on * jax.nn.silu(g))

    def chunks(it, carry):
        for j in range(RET_UNROLL):
            chunk(it * RET_UNROLL + j)
        return carry

    lax.fori_loop(0, tl // (c * RET_UNROLL), chunks, 0)

    @pl.when(l == pl.num_programs(1) - 1)
    def _():
        st_ref[0] = s_ref[...]


def _ret_scan(x, w, cos, sin, nb, nl_tok, tl, c):
    w, wspec = _weight(w)
    nl = nl_tok // tl
    tok = lambda b, l: (b * nl + l, 0)
    return pl.pallas_call(
        functools.partial(_ret_scan_kernel, tl=tl, c=c),
        out_shape=(jax.ShapeDtypeStruct((nb * nl_tok, R_VAL), bf16),
                   jax.ShapeDtypeStruct((nb, R_HEADS, R_DK, R_DV), f32)),
        grid=(nb, nl),
        in_specs=[pl.BlockSpec((tl, D_MODEL), tok), wspec,
                  pl.BlockSpec((tl, R_DK), lambda b, l: (l, 0)),
                  pl.BlockSpec((tl, R_DK), lambda b, l: (l, 0))],
        out_specs=(pl.BlockSpec((tl, R_VAL), tok),
                   pl.BlockSpec((1, R_HEADS, R_DK, R_DV), lambda b, l: (b, 0, 0, 0))),
        scratch_shapes=[pltpu.VMEM((tl, 2 * R_QK + 2 * R_VAL), f32),
                        pltpu.VMEM((R_HEADS, R_DK, R_DV), f32)],
        compiler_params=_params(("parallel", "arbitrary")),
        name="ret_scan",
    )(x, w, cos, sin)


CONV_BLK = 512


def _project_and_conv(xb, w_ref, w_off, width, xbuf_ref, cw_ref, cb_ref, dst_ref, tl, first):
    @pl.when(first)
    def _():
        xbuf_ref[0:SUBLANES, :] = jnp.zeros((SUBLANES, width), f32)

    def project(n):
        cs = slice(n * CONV_BLK, (n + 1) * CONV_BLK)
        xbuf_ref[SUBLANES:SUBLANES + tl, cs] = _dot(xb, w_ref[:, w_off + cs.start:w_off + cs.stop])

    def conv(n):
        cs = slice(n * CONV_BLK, (n + 1) * CONV_BLK)
        acc = xbuf_ref[SUBLANES:SUBLANES + tl, cs] * cw_ref[CONV_W - 1:CONV_W, cs]
        for j in range(CONV_W - 1):
            r0 = SUBLANES - (CONV_W - 1) + j
            acc = acc + xbuf_ref[r0:r0 + tl, cs] * cw_ref[j:j + 1, cs]
        if cb_ref is not None:
            acc = acc + cb_ref[:, cs]
        dst_ref[:, cs] = jax.nn.silu(acc)
        xbuf_ref[0:SUBLANES, cs] = xbuf_ref[tl:tl + SUBLANES, cs]

    nblk = width // CONV_BLK
    project(0)
    for n in range(nblk):
        if n + 1 < nblk:
            project(n + 1)
        conv(n)


def _head_expander(e2_ref, width):
    er = lax.broadcasted_iota(jnp.int32, e2_ref.shape, 0)
    el = lax.broadcasted_iota(jnp.int32, e2_ref.shape, 1)
    e2_ref[...] = _b(((er & (PAD - 1)) == (el >> (width.bit_length() - 1))).astype(f32))


def _chunk_block_mask(tri_ref, tl, c):
    log2c = c.bit_length() - 1
    ti = lax.broadcasted_iota(jnp.int32, (tl, tl), 0)
    tj = lax.broadcasted_iota(jnp.int32, (tl, tl), 1)
    tri_ref[...] = _b(((ti >= tj) & ((ti >> log2c) == (tj >> log2c))).astype(f32))


def _chunk_cumsum(tri_ref, cum_ref, la, c):
    a1, a2, a3 = _split3(la)
    cum = _dot(tri_ref[...], a1) + _dot(tri_ref[...], a2) + _dot(tri_ref[...], a3)
    cum_ref[...] = cum
    tl = la.shape[0]
    tot = jnp.concatenate([jnp.broadcast_to(cum_ref[k * c + c - 1:k * c + c, :], (c, la.shape[1]))
                           for k in range(tl // c)], axis=0)
    return cum, tot


def _expand_heads(v, e2_ref):
    hi = _b(v)
    lo = _b(v - hi.astype(f32))
    return _dot(jnp.concatenate([hi, lo], axis=1), e2_ref[...])


def _expand_heads_exact(v, e3_ref):
    return _dot(jnp.concatenate(_split3(v), axis=1), e3_ref[...])


def _ssd_scan_kernel(x_ref, w_ref, cw_ref, cb_ref, dtb_ref, alog_ref, dexp_ref, nw_ref,
                     act_ref, st_ref, conv_ref,
                     z_ref, xbuf_ref, xc_ref, cum_ref, cumx_ref, xdt_ref, xw_ref, ee_ref,
                     tri_ref, e3_ref, s_ref, *, tl, c):
    l = pl.program_id(1)

    @pl.when(l == 0)
    def _():
        s_ref[...] = jnp.zeros_like(s_ref)
        _chunk_block_mask(tri_ref, tl, c)
        _head_expander(e3_ref, M_HEADDIM)

    xb = _b(x_ref[...])
    dt = jax.nn.softplus(_dot(xb, w_ref[:, M_INNER + M_CONV_DIM:M_INNER + M_CONV_DIM + PAD]) + dtb_ref[...])
    _project_and_conv(xb, w_ref, M_INNER, M_CONV_DIM, xbuf_ref, cw_ref, cb_ref, xc_ref, tl, l == 0)
    z_ref[...] = _dot(xb, w_ref[:, 0:M_INNER])
    la = -jnp.exp(alog_ref[...]) * dt
    cum, tot = _chunk_cumsum(tri_ref, cum_ref, la, c)
    cumx_ref[...] = _expand_heads_exact(cum, e3_ref)
    e2_ref = e3_ref.at[0:2 * PAD, :]
    xdt = xc_ref[:, 0:M_INNER] * _expand_heads(dt, e2_ref)
    xdt_ref[...] = xdt
    xw_ref[...] = xdt * _expand_heads(jnp.exp(tot - cum), e2_ref)
    ee_ref[...] = _expand_heads(jnp.exp(cum), e2_ref)

    half = c
    lane = lax.broadcasted_iota(jnp.int32, (c, 2 * half), 1)
    rowi = lax.broadcasted_iota(jnp.int32, (c, 2 * half), 0)
    left = lane < half
    causal2 = (lane & (half - 1)) <= rowi
    b_off, c_off = M_INNER, M_INNER + M_GROUPS * M_STATE
    gr = range(M_GROUPS)
    pairs = [(g, pp) for g in gr for pp in range(M_HPG // 2)]

    def chunk(ck):
        r0 = pl.multiple_of(ck * c, c)
        rows = pl.ds(r0, c)
        cum_c = cum_ref[rows, :]
        cum_t = jnp.concatenate([cum_c, cum_c], axis=0).T
        bb = [_b(xc_ref[rows, b_off + g * M_STATE:b_off + (g + 1) * M_STATE]) for g in gr]
        cb = [_b(xc_ref[rows, c_off + g * M_STATE:c_off + (g + 1) * M_STATE]) for g in gr]
        g2 = [_dot_nt(cb[g], jnp.concatenate([bb[g], bb[g]], axis=0)) for g in gr]
        sg = [s_ref[g] for g in gr]
        inter = [ee_ref[rows, g * M_GW:(g + 1) * M_GW] * _dot(cb[g], _b(sg[g])) for g in gr]
        for g in gr:
            e_last = ee_ref[pl.ds(r0 + c - 1, 1), g * M_GW:(g + 1) * M_GW]
            s_ref[g] = sg[g] * e_last + _dot_tn(bb[g], _b(xw_ref[rows, g * M_GW:(g + 1) * M_GW]))
        a2s, rhs = [], []
        for g, pp in pairs:
            h0 = g * M_HPG + 2 * pp
            ls = slice(h0 * M_HEADDIM, (h0 + 2) * M_HEADDIM)
            colsel = cumx_ref[rows, ls]
            rowsel = jnp.where(left[0:1, :], cum_t[h0:h0 + 1, :], cum_t[h0 + 1:h0 + 2, :])
            d2 = jnp.where(causal2, jnp.exp(colsel - rowsel), 0.0)
            a2s.append(_b(g2[g] * d2))
            xp = xdt_ref[rows, ls]
            rhs.append(_b(jnp.concatenate([jnp.where(left, xp, 0.0), jnp.where(left, 0.0, xp)], axis=0)))
        intra = [_dot(a, r) for a, r in zip(a2s, rhs)]
        ys = []
        for n, (g, pp) in enumerate(pairs):
            h0 = g * M_HPG + 2 * pp
            ls = slice(h0 * M_HEADDIM, (h0 + 2) * M_HEADDIM)
            y = intra[n] + inter[g][:, pp * 2 * M_HEADDIM:(pp + 1) * 2 * M_HEADDIM] + dexp_ref[:, ls] * xc_ref[rows, ls]
            ys.append(y * jax.nn.silu(z_ref[rows, ls]))
        for g in gr:
            mine = [n for n, (gg, _) in enumerate(pairs) if gg == g]
            ms = sum(jnp.sum(jnp.square(ys[n]), -1, keepdims=True) for n in mine) * (1.0 / M_GW)
            r = lax.rsqrt(ms + NORM_EPS)
            for n in mine:
                h0 = g * M_HPG + 2 * pairs[n][1]
                ls = slice(h0 * M_HEADDIM, (h0 + 2) * M_HEADDIM)
                act_ref[rows, ls] = _b(ys[n] * r * nw_ref[:, ls])

    def chunks(it, carry):
        for j in range(SSD_UNROLL):
            chunk(it * SSD_UNROLL + j)
        return carry

    lax.fori_loop(0, tl // (c * SSD_UNROLL), chunks, 0)

    @pl.when(l == pl.num_programs(1) - 1)
    def _():
        for h in range(M_HEADS):
            g, hh = divmod(h, M_HPG)
            st_ref[0, h] = s_ref[g][:, hh * M_HEADDIM:(hh + 1) * M_HEADDIM].T
        conv_ref[0] = xbuf_ref[SUBLANES - (CONV_W - 1):SUBLANES, :]


def _ssd_scan(x, w, cw, cb, dtb, alog, dexp, nw, nb, nl_tok, tl, c):
    assert 2 * c == LANES and 2 * M_HEADDIM == LANES, "head pairs are packed into one 128-lane slab"
    w, wspec = _weight(w)
    nl = nl_tok // tl
    tok = lambda b, l: (b * nl + l, 0)
    return pl.pallas_call(
        functools.partial(_ssd_scan_kernel, tl=tl, c=c),
        out_shape=(jax.ShapeDtypeStruct((nb * nl_tok, M_INNER), bf16),
                   jax.ShapeDtypeStruct((nb, M_HEADS, M_HEADDIM, M_STATE), f32),
                   jax.ShapeDtypeStruct((nb, CONV_W - 1, M_CONV_DIM), f32)),
        grid=(nb, nl),
        in_specs=[pl.BlockSpec((tl, D_MODEL), tok)]
                 + [wspec] + [_const_spec(a.shape) for a in (cw, cb, dtb, alog, dexp, nw)],
        out_specs=(pl.BlockSpec((tl, M_INNER), tok),
                   pl.BlockSpec((1, M_HEADS, M_HEADDIM, M_STATE), lambda b, l: (b, 0, 0, 0)),
                   pl.BlockSpec((1, CONV_W - 1, M_CONV_DIM), lambda b, l: (b, 0, 0))),
        scratch_shapes=[pltpu.VMEM((tl, M_INNER), f32),
                        pltpu.VMEM((tl + SUBLANES, M_CONV_DIM), f32),
                        pltpu.VMEM((tl, M_CONV_DIM), f32),
                        pltpu.VMEM((tl, PAD), f32),
                        pltpu.VMEM((tl, M_INNER), f32),
                        pltpu.VMEM((tl, M_INNER), f32),
                        pltpu.VMEM((tl, M_INNER), f32),
                        pltpu.VMEM((tl, M_INNER), f32),
                        pltpu.VMEM((tl, tl), bf16),
                        pltpu.VMEM((3 * PAD, M_INNER), bf16),
                        pltpu.VMEM((M_GROUPS, M_STATE, M_GW), f32)],
        compiler_params=_params(("parallel", "arbitrary")),
        name="ssd_scan",
    )(x, w, cw, cb, dtb, alog, dexp, nw)


def _pair_blockdiag(x, left):
    return jnp.concatenate([jnp.where(left, x, jnp.zeros_like(x)), jnp.where(left, jnp.zeros_like(x), x)], axis=0)


def _gdn_scan_kernel(x_ref, w_ref, cw_ref, dtb_ref, alog_ref, nw_ref,
                     act_ref, st_ref, conv_ref,
                     gz_ref, xbuf_ref, qkv_ref, cum_ref, ee_ref,
                     q16_ref, k16_ref, kb16_ref, qe16_ref, kbe16_ref, kw16_ref, vb16_ref,
                     wy_ref, u0_ref, attn_ref, tri_ref, e2_ref, s_ref, *, tl, c):
    l = pl.program_id(1)
    n_pairs = G_HEADS // 2
    pw = 2 * G_DK

    @pl.when(l == 0)
    def _():
        s_ref[...] = jnp.zeros_like(s_ref)
        _chunk_block_mask(tri_ref, tl, c)
        _head_expander(e2_ref, G_DK)

    xb = _b(x_ref[...])
    ab = _dot(xb, w_ref[:, G_QKV + G_VAL:])
    _project_and_conv(xb, w_ref, 0, G_QKV, xbuf_ref, cw_ref, None, qkv_ref, tl, l == 0)
    gz_ref[...] = _dot(xb, w_ref[:, G_QKV:G_QKV + G_VAL])
    g = -jnp.exp(alog_ref[...]) * jax.nn.softplus(ab + dtb_ref[...])
    cum, tot = _chunk_cumsum(tri_ref, cum_ref, g, c)
    beta = pltpu.roll(jax.nn.sigmoid(ab), PAD - G_HEADS, axis=1)
    e_c, w_c = jnp.exp(cum), jnp.exp(tot - cum)
    for hb in range(n_pairs):
        ls = slice(hb * pw, (hb + 1) * pw)
        e_x = _expand_heads(e_c, e2_ref.at[:, ls])
        w_x = _expand_heads(w_c, e2_ref.at[:, ls])
        b_x = _expand_heads(beta, e2_ref.at[:, ls])
        ee_ref[:, ls] = e_x
        qn, kn = [], []
        for t in range(2):
            hs = slice((2 * hb + t) * G_DK, (2 * hb + t + 1) * G_DK)
            qt, kt = qkv_ref[:, hs], qkv_ref[:, G_KEY + hs.start:G_KEY + hs.stop]
            qn.append(qt * lax.rsqrt(jnp.sum(jnp.square(qt), -1, keepdims=True) + NORM_EPS) * (G_DK ** -0.5))
            kn.append(kt * lax.rsqrt(jnp.sum(jnp.square(kt), -1, keepdims=True) + NORM_EPS))
        q, k = jnp.concatenate(qn, axis=1), jnp.concatenate(kn, axis=1)
        kb = k * b_x
        q16_ref[:, ls] = _b(q)
        k16_ref[:, ls] = _b(k)
        kb16_ref[:, ls] = _b(kb)
        qe16_ref[:, ls] = _b(q * e_x)
        kbe16_ref[:, ls] = _b(kb * e_x)
        kw16_ref[:, ls] = _b(k * w_x)
        vb16_ref[:, ls] = _b(qkv_ref[:, 2 * G_KEY + hb * pw:2 * G_KEY + (hb + 1) * pw] * b_x)

    lane = lax.broadcasted_iota(jnp.int32, (c, 2 * c), 1)
    rowi = lax.broadcasted_iota(jnp.int32, (c, 2 * c), 0)
    left = lane < c
    jloc = lane & (c - 1)
    causal2, strict2 = jloc <= rowi, jloc < rowi
    eye2 = (jloc == rowi).astype(f32)
    left_w = lax.broadcasted_iota(jnp.int32, (c, pw), 1) < G_DK
    left_s = lax.broadcasted_iota(jnp.int32, (G_DK, pw), 1) < G_DK
    pr = range(n_pairs)

    def precompute(it, carry):
        cks = [it * GDN_PRE_UNROLL + j for j in range(GDN_PRE_UNROLL)]
        rows = [pl.ds(pl.multiple_of(ck * c, c), c) for ck in cks]
        lsl = [slice(p * pw, (p + 1) * pw) for p in pr]
        cp = [(j, p) for j in range(GDN_PRE_UNROLL) for p in pr]
        cum_c = [cum_ref[r, :] for r in rows]
        cum_t = [jnp.concatenate([x, x], axis=0).T for x in cum_c]
        d2 = []
        for j, p in cp:
            colsel = jnp.where(left, cum_c[j][:, 2 * p:2 * p + 1], cum_c[j][:, 2 * p + 1:2 * p + 2])
            rowsel = jnp.where(left[0:1, :], cum_t[j][2 * p:2 * p + 1, :], cum_t[j][2 * p + 1:2 * p + 2, :])
            d2.append(jnp.where(causal2, jnp.exp(colsel - rowsel), 0.0))
        kbd = [_pair_blockdiag(k16_ref[rows[j], lsl[p]], left_w) for j, p in cp]
        kq = [_dot_nt(jnp.concatenate([kb16_ref[rows[j], lsl[p]], q16_ref[rows[j], lsl[p]]], axis=0), kbd[n])
              for n, (j, p) in enumerate(cp)]
        lm = [jnp.where(strict2, kq[n][0:c, :] * d2[n], 0.0) for n in range(len(cp))]
        attn = [kq[n][c:2 * c, :] * d2[n] for n in range(len(cp))]
        for n, (j, p) in enumerate(cp):
            attn_ref[rows[j], p * 2 * c:(p + 1) * 2 * c] = _b(attn[n])
        ps = [eye2 - x for x in lm]
        ms = lm
        kpow = 2
        while kpow < c:
            ms = [_dot(_b(m), _b(_pair_blockdiag(m, left))) for m in ms]
            ps = [x + _dot(_b(x), _b(_pair_blockdiag(m, left))) for x, m in zip(ps, ms)]
            kpow *= 2
        rhs = [jnp.concatenate([_pair_blockdiag(kbe16_ref[rows[j], lsl[p]], left_w),
                                _pair_blockdiag(vb16_ref[rows[j], lsl[p]], left_w)], axis=1) for j, p in cp]
        wu = [_dot(_b(ps[n]), rhs[n]) for n in range(len(cp))]
        for n, (j, p) in enumerate(cp):
            wy_ref[rows[j], lsl[p]] = _b(wu[n][:, 0:pw])
            u0_ref[rows[j], lsl[p]] = wu[n][:, pw:2 * pw]
        return carry

    lax.fori_loop(0, tl // (c * GDN_PRE_UNROLL), precompute, 0)

    def scan_chunk(ck):
        r0 = pl.multiple_of(ck * c, c)
        rows = pl.ds(r0, c)
        lsl = [slice(p * pw, (p + 1) * pw) for p in pr]
        sp = [s_ref[p] for p in pr]
        sbd = [_b(_pair_blockdiag(sp[p], left_s)) for p in pr]
        r = [_dot(jnp.concatenate([wy_ref[rows, lsl[p]], qe16_ref[rows, lsl[p]]], axis=0), sbd[p]) for p in pr]
        u = [u0_ref[rows, lsl[p]] - r[p][0:c, :] for p in pr]
        ubd = [_b(_pair_blockdiag(u[p], left_w)) for p in pr]
        for p in pr:
            kw = kw16_ref[rows, lsl[p]]
            kw_stack = jnp.concatenate([kw[:, 0:G_DK], kw[:, G_DK:pw]], axis=0)
            e_last = ee_ref[pl.ds(r0 + c - 1, 1), lsl[p]]
            s_ref[p] = sp[p] * e_last + _dot_tn(kw_stack, ubd[p])
        o = [r[p][c:2 * c, :] + _dot(attn_ref[rows, p * 2 * c:(p + 1) * 2 * c], ubd[p]) for p in pr]
        for p in pr:
            for t in range(2):
                hs = slice((2 * p + t) * G_DV, (2 * p + t + 1) * G_DV)
                act_ref[rows, hs] = _b(_rms(o[p][:, t * G_DV:(t + 1) * G_DV]) * nw_ref[...]
                                       * jax.nn.silu(gz_ref[rows, hs]))

    def scan(it, carry):
        for j in range(GDN_SCAN_UNROLL):
            scan_chunk(it * GDN_SCAN_UNROLL + j)
        return carry

    lax.fori_loop(0, tl // (c * GDN_SCAN_UNROLL), scan, 0)

    @pl.when(l == pl.num_programs(1) - 1)
    def _():
        for h in range(G_HEADS):
            st_ref[0, h] = s_ref[h // 2][:, (h % 2) * G_DV:(h % 2 + 1) * G_DV]
        conv_ref[0] = xbuf_ref[SUBLANES - (CONV_W - 1):SUBLANES, :]


def _gdn_scan(x, w, cw, dtb, alog, nw, nb, nl_tok, tl, c):
    assert 2 * c == LANES and G_DK == G_DV == LANES, "two heads' (c, c) blocks share one 128-lane slab"
    nl = nl_tok // tl
    tok = lambda b, l: (b * nl + l, 0)
    return pl.pallas_call(
        functools.partial(_gdn_scan_kernel, tl=tl, c=c),
        out_shape=(jax.ShapeDtypeStruct((nb * nl_tok, G_VAL), bf16),
                   jax.ShapeDtypeStruct((nb, G_HEADS, G_DK, G_DV), f32),
                   jax.ShapeDtypeStruct((nb, CONV_W - 1, G_QKV), f32)),
        grid=(nb, nl),
        in_specs=[pl.BlockSpec((tl, D_MODEL), tok)]
                 + [_const_spec(a.shape) for a in (w, cw, dtb, alog, nw)],
        out_specs=(pl.BlockSpec((tl, G_VAL), tok),
                   pl.BlockSpec((1, G_HEADS, G_DK, G_DV), lambda b, l: (b, 0, 0, 0)),
                   pl.BlockSpec((1, CONV_W - 1, G_QKV), lambda b, l: (b, 0, 0))),
        scratch_shapes=[pltpu.VMEM((tl, G_VAL), f32),
                        pltpu.VMEM((tl + SUBLANES, G_QKV), f32),
                        pltpu.VMEM((tl, G_QKV), f32),
                        pltpu.VMEM((tl, PAD), f32),
                        pltpu.VMEM((tl, G_KEY), f32),
                        ] + [pltpu.VMEM((tl, G_KEY), bf16)] * 7 + [
                        pltpu.VMEM((tl, G_KEY), bf16),
                        pltpu.VMEM((tl, G_VAL), f32),
                        pltpu.VMEM((tl, G_HEADS * c), bf16),
                        pltpu.VMEM((tl, tl), bf16),
                        pltpu.VMEM((2 * PAD, G_KEY), bf16),
                        pltpu.VMEM((G_HEADS // 2, G_DK, 2 * G_DV), f32)],
        compiler_params=_params(("parallel", "arbitrary")),
        name="gdn_scan",
    )(x, w, cw, dtb, alog, nw)


def _token_lanes_to_front(src_ref, dst_ref, i):
    n = src_ref.shape[1]
    dst_ref[...] = pltpu.roll(src_ref[...], (n - i * DEC_BT) % n, axis=1)


def _conv_step(cst_ref, x_new, cw_ref, cb_ref, conv_out_ref):
    acc = x_new * cw_ref[CONV_W - 1:CONV_W, :]
    for j in range(CONV_W - 1):
        acc = acc + cst_ref[j] * cw_ref[j:j + 1, :]
    if cb_ref is not None:
        acc = acc + cb_ref[...]
    for j in range(CONV_W - 2):
        conv_out_ref[j] = cst_ref[j + 1]
    conv_out_ref[CONV_W - 2] = x_new
    return jax.nn.silu(acc)


def _state_specs(state, prev, layer, blk):
    zeros = (0,) * (len(blk) - 2)
    spec = pl.BlockSpec(blk, lambda i: (layer, i) + zeros)
    if prev is None:
        prev = jnp.zeros((SUBLANES, LANES), f32)
        alias = {}
    else:
        alias = None
    return spec, prev, alias


def _dec_ret_kernel(x_ref, w_ref, cos_ref, sin_ref, s_in, prev_ref, act_ref, s_out,
                    q_ref, kt_ref, vg_ref, ks_ref, o_ref):
    del prev_ref
    i = pl.program_id(0)

    @pl.when(i == 0)
    def _():
        proj = _dot(_b(x_ref[...]), w_ref[...])
        cos, sin = cos_ref[...], sin_ref[...]
        for h in range(R_HEADS):
            hs = slice(h * R_DK, (h + 1) * R_DK)
            t = proj[:, hs]
            q_ref[:, hs] = t * cos + pltpu.roll(t, R_DK // 2, axis=1) * sin
            t = proj[:, R_QK + h * R_DK:R_QK + (h + 1) * R_DK]
            q_ref[:, R_QK + h * R_DK:R_QK + (h + 1) * R_DK] = t = (
                t * cos + pltpu.roll(t, R_DK // 2, axis=1) * sin) * (R_DK ** -0.5)
            kt_ref[hs, :] = t.T
        vg_ref[...] = proj[:, 2 * R_QK:2 * R_QK + 2 * R_VAL]

    _token_lanes_to_front(kt_ref, ks_ref, i)
    rows = pl.ds(pl.multiple_of(i * DEC_BT, DEC_BT), DEC_BT)
    vg, qk8 = vg_ref[rows, :], q_ref[rows, :]
    for h in range(R_HEADS):
        q8 = qk8[:, h * R_DK:(h + 1) * R_DK]
        k8 = qk8[:, R_QK + h * R_DK:R_QK + (h + 1) * R_DK]
        qk = jnp.sum(q8 * k8, axis=1, keepdims=True)
        lhs = _b(jnp.concatenate([q8, q8], axis=0))
        gam = math.exp(LOG_GAMMA[h])
        for j in range(DEC_BT):
            kc = ks_ref[h * R_DK:(h + 1) * R_DK, j:j + 1]
            v_row = vg[j:j + 1, h * R_DV:(h + 1) * R_DV]
            s = s_in[j, h]
            qs = _dot(lhs, _b(s))[j:j + 1, :]
            o_ref[j:j + 1, h * R_DV:(h + 1) * R_DV] = gam * qs + qk[j:j + 1, :] * v_row
            s_out[j, h] = s * gam + kc * v_row
    for h in range(R_HEADS):
        o = o_ref[:, h * R_DV:(h + 1) * R_DV]
        mu = jnp.mean(o, -1, keepdims=True)
        var = jnp.mean(jnp.square(o - mu), -1, keepdims=True)
        g = vg[:, R_VAL + h * R_DV:R_VAL + (h + 1) * R_DV]
        act_ref[:, h * R_DV:(h + 1) * R_DV] = (o - mu) * lax.rsqrt(var + LN_EPS) * jax.nn.silu(g)


def _dec_ret(x, w, cos, sin, state, layer, prev):
    n = x.shape[0]
    w, wspec = _weight(w)
    consts = (x, w, cos, sin)
    sspec, prev, alias = _state_specs(state, prev, layer, (None, DEC_BT, R_HEADS, R_DK, R_DV))
    return pl.pallas_call(
        _dec_ret_kernel,
        out_shape=(jax.ShapeDtypeStruct((n, R_VAL), f32), jax.ShapeDtypeStruct(state.shape, f32)),
        grid=(n // DEC_BT,),
        in_specs=[_const_spec(x.shape), wspec, _const_spec(cos.shape), _const_spec(sin.shape),
                  sspec, pl.BlockSpec(memory_space=pl.ANY)],
        out_specs=(pl.BlockSpec((DEC_BT, R_VAL), lambda i: (i, 0)), sspec),
        scratch_shapes=[pltpu.VMEM((n, 2 * R_QK), f32),
                        pltpu.VMEM((R_QK, n), f32),
                        pltpu.VMEM((n, 2 * R_VAL), f32),
                        pltpu.VMEM((R_QK, n), f32),
                        pltpu.VMEM((DEC_BT, R_VAL), f32)],
        input_output_aliases={len(consts) + 1: 1} if alias is None else alias,
        compiler_params=_params(("arbitrary",)),
        name="dec_ret",
    )(*consts, state, prev)


def _dec_ssd_kernel(x_ref, w_ref, cst_ref, cw_ref, cb_ref, dtb_ref, alog_ref, dexp_ref, nw_ref, s_in, prev_ref,
                    act_ref, s_out, conv_out_ref,
                    xc_ref, z_ref, xdt_ref, xdtt_ref, ela_ref, elax_ref, xts_ref, e2_ref, o_ref):
    del prev_ref
    i = pl.program_id(0)

    @pl.when(i == 0)
    def _():
        _head_expander(e2_ref, M_HEADDIM)
        proj = _dot(_b(x_ref[...]), w_ref[...])
        xc = _conv_step(cst_ref, proj[:, M_INNER:M_INNER + M_CONV_DIM], cw_ref, cb_ref, conv_out_ref)
        xc_ref[...] = xc
        z_ref[...] = proj[:, 0:M_INNER]
        dt = jax.nn.softplus(proj[:, M_INNER + M_CONV_DIM:M_INNER + M_CONV_DIM + PAD] + dtb_ref[...])
        ela = jnp.exp(-jnp.exp(alog_ref[...]) * dt)
        ela_ref[...] = ela
        elax_ref[...] = _expand_heads(ela, e2_ref)
        xdt = xc[:, 0:M_INNER] * _expand_heads(dt, e2_ref)
        xdt_ref[...] = xdt
        xdtt_ref[...] = xdt.T

    _token_lanes_to_front(xdtt_ref, xts_ref, i)
    rows = pl.ds(pl.multiple_of(i * DEC_BT, DEC_BT), DEC_BT)
    xc8, xdt8, ela8, elax8 = xc_ref[rows, :], xdt_ref[rows, :], ela_ref[rows, :], elax_ref[rows, :]
    b_off, c_off = M_INNER, M_INNER + M_GROUPS * M_STATE
    for g in range(M_GROUPS):
        gs = slice(g * M_GW, (g + 1) * M_GW)
        b8 = xc8[:, b_off + g * M_STATE:b_off + (g + 1) * M_STATE]
        c8 = xc8[:, c_off + g * M_STATE:c_off + (g + 1) * M_STATE]
        cb = jnp.sum(c8 * b8, axis=1, keepdims=True)
        lhs = _b(jnp.concatenate([c8, c8], axis=0))
        for j in range(DEC_BT):
            st = s_in[j, g * M_HPG:(g + 1) * M_HPG]
            cs = _dot_nt(lhs, _b(st.reshape(M_GW, M_STATE)))[j:j + 1, :]
            o_ref[j:j + 1, gs] = cb[j:j + 1, :] * xdt8[j:j + 1, gs] + elax8[j:j + 1, gs] * cs
            b_row = b8[j:j + 1, :]
            for hh in range(M_HPG):
                h = g * M_HPG + hh
                xdt_col = xts_ref[h * M_HEADDIM:(h + 1) * M_HEADDIM, j:j + 1]
                s_out[j, h] = st[hh] * ela8[j:j + 1, h:h + 1] + xdt_col * b_row
    y = (o_ref[...] + dexp_ref[...] * xc8[:, 0:M_INNER]) * jax.nn.silu(z_ref[rows, :])
    for g in range(M_GROUPS):
        gs = slice(g * M_GW, (g + 1) * M_GW)
        act_ref[:, gs] = _rms(y[:, gs]) * nw_ref[:, gs]


def _dec_ssd(x, w, cst, cw, cb, dtb, alog, dexp, nw, state, layer, prev):
    n = x.shape[0]
    w, wspec = _weight(w)
    consts = (x, w, cst, cw, cb, dtb, alog, dexp, nw)
    sspec, prev, alias = _state_specs(state, prev, layer, (None, DEC_BT, M_HEADS, M_HEADDIM, M_STATE))
    cspecs = [_const_spec(a.shape) for a in consts]
    cspecs[1] = wspec
    cspecs[2] = pl.BlockSpec((None,) + cst.shape[1:], lambda i: (layer, 0, 0, 0), pipeline_mode=pl.Buffered(1))
    return pl.pallas_call(
        _dec_ssd_kernel,
        out_shape=(jax.ShapeDtypeStruct((n, M_INNER), f32), jax.ShapeDtypeStruct(state.shape, f32),
                   jax.ShapeDtypeStruct(cst.shape[1:], f32)),
        grid=(n // DEC_BT,),
        in_specs=cspecs + [sspec, pl.BlockSpec(memory_space=pl.ANY)],
        out_specs=(pl.BlockSpec((DEC_BT, M_INNER), lambda i: (i, 0)), sspec,
                   pl.BlockSpec(cst.shape[1:], lambda i: (0, 0, 0))),
        scratch_shapes=[pltpu.VMEM((n, M_CONV_DIM), f32),
                        pltpu.VMEM((n, M_INNER), f32),
                        pltpu.VMEM((n, M_INNER), f32),
                        pltpu.VMEM((M_INNER, n), f32),
                        pltpu.VMEM((n, PAD), f32),
                        pltpu.VMEM((n, M_INNER), f32),
                        pltpu.VMEM((M_INNER, n), f32),
                        pltpu.VMEM((2 * PAD, M_INNER), bf16),
                        pltpu.VMEM((DEC_BT, M_INNER), f32)],
        input_output_aliases={len(consts) + 1: 1} if alias is None else alias,
        compiler_params=_params(("arbitrary",)),
        name="dec_ssd",
    )(*consts, state, prev)


def _dec_gdn_kernel(x_ref, w_ref, cst_ref, cw_ref, dtb_ref, alog_ref, nw_ref, s_in, prev_ref,
                    act_ref, s_out, conv_out_ref,
                    qkv_ref, kt_ref, gz_ref, eg_ref, beta_ref, ks_ref, o_ref):
    del prev_ref
    i = pl.program_id(0)

    @pl.when(i == 0)
    def _():
        proj = _dot(_b(x_ref[...]), w_ref[...])
        qkv = _conv_step(cst_ref, proj[:, 0:G_QKV], cw_ref, None, conv_out_ref)
        for h in range(G_HEADS):
            hs = slice(h * G_DK, (h + 1) * G_DK)
            q = qkv[:, hs]
            qkv_ref[:, hs] = q * lax.rsqrt(jnp.sum(jnp.square(q), -1, keepdims=True) + NORM_EPS) * (G_DK ** -0.5)
            k = qkv[:, G_KEY + h * G_DK:G_KEY + (h + 1) * G_DK]
            k = k * lax.rsqrt(jnp.sum(jnp.square(k), -1, keepdims=True) + NORM_EPS)
            qkv_ref[:, G_KEY + h * G_DK:G_KEY + (h + 1) * G_DK] = k
            kt_ref[hs, :] = k.T
        qkv_ref[:, 2 * G_KEY:] = qkv[:, 2 * G_KEY:]
        gz_ref[...] = proj[:, G_QKV:G_QKV + G_VAL]
        ab = proj[:, G_QKV + G_VAL:G_QKV + G_VAL + PAD]
        eg_ref[...] = jnp.exp(-jnp.exp(alog_ref[...]) * jax.nn.softplus(ab + dtb_ref[...]))
        beta_ref[...] = jax.nn.sigmoid(ab)

    _token_lanes_to_front(kt_ref, ks_ref, i)
    rows = pl.ds(pl.multiple_of(i * DEC_BT, DEC_BT), DEC_BT)
    qkv8, eg8, beta8 = qkv_ref[rows, :], eg_ref[rows, :], beta_ref[rows, :]
    for h in range(G_HEADS):
        hs = slice(h * G_DV, (h + 1) * G_DV)
        q8 = qkv8[:, h * G_DK:(h + 1) * G_DK]
        k8 = qkv8[:, G_KEY + h * G_DK:G_KEY + (h + 1) * G_DK]
        v8 = qkv8[:, 2 * G_KEY + h * G_DV:2 * G_KEY + (h + 1) * G_DV]
        qk = jnp.sum(q8 * k8, axis=1, keepdims=True)
        lhs = _b(jnp.concatenate([q8, k8], axis=0))
        bh = beta8[:, G_HEADS + h:G_HEADS + h + 1]
        eg = eg8[:, h:h + 1]
        for j in range(DEC_BT):
            kc = ks_ref[h * G_DK:(h + 1) * G_DK, j:j + 1]
            s = s_in[j, h]
            qks = _dot(lhs, _b(s))
            bj, ej = bh[j:j + 1, :], eg[j:j + 1, :]
            u = v8[j:j + 1, :] * bj - (bj * ej) * qks[DEC_BT + j:DEC_BT + j + 1, :]
            o_ref[j:j + 1, hs] = ej * qks[j:j + 1, :] + qk[j:j + 1, :] * u
            s_out[j, h] = s * ej + kc * u
    gz8 = gz_ref[rows, :]
    for h in range(G_HEADS):
        hs = slice(h * G_DV, (h + 1) * G_DV)
        act_ref[:, hs] = _rms(o_ref[:, hs]) * nw_ref[...] * jax.nn.silu(gz8[:, hs])


def _dec_gdn(x, w, cst, cw, dtb, alog, nw, state, layer, prev):
    n = x.shape[0]
    consts = (x, w, cst, cw, dtb, alog, nw)
    sspec, prev, alias = _state_specs(state, prev, layer, (None, DEC_BT, G_HEADS, G_DK, G_DV))
    cspecs = [_const_spec(a.shape) for a in consts]
    cspecs[2] = pl.BlockSpec((None,) + cst.shape[1:], lambda i: (layer, 0, 0, 0), pipeline_mode=pl.Buffered(1))
    return pl.pallas_call(
        _dec_gdn_kernel,
        out_shape=(jax.ShapeDtypeStruct((n, G_VAL), f32), jax.ShapeDtypeStruct(state.shape, f32),
                   jax.ShapeDtypeStruct(cst.shape[1:], f32)),
        grid=(n // DEC_BT,),
        in_specs=cspecs + [sspec, pl.BlockSpec(memory_space=pl.ANY)],
        out_specs=(pl.BlockSpec((DEC_BT, G_VAL), lambda i: (i, 0)), sspec,
                   pl.BlockSpec(cst.shape[1:], lambda i: (0, 0, 0))),
        scratch_shapes=[pltpu.VMEM((n, G_QKV), f32),
                        pltpu.VMEM((G_KEY, n), f32),
                        pltpu.VMEM((n, G_VAL), f32),
                        pltpu.VMEM((n, PAD), f32), pltpu.VMEM((n, PAD), f32),
                        pltpu.VMEM((G_KEY, n), f32),
                        pltpu.VMEM((DEC_BT, G_VAL), f32)],
        input_output_aliases={len(consts) + 1: 1} if alias is None else alias,
        compiler_params=_params(("arbitrary",)),
        name="dec_gdn",
    )(*consts, state, prev)


def _rope_tables(pos):
    half = R_DK // 2
    inv = ROPE_BASE ** (-jnp.arange(half, dtype=f32) / half)
    ang = pos[:, None] * inv[None, :]
    cos, sin = jnp.cos(ang), jnp.sin(ang)
    return jnp.concatenate([cos, cos], axis=1), jnp.concatenate([-sin, sin], axis=1)


def _lane_pad(v, start=0):
    return jnp.zeros((1, PAD), f32).at[0, start:start + v.shape[0]].set(v)


def _stacked_weights(prm):
    return {k: _b(prm[k]) for k in ("ffn_wg", "ffn_wu", "ffn_wd", "w_ret_out", "w_ssm_out", "w_gdn_out", "w_o",
                                    "pe_proj", "pe_gate")}


def _layer_weights(i, prm):
    w_in = prm["w_in"][i]
    w_gdn = jnp.concatenate([w_in[:, OFF_GDN:OFF_MERGE], jnp.zeros((D_MODEL, PAD - 2 * G_HEADS), f32)], axis=1)
    return dict(
        ln_g=prm["ln_g"][i], ln_b=prm["ln_b"][i],
        w_gdn=_b(w_gdn), w_merge=_b(w_in[:, OFF_MERGE:]),
        ssm_cw=prm["ssm_conv_w"][i], ssm_cb=prm["ssm_conv_b"][i][None, :],
        ssm_dtb=_lane_pad(prm["ssm_dt_bias"][i]), ssm_alog=_lane_pad(prm["ssm_a_log"][i]),
        ssm_dexp=jnp.repeat(prm["ssm_d"][i], M_HEADDIM)[None, :], ssm_nw=prm["ssm_norm_w"][i][None, :],
        gdn_cw=prm["gdn_conv_w"][i],
        gdn_dtb=_lane_pad(prm["gdn_dt_bias"][i]), gdn_alog=_lane_pad(prm["gdn_a_log"][i]),
        gdn_nw=prm["gdn_norm_w"][i][None, :],
    )


def _post_mix(x1, acts, p, i, w, sw, tm):
    x2 = _merge(x1, *acts, w["w_merge"], sw["w_ret_out"], sw["w_ssm_out"], sw["w_gdn_out"], sw["w_o"],
                w["ln_g"], w["ln_b"], i, tm)
    return _ffn_pe(x2, p, i, sw["ffn_wg"], sw["ffn_wu"], sw["ffn_wd"], sw["pe_gate"], sw["pe_proj"],
                   w["ln_g"], w["ln_b"], tm)


def kernel(x_prompt, x_sample, state_ret, state_ssm, state_ssm_conv, state_gdn, state_gdn_conv,
           p_prompt, p_sample, ln_g, ln_b, ffn_wg, ffn_wu, ffn_wd, w_in,
           ssm_conv_w, ssm_conv_b, ssm_dt_bias, ssm_a_log, ssm_d, ssm_norm_w,
           gdn_conv_w, gdn_dt_bias, gdn_a_log, gdn_norm_w,
           w_ret_out, w_ssm_out, w_gdn_out, w_o, pe_proj, pe_gate):
    prm = dict(ln_g=ln_g, ln_b=ln_b, ffn_wg=ffn_wg, ffn_wu=ffn_wu, ffn_wd=ffn_wd, w_in=w_in,
               ssm_conv_w=ssm_conv_w, ssm_conv_b=ssm_conv_b, ssm_dt_bias=ssm_dt_bias,
               ssm_a_log=ssm_a_log, ssm_d=ssm_d, ssm_norm_w=ssm_norm_w,
               gdn_conv_w=gdn_conv_w, gdn_dt_bias=gdn_dt_bias, gdn_a_log=gdn_a_log,
               gdn_norm_w=gdn_norm_w, w_ret_out=w_ret_out, w_ssm_out=w_ssm_out,
               w_gdn_out=w_gdn_out, w_o=w_o, pe_proj=pe_proj, pe_gate=pe_gate)
    nb, seq, _ = x_prompt.shape
    ns = x_sample.shape[0]
    depth = w_in.shape[0]
    tl = min(TL_SCAN, seq)
    tm = min(TM_DENSE, nb * seq)
    chunk = CHUNK if seq % CHUNK == 0 else seq

    cos_p, sin_p = _rope_tables(jnp.arange(seq, dtype=f32))
    cos_s, sin_s = _rope_tables(jnp.full((1,), PAST_LEN, f32))
    xp = x_prompt.reshape(nb * seq, D_MODEL)
    xs = x_sample.reshape(ns, D_MODEL)
    pp = p_prompt.reshape(depth, nb * seq, PLE_DIM)
    ps = p_sample.reshape(depth, ns, PLE_DIM)
    ssm_state_t = jnp.swapaxes(state_ssm, 3, 4)
    ssm_conv_t = jnp.transpose(state_ssm_conv, (0, 2, 1, 3))
    gdn_conv_t = jnp.transpose(state_gdn_conv, (0, 2, 1, 3))

    sw = _stacked_weights(prm)
    w_in16 = _b(w_in)
    assert OFF_RET == 0 and OFF_SSD == W_IN_BLK and OFF_SSD + M_INNER + M_CONV_DIM + M_HEADS <= 2 * W_IN_BLK
    prompt_states, sample_convs = [], []
    t_r = t_s = t_g = None
    for i in range(depth):
        w = _layer_weights(i, prm)
        w["w_ret"], w["w_ssd"] = _w_in_block(w_in16, i, 0), _w_in_block(w_in16, i, 1)
        x1 = _ffn_ln(xp, sw["ffn_wg"], sw["ffn_wu"], sw["ffn_wd"], w["ln_g"], w["ln_b"], i, tm)
        a_r, s_r = _ret_scan(x1, w["w_ret"], cos_p, sin_p, nb, seq, tl, RET_CHUNK if seq % RET_CHUNK == 0 else chunk)
        a_s, s_s, c_s = _ssd_scan(x1, w["w_ssd"], w["ssm_cw"], w["ssm_cb"], w["ssm_dtb"], w["ssm_alog"],
                                  w["ssm_dexp"], w["ssm_nw"], nb, seq, tl, chunk)
        a_g, s_g, c_g = _gdn_scan(x1, w["w_gdn"], w["gdn_cw"], w["gdn_dtb"], w["gdn_alog"], w["gdn_nw"],
                                  nb, seq, tl, chunk)
        xp = _post_mix(x1, (a_r, a_s, a_g), pp, i, w, sw, tm)
        prompt_states.append((s_r, s_s, c_s, s_g, c_g))
        y1 = _ffn_ln(xs, sw["ffn_wg"], sw["ffn_wu"], sw["ffn_wd"], w["ln_g"], w["ln_b"], i, ns)
        b_r, t_r = _dec_ret(y1, w["w_ret"], cos_s, sin_s, state_ret, i, t_r)
        b_s, t_s, d_s = _dec_ssd(y1, w["w_ssd"], ssm_conv_t, w["ssm_cw"], w["ssm_cb"], w["ssm_dtb"],
                                 w["ssm_alog"], w["ssm_dexp"], w["ssm_nw"], ssm_state_t, i, t_s)
        b_g, t_g, d_g = _dec_gdn(y1, w["w_gdn"], gdn_conv_t, w["gdn_cw"], w["gdn_dtb"], w["gdn_alog"],
                                 w["gdn_nw"], state_gdn, i, t_g)
        xs = _post_mix(y1, (b_r, b_s, b_g), ps, i, w, sw, ns)
        sample_convs.append((d_s, d_g))

    r_p, s_p, sc_p, g_p, gc_p = (jnp.stack([s[j] for s in prompt_states]) for j in range(5))
    sc_s, gc_s = (jnp.transpose(jnp.stack([c[j] for c in sample_convs]), (0, 2, 1, 3)) for j in range(2))
    return (xp.reshape(nb, seq, D_MODEL), xs.reshape(ns, 1, D_MODEL),
            r_p, jnp.swapaxes(s_p, 3, 4), sc_p, g_p, gc_p,
            t_r, jnp.swapaxes(t_s, 3, 4), sc_s, t_g, gc_s)
```

```python
import functools
import math

import numpy as np
import jax
import jax.numpy as jnp
from jax import lax
from jax.experimental import pallas as pl
from jax.experimental.pallas import tpu as pltpu

f32, bf16 = jnp.float32, jnp.bfloat16

D_MODEL = 1024
DEPTH = 2
PAST_LEN = 16384
R_HEADS, R_DK, R_DV = 4, 128, 256
R_QK, R_VAL = R_HEADS * R_DK, R_HEADS * R_DV
ROPE_BASE = 10000.0
M_HEADS, M_HEADDIM, M_GROUPS, M_STATE = 16, 64, 2, 128
M_INNER = M_HEADS * M_HEADDIM
M_CONV_DIM = M_INNER + 2 * M_GROUPS * M_STATE
M_HPG = M_HEADS // M_GROUPS
M_GW = M_HPG * M_HEADDIM
G_HEADS, G_DK, G_DV = 8, 128, 128
G_KEY, G_VAL = G_HEADS * G_DK, G_HEADS * G_DV
G_QKV = 2 * G_KEY + G_VAL
CONV_W = 4
FFN_DIM = 2048
PLE_DIM = 256
DN_ALPHA = (2 * DEPTH) ** 0.25
LN_EPS = 1e-5
NORM_EPS = 1e-6

_sizes = (R_QK, R_QK, R_VAL, R_VAL, M_INNER, M_CONV_DIM, M_HEADS, G_QKV, G_VAL, G_HEADS, G_HEADS,
          D_MODEL, D_MODEL, D_MODEL)
_off = np.concatenate([[0], np.cumsum(_sizes)]).tolist()
OFF_RET, OFF_SSD, OFF_GDN, OFF_MERGE, IN_DIM = _off[0], _off[4], _off[7], _off[11], _off[14]

LANES = 128
SUBLANES = 8
VMEM_LIMIT = 56 * 2 ** 20

TM_DENSE = 512
TL_SCAN = 512
CHUNK = 64
FFN_CHUNK = 512
DEC_BT = 8
RET_CHUNK = 128
RET_UNROLL = 2
SSD_UNROLL = 4
GDN_PRE_UNROLL = 4
GDN_SCAN_UNROLL = 4
PAD = LANES

LOG_GAMMA = [math.log1p(-(2.0 ** (-5.0 - h))) for h in range(R_HEADS)]


def _dot(a, b):
    return jnp.dot(a, b, preferred_element_type=f32)


def _dot_nt(a, b):
    return lax.dot_general(a, b, (((1,), (1,)), ((), ())), preferred_element_type=f32)


def _dot_tn(a, b):
    return lax.dot_general(a, b, (((0,), (0,)), ((), ())), preferred_element_type=f32)


def _b(x):
    return x.astype(bf16)


def _layer_norm(y, g, b):
    mu = jnp.mean(y, -1, keepdims=True)
    var = jnp.mean(jnp.square(y - mu), -1, keepdims=True)
    return (y - mu) * lax.rsqrt(var + LN_EPS) * g + b


def _rms(y):
    return y * lax.rsqrt(jnp.mean(jnp.square(y), -1, keepdims=True) + NORM_EPS)


def _const_spec(shape):
    return pl.BlockSpec(shape, lambda *_: (0,) * len(shape), pipeline_mode=pl.Buffered(1))


def _layer_spec(arr, *lead):
    blk = (None,) * len(lead) + arr.shape[len(lead):]
    tail = (0,) * (arr.ndim - len(lead))
    return pl.BlockSpec(blk, lambda *_: tuple(lead) + tail, pipeline_mode=pl.Buffered(1))


W_IN_BLK = 3072


def _w_in_block(wt_all, layer, blk):
    return wt_all, pl.BlockSpec((None, W_IN_BLK, D_MODEL), lambda *_: (layer, blk, 0), pipeline_mode=pl.Buffered(1))


def _weight(w):
    return w if isinstance(w, tuple) else (w, _const_spec(w.shape))


def _params(sem):
    return pltpu.CompilerParams(dimension_semantics=sem, vmem_limit_bytes=VMEM_LIMIT)


def _split3(x):
    a1 = _b(x)
    r1 = x - a1.astype(f32)
    a2 = _b(r1)
    a3 = _b(r1 - a2.astype(f32))
    return a1, a2, a3


def _chunk_iotas(c):
    ii = lax.broadcasted_iota(jnp.int32, (c, c), 0)
    jj = lax.broadcasted_iota(jnp.int32, (c, c), 1)
    return ii, jj


def _swiglu(x, wg_ref, wu_ref, wd_ref):
    xb = _b(x)
    acc = None
    for c in range(FFN_DIM // FFN_CHUNK):
        sl = slice(c * FFN_CHUNK, (c + 1) * FFN_CHUNK)
        a = jax.nn.silu(_dot(xb, wg_ref[:, sl])) * _dot(xb, wu_ref[:, sl])
        part = _dot(_b(a), wd_ref[sl, :])
        acc = part if acc is None else acc + part
    return acc


def _ffn_ln_kernel(x_ref, wg_ref, wu_ref, wd_ref, g_ref, b_ref, o_ref):
    x = x_ref[...]
    y = DN_ALPHA * x + 0.5 * _swiglu(x, wg_ref, wu_ref, wd_ref)
    o_ref[...] = _layer_norm(y, g_ref[0:1, :], b_ref[0:1, :])


def _ffn_ln(x, wg, wu, wd, g, b, layer, tm):
    n = x.shape[0]
    return pl.pallas_call(
        _ffn_ln_kernel,
        out_shape=jax.ShapeDtypeStruct((n, D_MODEL), f32),
        grid=(n // tm,),
        in_specs=[pl.BlockSpec((tm, D_MODEL), lambda i: (i, 0)),
                  _layer_spec(wg, layer, 0), _layer_spec(wu, layer, 0), _layer_spec(wd, layer, 0),
                  _const_spec(g.shape), _const_spec(b.shape)],
        out_specs=pl.BlockSpec((tm, D_MODEL), lambda i: (i, 0)),
        compiler_params=_params(("parallel",)),
        name="ffn_ln",
    )(x, wg, wu, wd, g, b)


def _merge_kernel(x_ref, ar_ref, as_ref, ag_ref, wm_ref, wr_ref, ws_ref, wgd_ref, wo_ref, g_ref, b_ref, o_ref):
    x = x_ref[...]
    m = _dot_nt(_b(x), wm_ref[...])
    yr = _dot(_b(ar_ref[...]), wr_ref[...])
    ys = _dot(_b(as_ref[...]), ws_ref[...])
    yg = _dot(_b(ag_ref[...]), wgd_ref[...])
    mixed = (jax.nn.sigmoid(m[:, 0:D_MODEL]) * yr + jax.nn.sigmoid(m[:, D_MODEL:2 * D_MODEL]) * ys
             + jax.nn.sigmoid(m[:, 2 * D_MODEL:3 * D_MODEL]) * yg)
    y = DN_ALPHA * x + _dot(_b(mixed), wo_ref[...])
    o_ref[...] = _layer_norm(y, g_ref[1:2, :], b_ref[1:2, :])


def _merge(x, ar, a_s, ag, wm, wr, ws, wgd, wo, g, b, layer, tm):
    n = x.shape[0]
    tok = lambda i: (i, 0)
    return pl.pallas_call(
        _merge_kernel,
        out_shape=jax.ShapeDtypeStruct((n, D_MODEL), f32),
        grid=(n // tm,),
        in_specs=[pl.BlockSpec((tm, D_MODEL), tok)] * 4
                 + [_const_spec(wm.shape)] + [_layer_spec(w, layer) for w in (wr, ws, wgd, wo)]
                 + [_const_spec(g.shape), _const_spec(b.shape)],
        out_specs=pl.BlockSpec((tm, D_MODEL), tok),
        compiler_params=_params(("parallel",)),
        name="merge",
    )(x, ar, a_s, ag, wm, wr, ws, wgd, wo, g, b)


def _ffn_pe_kernel(x_ref, p_ref, wg_ref, wu_ref, wd_ref, pg_ref, pp_ref, g_ref, b_ref, o_ref):
    x = x_ref[...]
    x = _layer_norm(DN_ALPHA * x + 0.5 * _swiglu(x, wg_ref, wu_ref, wd_ref), g_ref[2:3, :], b_ref[2:3, :])
    pe = jax.nn.sigmoid(_dot(_b(x), pg_ref[...])) * _dot(_b(p_ref[...]), pp_ref[...])
    o_ref[...] = _layer_norm(DN_ALPHA * x + pe, g_ref[3:4, :], b_ref[3:4, :])


def _ffn_pe(x, p, layer, wg, wu, wd, pg, pp, g, b, tm):
    n = x.shape[0]
    return pl.pallas_call(
        _ffn_pe_kernel,
        out_shape=jax.ShapeDtypeStruct((n, D_MODEL), f32),
        grid=(n // tm,),
        in_specs=[pl.BlockSpec((tm, D_MODEL), lambda i: (i, 0)),
                  pl.BlockSpec((None, tm, PLE_DIM), lambda i: (layer, i, 0)),
                  _layer_spec(wg, layer, 1), _layer_spec(wu, layer, 1), _layer_spec(wd, layer, 1),
                  _layer_spec(pg, layer), _layer_spec(pp, layer),
                  _const_spec(g.shape), _const_spec(b.shape)],
        out_specs=pl.BlockSpec((tm, D_MODEL), lambda i: (i, 0)),
        compiler_params=_params(("parallel",)),
        name="ffn_pe",
    )(x, p, wg, wu, wd, pg, pp, g, b)


def _rope_inplace(proj_ref, off, cos, sin, scale):
    t = proj_ref[:, off:off + R_DK]
    t = t * cos + pltpu.roll(t, R_DK // 2, axis=1) * sin
    if scale != 1.0:
        t = t * scale
    proj_ref[:, off:off + R_DK] = t


def _ret_scan_kernel(x_ref, w_ref, cos_ref, sin_ref, act_ref, st_ref, proj_ref, s_ref, *, tl, c):
    l = pl.program_id(1)

    @pl.when(l == 0)
    def _():
        s_ref[...] = jnp.zeros_like(s_ref)

    proj_ref[...] = _dot_nt(_b(x_ref[...]), w_ref[...])
    cos, sin = cos_ref[...], sin_ref[...]
    for h in range(R_HEADS):
        _rope_inplace(proj_ref, h * R_DK, cos, sin, 1.0)
        _rope_inplace(proj_ref, R_QK + h * R_DK, cos, sin, R_DK ** -0.5)

    ii, jj = _chunk_iotas(c)
    dif = (ii - jj).astype(f32)
    ci = lax.broadcasted_iota(jnp.int32, (c, 1), 0).astype(f32)
    decay = [jnp.where(dif >= 0, jnp.exp(dif * lg), 0.0) for lg in LOG_GAMMA]
    e_col = [jnp.exp((ci + 1.0) * lg) for lg in LOG_GAMMA]
    w_col = [jnp.exp((c - 1.0 - ci) * lg) for lg in LOG_GAMMA]

    def chunk(ck):
        rows = pl.ds(pl.multiple_of(ck * c, c), c)
        hd = range(R_HEADS)
        q = [proj_ref[rows, h * R_DK:(h + 1) * R_DK] for h in hd]
        k = [proj_ref[rows, R_QK + h * R_DK:R_QK + (h + 1) * R_DK] for h in hd]
        v = [_b(proj_ref[rows, 2 * R_QK + h * R_DV:2 * R_QK + (h + 1) * R_DV]) for h in hd]
        s = [s_ref[h] for h in hd]
        scores = [_dot_nt(_b(q[h]), _b(k[h])) * decay[h] for h in hd]
        inter = [_dot(_b(q[h] * e_col[h]), _b(s[h])) for h in hd]
        for h in hd:
            s_ref[h] = s[h] * math.exp(c * LOG_GAMMA[h]) + _dot_tn(_b(k[h] * w_col[h]), v[h])
        o = [_dot(_b(scores[h]), v[h]) + inter[h] for h in hd]
        for h in hd:
            mu = jnp.mean(o[h], -1, keepdims=True)
            var = jnp.mean(jnp.square(o[h] - mu), -1, keepdims=True)
            on = (o[h] - mu) * lax.rsqrt(var + LN_EPS)
            g = proj_ref[rows, 2 * R_QK + R_VAL + h * R_DV:2 * R_QK + R_VAL + (h + 1) * R_DV]
            act_ref[rows, h * R_DV:(h + 1) * R_DV] = _b(on * jax.nn.silu(g))

    def chunks(it, carry):
        for j in range(RET_UNROLL):
            chunk(it * RET_UNROLL + j)
        return carry

    lax.fori_loop(0, tl // (c * RET_UNROLL), chunks, 0)

    @pl.when(l == pl.num_programs(1) - 1)
    def _():
        st_ref[0] = s_ref[...]


def _ret_scan(x, w, cos, sin, nb, nl_tok, tl, c):
    w, wspec = _weight(w)
    nl = nl_tok // tl
    tok = lambda b, l: (b * nl + l, 0)
    return pl.pallas_call(
        functools.partial(_ret_scan_kernel, tl=tl, c=c),
        out_shape=(jax.ShapeDtypeStruct((nb * nl_tok, R_VAL), bf16),
                   jax.ShapeDtypeStruct((nb, R_HEADS, R_DK, R_DV), f32)),
        grid=(nb, nl),
        in_specs=[pl.BlockSpec((tl, D_MODEL), tok), wspec,
                  pl.BlockSpec((tl, R_DK), lambda b, l: (l, 0)),
                  pl.BlockSpec((tl, R_DK), lambda b, l: (l, 0))],
        out_specs=(pl.BlockSpec((tl, R_VAL), tok),
                   pl.BlockSpec((1, R_HEADS, R_DK, R_DV), lambda b, l: (b, 0, 0, 0))),
        scratch_shapes=[pltpu.VMEM((tl, 2 * R_QK + 2 * R_VAL), f32),
                        pltpu.VMEM((R_HEADS, R_DK, R_DV), f32)],
        compiler_params=_params(("parallel", "arbitrary")),
        name="ret_scan",
    )(x, w, cos, sin)


CONV_BLK = 512


def _project_and_conv(xb, w_ref, w_off, width, xbuf_ref, cw_ref, cb_ref, dst_ref, tl, first):
    @pl.when(first)
    def _():
        xbuf_ref[0:SUBLANES, :] = jnp.zeros((SUBLANES, width), f32)

    def project(n):
        cs = slice(n * CONV_BLK, (n + 1) * CONV_BLK)
        xbuf_ref[SUBLANES:SUBLANES + tl, cs] = _dot_nt(xb, w_ref[w_off + cs.start:w_off + cs.stop, :])

    def conv(n):
        cs = slice(n * CONV_BLK, (n + 1) * CONV_BLK)
        acc = xbuf_ref[SUBLANES:SUBLANES + tl, cs] * cw_ref[CONV_W - 1:CONV_W, cs]
        for j in range(CONV_W - 1):
            r0 = SUBLANES - (CONV_W - 1) + j
            acc = acc + xbuf_ref[r0:r0 + tl, cs] * cw_ref[j:j + 1, cs]
        if cb_ref is not None:
            acc = acc + cb_ref[:, cs]
        dst_ref[:, cs] = jax.nn.silu(acc)
        xbuf_ref[0:SUBLANES, cs] = xbuf_ref[tl:tl + SUBLANES, cs]

    nblk = width // CONV_BLK
    project(0)
    for n in range(nblk):
        if n + 1 < nblk:
            project(n + 1)
        conv(n)


def _head_expander(e2_ref, width):
    er = lax.broadcasted_iota(jnp.int32, e2_ref.shape, 0)
    el = lax.broadcasted_iota(jnp.int32, e2_ref.shape, 1)
    e2_ref[...] = _b(((er & (PAD - 1)) == (el >> (width.bit_length() - 1))).astype(f32))


def _chunk_block_mask(tri_ref, tl, c):
    log2c = c.bit_length() - 1
    ti = lax.broadcasted_iota(jnp.int32, (tl, tl), 0)
    tj = lax.broadcasted_iota(jnp.int32, (tl, tl), 1)
    tri_ref[...] = _b(((ti >= tj) & ((ti >> log2c) == (tj >> log2c))).astype(f32))


def _chunk_cumsum(tri_ref, cum_ref, la, c):
    a1, a2, a3 = _split3(la)
    cum = _dot(tri_ref[...], a1) + _dot(tri_ref[...], a2) + _dot(tri_ref[...], a3)
    cum_ref[...] = cum
    tl = la.shape[0]
    tot = jnp.concatenate([jnp.broadcast_to(cum_ref[k * c + c - 1:k * c + c, :], (c, la.shape[1]))
                           for k in range(tl // c)], axis=0)
    return cum, tot


def _expand_heads(v, e2_ref):
    hi = _b(v)
    lo = _b(v - hi.astype(f32))
    return _dot(jnp.concatenate([hi, lo], axis=1), e2_ref[...])


def _expand_heads_exact(v, e3_ref):
    return _dot(jnp.concatenate(_split3(v), axis=1), e3_ref[...])


def _ssd_scan_kernel(x_ref, w_ref, cw_ref, cb_ref, dtb_ref, alog_ref, dexp_ref, nw_ref,
                     act_ref, st_ref, conv_ref,
                     z_ref, xbuf_ref, xc_ref, cum_ref, cumx_ref, xdt_ref, xw_ref, ee_ref,
                     tri_ref, e3_ref, s_ref, *, tl, c):
    l = pl.program_id(1)

    @pl.when(l == 0)
    def _():
        s_ref[...] = jnp.zeros_like(s_ref)
        _chunk_block_mask(tri_ref, tl, c)
        _head_expander(e3_ref, M_HEADDIM)

    xb = _b(x_ref[...])
    dt = jax.nn.softplus(_dot_nt(xb, w_ref[M_INNER + M_CONV_DIM:M_INNER + M_CONV_DIM + PAD, :]) + dtb_ref[...])
    _project_and_conv(xb, w_ref, M_INNER, M_CONV_DIM, xbuf_ref, cw_ref, cb_ref, xc_ref, tl, l == 0)
    z_ref[...] = _dot_nt(xb, w_ref[0:M_INNER, :])
    la = -jnp.exp(alog_ref[...]) * dt
    cum, tot = _chunk_cumsum(tri_ref, cum_ref, la, c)
    cumx_ref[...] = _expand_heads_exact(cum, e3_ref)
    e2_ref = e3_ref.at[0:2 * PAD, :]
    xdt = xc_ref[:, 0:M_INNER] * _expand_heads(dt, e2_ref)
    xdt_ref[...] = xdt
    xw_ref[...] = xdt * _expand_heads(jnp.exp(tot - cum), e2_ref)
    ee_ref[...] = _expand_heads(jnp.exp(cum), e2_ref)

    half = c
    lane = lax.broadcasted_iota(jnp.int32, (c, 2 * half), 1)
    rowi = lax.broadcasted_iota(jnp.int32, (c, 2 * half), 0)
    left = lane < half
    causal2 = (lane & (half - 1)) <= rowi
    b_off, c_off = M_INNER, M_INNER + M_GROUPS * M_STATE
    gr = range(M_GROUPS)
    pairs = [(g, pp) for g in gr for pp in range(M_HPG // 2)]

    def chunk(ck):
        r0 = pl.multiple_of(ck * c, c)
        rows = pl.ds(r0, c)
        cum_c = cum_ref[rows, :]
        cum_t = jnp.concatenate([cum_c, cum_c], axis=0).T
        bb = [_b(xc_ref[rows, b_off + g * M_STATE:b_off + (g + 1) * M_STATE]) for g in gr]
        cb = [_b(xc_ref[rows, c_off + g * M_STATE:c_off + (g + 1) * M_STATE]) for g in gr]
        g2 = [_dot_nt(cb[g], jnp.concatenate([bb[g], bb[g]], axis=0)) for g in gr]
        sg = [s_ref[g] for g in gr]
        inter = [ee_ref[rows, g * M_GW:(g + 1) * M_GW] * _dot(cb[g], _b(sg[g])) for g in gr]
        for g in gr:
            e_last = ee_ref[pl.ds(r0 + c - 1, 1), g * M_GW:(g + 1) * M_GW]
            s_ref[g] = sg[g] * e_last + _dot_tn(bb[g], _b(xw_ref[rows, g * M_GW:(g + 1) * M_GW]))
        a2s, rhs = [], []
        for g, pp in pairs:
            h0 = g * M_HPG + 2 * pp
            ls = slice(h0 * M_HEADDIM, (h0 + 2) * M_HEADDIM)
            colsel = cumx_ref[rows, ls]
            rowsel = jnp.where(left[0:1, :], cum_t[h0:h0 + 1, :], cum_t[h0 + 1:h0 + 2, :])
            d2 = jnp.where(causal2, jnp.exp(colsel - rowsel), 0.0)
            a2s.append(_b(g2[g] * d2))
            xp = xdt_ref[rows, ls]
            rhs.append(_b(jnp.concatenate([jnp.where(left, xp, 0.0), jnp.where(left, 0.0, xp)], axis=0)))
        intra = [_dot(a, r) for a, r in zip(a2s, rhs)]
        ys = []
        for n, (g, pp) in enumerate(pairs):
            h0 = g * M_HPG + 2 * pp
            ls = slice(h0 * M_HEADDIM, (h0 + 2) * M_HEADDIM)
            y = intra[n] + inter[g][:, pp * 2 * M_HEADDIM:(pp + 1) * 2 * M_HEADDIM] + dexp_ref[:, ls] * xc_ref[rows, ls]
            ys.append(y * jax.nn.silu(z_ref[rows, ls]))
        for g in gr:
            mine = [n for n, (gg, _) in enumerate(pairs) if gg == g]
            ms = sum(jnp.sum(jnp.square(ys[n]), -1, keepdims=True) for n in mine) * (1.0 / M_GW)
            r = lax.rsqrt(ms + NORM_EPS)
            for n in mine:
                h0 = g * M_HPG + 2 * pairs[n][1]
                ls = slice(h0 * M_HEADDIM, (h0 + 2) * M_HEADDIM)
                act_ref[rows, ls] = _b(ys[n] * r * nw_ref[:, ls])

    def chunks(it, carry):
        for j in range(SSD_UNROLL):
            chunk(it * SSD_UNROLL + j)
        return carry

    lax.fori_loop(0, tl // (c * SSD_UNROLL), chunks, 0)

    @pl.when(l == pl.num_programs(1) - 1)
    def _():
        for h in range(M_HEADS):
            g, hh = divmod(h, M_HPG)
            st_ref[0, h] = s_ref[g][:, hh * M_HEADDIM:(hh + 1) * M_HEADDIM].T
        conv_ref[0] = xbuf_ref[SUBLANES - (CONV_W - 1):SUBLANES, :]


def _ssd_scan(x, w, cw, cb, dtb, alog, dexp, nw, nb, nl_tok, tl, c):
    assert 2 * c == LANES and 2 * M_HEADDIM == LANES, "head pairs are packed into one 128-lane slab"
    w, wspec = _weight(w)
    nl = nl_tok // tl
    tok = lambda b, l: (b * nl + l, 0)
    return pl.pallas_call(
        functools.partial(_ssd_scan_kernel, tl=tl, c=c),
        out_shape=(jax.ShapeDtypeStruct((nb * nl_tok, M_INNER), bf16),
                   jax.ShapeDtypeStruct((nb, M_HEADS, M_HEADDIM, M_STATE), f32),
                   jax.ShapeDtypeStruct((nb, CONV_W - 1, M_CONV_DIM), f32)),
        grid=(nb, nl),
        in_specs=[pl.BlockSpec((tl, D_MODEL), tok)]
                 + [wspec] + [_const_spec(a.shape) for a in (cw, cb, dtb, alog, dexp, nw)],
        out_specs=(pl.BlockSpec((tl, M_INNER), tok),
                   pl.BlockSpec((1, M_HEADS, M_HEADDIM, M_STATE), lambda b, l: (b, 0, 0, 0)),
                   pl.BlockSpec((1, CONV_W - 1, M_CONV_DIM), lambda b, l: (b, 0, 0))),
        scratch_shapes=[pltpu.VMEM((tl, M_INNER), f32),
                        pltpu.VMEM((tl + SUBLANES, M_CONV_DIM), f32),
                        pltpu.VMEM((tl, M_CONV_DIM), f32),
                        pltpu.VMEM((tl, PAD), f32),
                        pltpu.VMEM((tl, M_INNER), f32),
                        pltpu.VMEM((tl, M_INNER), f32),
                        pltpu.VMEM((tl, M_INNER), f32),
                        pltpu.VMEM((tl, M_INNER), f32),
                        pltpu.VMEM((tl, tl), bf16),
                        pltpu.VMEM((3 * PAD, M_INNER), bf16),
                        pltpu.VMEM((M_GROUPS, M_STATE, M_GW), f32)],
        compiler_params=_params(("parallel", "arbitrary")),
        name="ssd_scan",
    )(x, w, cw, cb, dtb, alog, dexp, nw)


def _pair_blockdiag(x, left):
    return jnp.concatenate([jnp.where(left, x, jnp.zeros_like(x)), jnp.where(left, jnp.zeros_like(x), x)], axis=0)


def _gdn_scan_kernel(x_ref, w_ref, cw_ref, dtb_ref, alog_ref, nw_ref,
                     act_ref, st_ref, conv_ref,
                     gz_ref, xbuf_ref, qkv_ref, cum_ref, ee_ref,
                     q16_ref, k16_ref, kb16_ref, qe16_ref, kbe16_ref, kw16_ref, vb16_ref,
                     wy_ref, u0_ref, attn_ref, tri_ref, e2_ref, s_ref, *, tl, c):
    l = pl.program_id(1)
    n_pairs = G_HEADS // 2
    pw = 2 * G_DK

    @pl.when(l == 0)
    def _():
        s_ref[...] = jnp.zeros_like(s_ref)
        _chunk_block_mask(tri_ref, tl, c)
        _head_expander(e2_ref, G_DK)

    xb = _b(x_ref[...])
    ab = _dot_nt(xb, w_ref[G_QKV + G_VAL:G_QKV + G_VAL + PAD, :])
    _project_and_conv(xb, w_ref, 0, G_QKV, xbuf_ref, cw_ref, None, qkv_ref, tl, l == 0)
    gz_ref[...] = _dot_nt(xb, w_ref[G_QKV:G_QKV + G_VAL, :])
    g = -jnp.exp(alog_ref[...]) * jax.nn.softplus(ab + dtb_ref[...])
    cum, tot = _chunk_cumsum(tri_ref, cum_ref, g, c)
    beta = pltpu.roll(jax.nn.sigmoid(ab), PAD - G_HEADS, axis=1)
    e_c, w_c = jnp.exp(cum), jnp.exp(tot - cum)
    for hb in range(n_pairs):
        ls = slice(hb * pw, (hb + 1) * pw)
        e_x = _expand_heads(e_c, e2_ref.at[:, ls])
        w_x = _expand_heads(w_c, e2_ref.at[:, ls])
        b_x = _expand_heads(beta, e2_ref.at[:, ls])
        ee_ref[:, ls] = e_x
        qn, kn = [], []
        for t in range(2):
            hs = slice((2 * hb + t) * G_DK, (2 * hb + t + 1) * G_DK)
            qt, kt = qkv_ref[:, hs], qkv_ref[:, G_KEY + hs.start:G_KEY + hs.stop]
            qn.append(qt * lax.rsqrt(jnp.sum(jnp.square(qt), -1, keepdims=True) + NORM_EPS) * (G_DK ** -0.5))
            kn.append(kt * lax.rsqrt(jnp.sum(jnp.square(kt), -1, keepdims=True) + NORM_EPS))
        q, k = jnp.concatenate(qn, axis=1), jnp.concatenate(kn, axis=1)
        kb = k * b_x
        q16_ref[:, ls] = _b(q)
        k16_ref[:, ls] = _b(k)
        kb16_ref[:, ls] = _b(kb)
        qe16_ref[:, ls] = _b(q * e_x)
        kbe16_ref[:, ls] = _b(kb * e_x)
        kw16_ref[:, ls] = _b(k * w_x)
        vb16_ref[:, ls] = _b(qkv_ref[:, 2 * G_KEY + hb * pw:2 * G_KEY + (hb + 1) * pw] * b_x)

    lane = lax.broadcasted_iota(jnp.int32, (c, 2 * c), 1)
    rowi = lax.broadcasted_iota(jnp.int32, (c, 2 * c), 0)
    left = lane < c
    jloc = lane & (c - 1)
    causal2, strict2 = jloc <= rowi, jloc < rowi
    eye2 = (jloc == rowi).astype(f32)
    left_w = lax.broadcasted_iota(jnp.int32, (c, pw), 1) < G_DK
    left_s = lax.broadcasted_iota(jnp.int32, (G_DK, pw), 1) < G_DK
    pr = range(n_pairs)

    def precompute(it, carry):
        cks = [it * GDN_PRE_UNROLL + j for j in range(GDN_PRE_UNROLL)]
        rows = [pl.ds(pl.multiple_of(ck * c, c), c) for ck in cks]
        lsl = [slice(p * pw, (p + 1) * pw) for p in pr]
        cp = [(j, p) for j in range(GDN_PRE_UNROLL) for p in pr]
        cum_c = [cum_ref[r, :] for r in rows]
        cum_t = [jnp.concatenate([x, x], axis=0).T for x in cum_c]
        d2 = []
        for j, p in cp:
            colsel = jnp.where(left, cum_c[j][:, 2 * p:2 * p + 1], cum_c[j][:, 2 * p + 1:2 * p + 2])
            rowsel = jnp.where(left[0:1, :], cum_t[j][2 * p:2 * p + 1, :], cum_t[j][2 * p + 1:2 * p + 2, :])
            d2.append(jnp.where(causal2, jnp.exp(colsel - rowsel), 0.0))
        kbd = [_pair_blockdiag(k16_ref[rows[j], lsl[p]], left_w) for j, p in cp]
        kq = [_dot_nt(jnp.concatenate([kb16_ref[rows[j], lsl[p]], q16_ref[rows[j], lsl[p]]], axis=0), kbd[n])
              for n, (j, p) in enumerate(cp)]
        lm = [jnp.where(strict2, kq[n][0:c, :] * d2[n], 0.0) for n in range(len(cp))]
        attn = [kq[n][c:2 * c, :] * d2[n] for n in range(len(cp))]
        for n, (j, p) in enumerate(cp):
            attn_ref[rows[j], p * 2 * c:(p + 1) * 2 * c] = _b(attn[n])
        ps = [eye2 - x for x in lm]
        ms = lm
        kpow = 2
        while kpow < c:
            ms = [_dot(_b(m), _b(_pair_blockdiag(m, left))) for m in ms]
            ps = [x + _dot(_b(x), _b(_pair_blockdiag(m, left))) for x, m in zip(ps, ms)]
            kpow *= 2
        rhs = [jnp.concatenate([_pair_blockdiag(kbe16_ref[rows[j], lsl[p]], left_w),
                                _pair_blockdiag(vb16_ref[rows[j], lsl[p]], left_w)], axis=1) for j, p in cp]
        wu = [_dot(_b(ps[n]), rhs[n]) for n in range(len(cp))]
        for n, (j, p) in enumerate(cp):
            wy_ref[rows[j], lsl[p]] = _b(wu[n][:, 0:pw])
            u0_ref[rows[j], lsl[p]] = wu[n][:, pw:2 * pw]
        return carry

    lax.fori_loop(0, tl // (c * GDN_PRE_UNROLL), precompute, 0)

    def scan_chunk(ck):
        r0 = pl.multiple_of(ck * c, c)
        rows = pl.ds(r0, c)
        lsl = [slice(p * pw, (p + 1) * pw) for p in pr]
        sp = [s_ref[p] for p in pr]
        sbd = [_b(_pair_blockdiag(sp[p], left_s)) for p in pr]
        r = [_dot(jnp.concatenate([wy_ref[rows, lsl[p]], qe16_ref[rows, lsl[p]]], axis=0), sbd[p]) for p in pr]
        u = [u0_ref[rows, lsl[p]] - r[p][0:c, :] for p in pr]
        ubd = [_b(_pair_blockdiag(u[p], left_w)) for p in pr]
        for p in pr:
            kw = kw16_ref[rows, lsl[p]]
            kw_stack = jnp.concatenate([kw[:, 0:G_DK], kw[:, G_DK:pw]], axis=0)
            e_last = ee_ref[pl.ds(r0 + c - 1, 1), lsl[p]]
            s_ref[p] = sp[p] * e_last + _dot_tn(kw_stack, ubd[p])
        o = [r[p][c:2 * c, :] + _dot(attn_ref[rows, p * 2 * c:(p + 1) * 2 * c], ubd[p]) for p in pr]
        for p in pr:
            for t in range(2):
                hs = slice((2 * p + t) * G_DV, (2 * p + t + 1) * G_DV)
                act_ref[rows, hs] = _b(_rms(o[p][:, t * G_DV:(t + 1) * G_DV]) * nw_ref[...]
                                       * jax.nn.silu(gz_ref[rows, hs]))

    def scan(it, carry):
        for j in range(GDN_SCAN_UNROLL):
            scan_chunk(it * GDN_SCAN_UNROLL + j)
        return carry

    lax.fori_loop(0, tl // (c * GDN_SCAN_UNROLL), scan, 0)

    @pl.when(l == pl.num_programs(1) - 1)
    def _():
        for h in range(G_HEADS):
            st_ref[0, h] = s_ref[h // 2][:, (h % 2) * G_DV:(h % 2 + 1) * G_DV]
        conv_ref[0] = xbuf_ref[SUBLANES - (CONV_W - 1):SUBLANES, :]


def _gdn_scan(x, w, cw, dtb, alog, nw, nb, nl_tok, tl, c):
    assert 2 * c == LANES and G_DK == G_DV == LANES, "two heads' (c, c) blocks share one 128-lane slab"
    nl = nl_tok // tl
    tok = lambda b, l: (b * nl + l, 0)
    return pl.pallas_call(
        functools.partial(_gdn_scan_kernel, tl=tl, c=c),
        out_shape=(jax.ShapeDtypeStruct((nb * nl_tok, G_VAL), bf16),
                   jax.ShapeDtypeStruct((nb, G_HEADS, G_DK, G_DV), f32),
                   jax.ShapeDtypeStruct((nb, CONV_W - 1, G_QKV), f32)),
        grid=(nb, nl),
        in_specs=[pl.BlockSpec((tl, D_MODEL), tok)]
                 + [_const_spec(a.shape) for a in (w, cw, dtb, alog, nw)],
        out_specs=(pl.BlockSpec((tl, G_VAL), tok),
                   pl.BlockSpec((1, G_HEADS, G_DK, G_DV), lambda b, l: (b, 0, 0, 0)),
                   pl.BlockSpec((1, CONV_W - 1, G_QKV), lambda b, l: (b, 0, 0))),
        scratch_shapes=[pltpu.VMEM((tl, G_VAL), f32),
                        pltpu.VMEM((tl + SUBLANES, G_QKV), f32),
                        pltpu.VMEM((tl, G_QKV), f32),
                        pltpu.VMEM((tl, PAD), f32),
                        pltpu.VMEM((tl, G_KEY), f32),
                        ] + [pltpu.VMEM((tl, G_KEY), bf16)] * 7 + [
                        pltpu.VMEM((tl, G_KEY), bf16),
                        pltpu.VMEM((tl, G_VAL), f32),
                        pltpu.VMEM((tl, G_HEADS * c), bf16),
                        pltpu.VMEM((tl, tl), bf16),
                        pltpu.VMEM((2 * PAD, G_KEY), bf16),
                        pltpu.VMEM((G_HEADS // 2, G_DK, 2 * G_DV), f32)],
        compiler_params=_params(("parallel", "arbitrary")),
        name="gdn_scan",
    )(x, w, cw, dtb, alog, nw)


def _token_lanes_to_front(src_ref, dst_ref, i):
    n = src_ref.shape[1]
    dst_ref[...] = pltpu.roll(src_ref[...], (n - i * DEC_BT) % n, axis=1)


def _conv_step(cst_ref, x_new, cw_ref, cb_ref, conv_out_ref):
    acc = x_new * cw_ref[CONV_W - 1:CONV_W, :]
    for j in range(CONV_W - 1):
        acc = acc + cst_ref[j] * cw_ref[j:j + 1, :]
    if cb_ref is not None:
        acc = acc + cb_ref[...]
    for j in range(CONV_W - 2):
        conv_out_ref[j] = cst_ref[j + 1]
    conv_out_ref[CONV_W - 2] = x_new
    return jax.nn.silu(acc)


def _state_specs(state, prev, layer, blk):
    zeros = (0,) * (len(blk) - 2)
    spec = pl.BlockSpec(blk, lambda i: (layer, i) + zeros)
    if prev is None:
        prev = jnp.zeros((SUBLANES, LANES), f32)
        alias = {}
    else:
        alias = None
    return spec, prev, alias


def _dec_ret_kernel(x_ref, w_ref, cos_ref, sin_ref, s_in, prev_ref, act_ref, s_out,
                    q_ref, kt_ref, vg_ref, ks_ref, o_ref):
    del prev_ref
    i = pl.program_id(0)

    @pl.when(i == 0)
    def _():
        proj = _dot_nt(_b(x_ref[...]), w_ref[...])
        cos, sin = cos_ref[...], sin_ref[...]
        for h in range(R_HEADS):
            hs = slice(h * R_DK, (h + 1) * R_DK)
            t = proj[:, hs]
            q_ref[:, hs] = t * cos + pltpu.roll(t, R_DK // 2, axis=1) * sin
            t = proj[:, R_QK + h * R_DK:R_QK + (h + 1) * R_DK]
            q_ref[:, R_QK + h * R_DK:R_QK + (h + 1) * R_DK] = t = (
                t * cos + pltpu.roll(t, R_DK // 2, axis=1) * sin) * (R_DK ** -0.5)
            kt_ref[hs, :] = t.T
        vg_ref[...] = proj[:, 2 * R_QK:2 * R_QK + 2 * R_VAL]

    _token_lanes_to_front(kt_ref, ks_ref, i)
    rows = pl.ds(pl.multiple_of(i * DEC_BT, DEC_BT), DEC_BT)
    vg, qk8 = vg_ref[rows, :], q_ref[rows, :]
    for h in range(R_HEADS):
        q8 = qk8[:, h * R_DK:(h + 1) * R_DK]
        k8 = qk8[:, R_QK + h * R_DK:R_QK + (h + 1) * R_DK]
        qk = jnp.sum(q8 * k8, axis=1, keepdims=True)
        lhs = _b(jnp.concatenate([q8, q8], axis=0))
        gam = math.exp(LOG_GAMMA[h])
        for j in range(DEC_BT):
            kc = ks_ref[h * R_DK:(h + 1) * R_DK, j:j + 1]
            v_row = vg[j:j + 1, h * R_DV:(h + 1) * R_DV]
            s = s_in[j, h]
            qs = _dot(lhs, _b(s))[j:j + 1, :]
            o_ref[j:j + 1, h * R_DV:(h + 1) * R_DV] = gam * qs + qk[j:j + 1, :] * v_row
            s_out[j, h] = s * gam + kc * v_row
    for h in range(R_HEADS):
        o = o_ref[:, h * R_DV:(h + 1) * R_DV]
        mu = jnp.mean(o, -1, keepdims=True)
        var = jnp.mean(jnp.square(o - mu), -1, keepdims=True)
        g = vg[:, R_VAL + h * R_DV:R_VAL + (h + 1) * R_DV]
        act_ref[:, h * R_DV:(h + 1) * R_DV] = (o - mu) * lax.rsqrt(var + LN_EPS) * jax.nn.silu(g)


def _dec_ret(x, w, cos, sin, state, layer, prev):
    n = x.shape[0]
    w, wspec = _weight(w)
    consts = (x, w, cos, sin)
    sspec, prev, alias = _state_specs(state, prev, layer, (None, DEC_BT, R_HEADS, R_DK, R_DV))
    return pl.pallas_call(
        _dec_ret_kernel,
        out_shape=(jax.ShapeDtypeStruct((n, R_VAL), f32), jax.ShapeDtypeStruct(state.shape, f32)),
        grid=(n // DEC_BT,),
        in_specs=[_const_spec(x.shape), wspec, _const_spec(cos.shape), _const_spec(sin.shape),
                  sspec, pl.BlockSpec(memory_space=pl.ANY)],
        out_specs=(pl.BlockSpec((DEC_BT, R_VAL), lambda i: (i, 0)), sspec),
        scratch_shapes=[pltpu.VMEM((n, 2 * R_QK), f32),
                        pltpu.VMEM((R_QK, n), f32),
                        pltpu.VMEM((n, 2 * R_VAL), f32),
                        pltpu.VMEM((R_QK, n), f32),
                        pltpu.VMEM((DEC_BT, R_VAL), f32)],
        input_output_aliases={len(consts) + 1: 1} if alias is None else alias,
        compiler_params=_params(("arbitrary",)),
        name="dec_ret",
    )(*consts, state, prev)


def _dec_ssd_kernel(x_ref, w_ref, cst_ref, cw_ref, cb_ref, dtb_ref, alog_ref, dexp_ref, nw_ref, s_in, prev_ref,
                    act_ref, s_out, conv_out_ref,
                    xc_ref, z_ref, xdt_ref, xdtt_ref, ela_ref, elax_ref, xts_ref, e2_ref, o_ref):
    del prev_ref
    i = pl.program_id(0)

    @pl.when(i == 0)
    def _():
        _head_expander(e2_ref, M_HEADDIM)
        proj = _dot_nt(_b(x_ref[...]), w_ref[...])
        xc = _conv_step(cst_ref, proj[:, M_INNER:M_INNER + M_CONV_DIM], cw_ref, cb_ref, conv_out_ref)
        xc_ref[...] = xc
        z_ref[...] = proj[:, 0:M_INNER]
        dt = jax.nn.softplus(proj[:, M_INNER + M_CONV_DIM:M_INNER + M_CONV_DIM + PAD] + dtb_ref[...])
        ela = jnp.exp(-jnp.exp(alog_ref[...]) * dt)
        ela_ref[...] = ela
        elax_ref[...] = _expand_heads(ela, e2_ref)
        xdt = xc[:, 0:M_INNER] * _expand_heads(dt, e2_ref)
        xdt_ref[...] = xdt
        xdtt_ref[...] = xdt.T

    _token_lanes_to_front(xdtt_ref, xts_ref, i)
    rows = pl.ds(pl.multiple_of(i * DEC_BT, DEC_BT), DEC_BT)
    xc8, xdt8, ela8, elax8 = xc_ref[rows, :], xdt_ref[rows, :], ela_ref[rows, :], elax_ref[rows, :]
    b_off, c_off = M_INNER, M_INNER + M_GROUPS * M_STATE
    for g in range(M_GROUPS):
        gs = slice(g * M_GW, (g + 1) * M_GW)
        b8 = xc8[:, b_off + g * M_STATE:b_off + (g + 1) * M_STATE]
        c8 = xc8[:, c_off + g * M_STATE:c_off + (g + 1) * M_STATE]
        cb = jnp.sum(c8 * b8, axis=1, keepdims=True)
        lhs = _b(jnp.concatenate([c8, c8], axis=0))
        for j in range(DEC_BT):
            st = s_in[j, g * M_HPG:(g + 1) * M_HPG]
            cs = _dot_nt(lhs, _b(st.reshape(M_GW, M_STATE)))[j:j + 1, :]
            o_ref[j:j + 1, gs] = cb[j:j + 1, :] * xdt8[j:j + 1, gs] + elax8[j:j + 1, gs] * cs
            b_row = b8[j:j + 1, :]
            for hh in range(M_HPG):
                h = g * M_HPG + hh
                xdt_col = xts_ref[h * M_HEADDIM:(h + 1) * M_HEADDIM, j:j + 1]
                s_out[j, h] = st[hh] * ela8[j:j + 1, h:h + 1] + xdt_col * b_row
    y = (o_ref[...] + dexp_ref[...] * xc8[:, 0:M_INNER]) * jax.nn.silu(z_ref[rows, :])
    for g in range(M_GROUPS):
        gs = slice(g * M_GW, (g + 1) * M_GW)
        act_ref[:, gs] = _rms(y[:, gs]) * nw_ref[:, gs]


def _dec_ssd(x, w, cst, cw, cb, dtb, alog, dexp, nw, state, layer, prev):
    n = x.shape[0]
    w, wspec = _weight(w)
    consts = (x, w, cst, cw, cb, dtb, alog, dexp, nw)
    sspec, prev, alias = _state_specs(state, prev, layer, (None, DEC_BT, M_HEADS, M_HEADDIM, M_STATE))
    cspecs = [_const_spec(a.shape) for a in consts]
    cspecs[1] = wspec
    cspecs[2] = pl.BlockSpec((None,) + cst.shape[1:], lambda i: (layer, 0, 0, 0), pipeline_mode=pl.Buffered(1))
    return pl.pallas_call(
        _dec_ssd_kernel,
        out_shape=(jax.ShapeDtypeStruct((n, M_INNER), f32), jax.ShapeDtypeStruct(state.shape, f32),
                   jax.ShapeDtypeStruct(cst.shape[1:], f32)),
        grid=(n // DEC_BT,),
        in_specs=cspecs + [sspec, pl.BlockSpec(memory_space=pl.ANY)],
        out_specs=(pl.BlockSpec((DEC_BT, M_INNER), lambda i: (i, 0)), sspec,
                   pl.BlockSpec(cst.shape[1:], lambda i: (0, 0, 0))),
        scratch_shapes=[pltpu.VMEM((n, M_CONV_DIM), f32),
                        pltpu.VMEM((n, M_INNER), f32),
                        pltpu.VMEM((n, M_INNER), f32),
                        pltpu.VMEM((M_INNER, n), f32),
                        pltpu.VMEM((n, PAD), f32),
                        pltpu.VMEM((n, M_INNER), f32),
                        pltpu.VMEM((M_INNER, n), f32),
                        pltpu.VMEM((2 * PAD, M_INNER), bf16),
                        pltpu.VMEM((DEC_BT, M_INNER), f32)],
        input_output_aliases={len(consts) + 1: 1} if alias is None else alias,
        compiler_params=_params(("arbitrary",)),
        name="dec_ssd",
    )(*consts, state, prev)


def _dec_gdn_kernel(x_ref, w_ref, cst_ref, cw_ref, dtb_ref, alog_ref, nw_ref, s_in, prev_ref,
                    act_ref, s_out, conv_out_ref,
                    qkv_ref, kt_ref, gz_ref, eg_ref, beta_ref, ks_ref, o_ref):
    del prev_ref
    i = pl.program_id(0)

    @pl.when(i == 0)
    def _():
        proj = _dot_nt(_b(x_ref[...]), w_ref[...])
        qkv = _conv_step(cst_ref, proj[:, 0:G_QKV], cw_ref, None, conv_out_ref)
        for h in range(G_HEADS):
            hs = slice(h * G_DK, (h + 1) * G_DK)
            q = qkv[:, hs]
            qkv_ref[:, hs] = q * lax.rsqrt(jnp.sum(jnp.square(q), -1, keepdims=True) + NORM_EPS) * (G_DK ** -0.5)
            k = qkv[:, G_KEY + h * G_DK:G_KEY + (h + 1) * G_DK]
            k = k * lax.rsqrt(jnp.sum(jnp.square(k), -1, keepdims=True) + NORM_EPS)
            qkv_ref[:, G_KEY + h * G_DK:G_KEY + (h + 1) * G_DK] = k
            kt_ref[hs, :] = k.T
        qkv_ref[:, 2 * G_KEY:] = qkv[:, 2 * G_KEY:]
        gz_ref[...] = proj[:, G_QKV:G_QKV + G_VAL]
        ab = proj[:, G_QKV + G_VAL:G_QKV + G_VAL + PAD]
        eg_ref[...] = jnp.exp(-jnp.exp(alog_ref[...]) * jax.nn.softplus(ab + dtb_ref[...]))
        beta_ref[...] = jax.nn.sigmoid(ab)

    _token_lanes_to_front(kt_ref, ks_ref, i)
    rows = pl.ds(pl.multiple_of(i * DEC_BT, DEC_BT), DEC_BT)
    qkv8, eg8, beta8 = qkv_ref[rows, :], eg_ref[rows, :], beta_ref[rows, :]
    for h in range(G_HEADS):
        hs = slice(h * G_DV, (h + 1) * G_DV)
        q8 = qkv8[:, h * G_DK:(h + 1) * G_DK]
        k8 = qkv8[:, G_KEY + h * G_DK:G_KEY + (h + 1) * G_DK]
        v8 = qkv8[:, 2 * G_KEY + h * G_DV:2 * G_KEY + (h + 1) * G_DV]
        qk = jnp.sum(q8 * k8, axis=1, keepdims=True)
        lhs = _b(jnp.concatenate([q8, k8], axis=0))
        bh = beta8[:, G_HEADS + h:G_HEADS + h + 1]
        eg = eg8[:, h:h + 1]
        for j in range(DEC_BT):
            kc = ks_ref[h * G_DK:(h + 1) * G_DK, j:j + 1]
            s = s_in[j, h]
            qks = _dot(lhs, _b(s))
            bj, ej = bh[j:j + 1, :], eg[j:j + 1, :]
            u = v8[j:j + 1, :] * bj - (bj * ej) * qks[DEC_BT + j:DEC_BT + j + 1, :]
            o_ref[j:j + 1, hs] = ej * qks[j:j + 1, :] + qk[j:j + 1, :] * u
            s_out[j, h] = s * ej + kc * u
    gz8 = gz_ref[rows, :]
    for h in range(G_HEADS):
        hs = slice(h * G_DV, (h + 1) * G_DV)
        act_ref[:, hs] = _rms(o_ref[:, hs]) * nw_ref[...] * jax.nn.silu(gz8[:, hs])


def _dec_gdn(x, w, cst, cw, dtb, alog, nw, state, layer, prev):
    n = x.shape[0]
    consts = (x, w, cst, cw, dtb, alog, nw)
    sspec, prev, alias = _state_specs(state, prev, layer, (None, DEC_BT, G_HEADS, G_DK, G_DV))
    cspecs = [_const_spec(a.shape) for a in consts]
    cspecs[2] = pl.BlockSpec((None,) + cst.shape[1:], lambda i: (layer, 0, 0, 0), pipeline_mode=pl.Buffered(1))
    return pl.pallas_call(
        _dec_gdn_kernel,
        out_shape=(jax.ShapeDtypeStruct((n, G_VAL), f32), jax.ShapeDtypeStruct(state.shape, f32),
                   jax.ShapeDtypeStruct(cst.shape[1:], f32)),
        grid=(n // DEC_BT,),
        in_specs=cspecs + [sspec, pl.BlockSpec(memory_space=pl.ANY)],
        out_specs=(pl.BlockSpec((DEC_BT, G_VAL), lambda i: (i, 0)), sspec,
                   pl.BlockSpec(cst.shape[1:], lambda i: (0, 0, 0))),
        scratch_shapes=[pltpu.VMEM((n, G_QKV), f32),
                        pltpu.VMEM((G_KEY, n), f32),
                        pltpu.VMEM((n, G_VAL), f32),
                        pltpu.VMEM((n, PAD), f32), pltpu.VMEM((n, PAD), f32),
                        pltpu.VMEM((G_KEY, n), f32),
                        pltpu.VMEM((DEC_BT, G_VAL), f32)],
        input_output_aliases={len(consts) + 1: 1} if alias is None else alias,
        compiler_params=_params(("arbitrary",)),
        name="dec_gdn",
    )(*consts, state, prev)


def _rope_tables(pos):
    half = R_DK // 2
    inv = ROPE_BASE ** (-jnp.arange(half, dtype=f32) / half)
    ang = pos[:, None] * inv[None, :]
    cos, sin = jnp.cos(ang), jnp.sin(ang)
    return jnp.concatenate([cos, cos], axis=1), jnp.concatenate([-sin, sin], axis=1)


def _lane_pad(v, start=0):
    return jnp.zeros((1, PAD), f32).at[0, start:start + v.shape[0]].set(v)


def _stacked_weights(prm):
    return {k: _b(prm[k]) for k in ("ffn_wg", "ffn_wu", "ffn_wd", "w_ret_out", "w_ssm_out", "w_gdn_out", "w_o",
                                    "pe_proj", "pe_gate")}


def _layer_weights(i, prm):
    return dict(
        ln_g=prm["ln_g"][i], ln_b=prm["ln_b"][i],
        ssm_cw=prm["ssm_conv_w"][i], ssm_cb=prm["ssm_conv_b"][i][None, :],
        ssm_dtb=_lane_pad(prm["ssm_dt_bias"][i]), ssm_alog=_lane_pad(prm["ssm_a_log"][i]),
        ssm_dexp=jnp.repeat(prm["ssm_d"][i], M_HEADDIM)[None, :], ssm_nw=prm["ssm_norm_w"][i][None, :],
        gdn_cw=prm["gdn_conv_w"][i],
        gdn_dtb=_lane_pad(prm["gdn_dt_bias"][i]), gdn_alog=_lane_pad(prm["gdn_a_log"][i]),
        gdn_nw=prm["gdn_norm_w"][i][None, :],
    )


def _post_mix(x1, acts, p, i, w, sw, tm):
    x2 = _merge(x1, *acts, w["w_merge"], sw["w_ret_out"], sw["w_ssm_out"], sw["w_gdn_out"], sw["w_o"],
                w["ln_g"], w["ln_b"], i, tm)
    return _ffn_pe(x2, p, i, sw["ffn_wg"], sw["ffn_wu"], sw["ffn_wd"], sw["pe_gate"], sw["pe_proj"],
                   w["ln_g"], w["ln_b"], tm)


def kernel(x_prompt, x_sample, state_ret, state_ssm, state_ssm_conv, state_gdn, state_gdn_conv,
           p_prompt, p_sample, ln_g, ln_b, ffn_wg, ffn_wu, ffn_wd, w_in,
           ssm_conv_w, ssm_conv_b, ssm_dt_bias, ssm_a_log, ssm_d, ssm_norm_w,
           gdn_conv_w, gdn_dt_bias, gdn_a_log, gdn_norm_w,
           w_ret_out, w_ssm_out, w_gdn_out, w_o, pe_proj, pe_gate):
    prm = dict(ln_g=ln_g, ln_b=ln_b, ffn_wg=ffn_wg, ffn_wu=ffn_wu, ffn_wd=ffn_wd, w_in=w_in,
               ssm_conv_w=ssm_conv_w, ssm_conv_b=ssm_conv_b, ssm_dt_bias=ssm_dt_bias,
               ssm_a_log=ssm_a_log, ssm_d=ssm_d, ssm_norm_w=ssm_norm_w,
               gdn_conv_w=gdn_conv_w, gdn_dt_bias=gdn_dt_bias, gdn_a_log=gdn_a_log,
               gdn_norm_w=gdn_norm_w, w_ret_out=w_ret_out, w_ssm_out=w_ssm_out,
               w_gdn_out=w_gdn_out, w_o=w_o, pe_proj=pe_proj, pe_gate=pe_gate)
    nb, seq, _ = x_prompt.shape
    ns = x_sample.shape[0]
    depth = w_in.shape[0]
    tl = min(TL_SCAN, seq)
    tm = min(TM_DENSE, nb * seq)
    chunk = CHUNK if seq % CHUNK == 0 else seq

    cos_p, sin_p = _rope_tables(jnp.arange(seq, dtype=f32))
    cos_s, sin_s = _rope_tables(jnp.full((1,), PAST_LEN, f32))
    xp = x_prompt.reshape(nb * seq, D_MODEL)
    xs = x_sample.reshape(ns, D_MODEL)
    pp = p_prompt.reshape(depth, nb * seq, PLE_DIM)
    ps = p_sample.reshape(depth, ns, PLE_DIM)
    ssm_state_t = jnp.swapaxes(state_ssm, 3, 4)
    ssm_conv_t = jnp.transpose(state_ssm_conv, (0, 2, 1, 3))
    gdn_conv_t = jnp.transpose(state_gdn_conv, (0, 2, 1, 3))

    sw = _stacked_weights(prm)
    w_in_t = _b(jnp.swapaxes(w_in, 1, 2))
    gdn_rows = G_QKV + G_VAL + PAD
    assert OFF_RET == 0 and OFF_SSD == W_IN_BLK and OFF_SSD + M_INNER + M_CONV_DIM + PAD <= 2 * W_IN_BLK
    assert OFF_GDN + gdn_rows <= IN_DIM
    prompt_states, sample_convs = [], []
    t_r = t_s = t_g = None
    for i in range(depth):
        w = _layer_weights(i, prm)
        w["w_ret"], w["w_ssd"] = _w_in_block(w_in_t, i, 0), _w_in_block(w_in_t, i, 1)
        w["w_gdn"] = w_in_t[i, OFF_GDN:OFF_GDN + gdn_rows]
        w["w_merge"] = w_in_t[i, OFF_MERGE:]
        x1 = _ffn_ln(xp, sw["ffn_wg"], sw["ffn_wu"], sw["ffn_wd"], w["ln_g"], w["ln_b"], i, tm)
        a_r, s_r = _ret_scan(x1, w["w_ret"], cos_p, sin_p, nb, seq, tl, RET_CHUNK if seq % RET_CHUNK == 0 else chunk)
        a_s, s_s, c_s = _ssd_scan(x1, w["w_ssd"], w["ssm_cw"], w["ssm_cb"], w["ssm_dtb"], w["ssm_alog"],
                                  w["ssm_dexp"], w["ssm_nw"], nb, seq, tl, chunk)
        a_g, s_g, c_g = _gdn_scan(x1, w["w_gdn"], w["gdn_cw"], w["gdn_dtb"], w["gdn_alog"], w["gdn_nw"],
                                  nb, seq, tl, chunk)
        xp = _post_mix(x1, (a_r, a_s, a_g), pp, i, w, sw, tm)
        prompt_states.append((s_r, s_s, c_s, s_g, c_g))
        y1 = _ffn_ln(xs, sw["ffn_wg"], sw["ffn_wu"], sw["ffn_wd"], w["ln_g"], w["ln_b"], i, ns)
        b_r, t_r = _dec_ret(y1, w["w_ret"], cos_s, sin_s, state_ret, i, t_r)
        b_s, t_s, d_s = _dec_ssd(y1, w["w_ssd"], ssm_conv_t, w["ssm_cw"], w["ssm_cb"], w["ssm_dtb"],
                                 w["ssm_alog"], w["ssm_dexp"], w["ssm_nw"], ssm_state_t, i, t_s)
        b_g, t_g, d_g = _dec_gdn(y1, w["w_gdn"], gdn_conv_t, w["gdn_cw"], w["gdn_dtb"], w["gdn_alog"],
                                 w["gdn_nw"], state_gdn, i, t_g)
        xs = _post_mix(y1, (b_r, b_s, b_g), ps, i, w, sw, ns)
        sample_convs.append((d_s, d_g))

    r_p, s_p, sc_p, g_p, gc_p = (jnp.stack([s[j] for s in prompt_states]) for j in range(5))
    sc_s, gc_s = (jnp.transpose(jnp.stack([c[j] for c in sample_convs]), (0, 2, 1, 3)) for j in range(2))
    return (xp.reshape(nb, seq, D_MODEL), xs.reshape(ns, 1, D_MODEL),
            r_p, jnp.swapaxes(s_p, 3, 4), sc_p, g_p, gc_p,
            t_r, jnp.swapaxes(t_s, 3, 4), sc_s, t_g, gc_s)
```

```python
import functools
import math

import numpy as np
import jax
import jax.numpy as jnp
from jax import lax
from jax.experimental import pallas as pl
from jax.experimental.pallas import tpu as pltpu

f32, bf16 = jnp.float32, jnp.bfloat16

D_MODEL = 1024
DEPTH = 2
PAST_LEN = 16384
R_HEADS, R_DK, R_DV = 4, 128, 256
R_QK, R_VAL = R_HEADS * R_DK, R_HEADS * R_DV
ROPE_BASE = 10000.0
M_HEADS, M_HEADDIM, M_GROUPS, M_STATE = 16, 64, 2, 128
M_INNER = M_HEADS * M_HEADDIM
M_CONV_DIM = M_INNER + 2 * M_GROUPS * M_STATE
M_HPG = M_HEADS // M_GROUPS
M_GW = M_HPG * M_HEADDIM
G_HEADS, G_DK, G_DV = 8, 128, 128
G_KEY, G_VAL = G_HEADS * G_DK, G_HEADS * G_DV
G_QKV = 2 * G_KEY + G_VAL
CONV_W = 4
FFN_DIM = 2048
PLE_DIM = 256
DN_ALPHA = (2 * DEPTH) ** 0.25
LN_EPS = 1e-5
NORM_EPS = 1e-6

_sizes = (R_QK, R_QK, R_VAL, R_VAL, M_INNER, M_CONV_DIM, M_HEADS, G_QKV, G_VAL, G_HEADS, G_HEADS,
          D_MODEL, D_MODEL, D_MODEL)
_off = np.concatenate([[0], np.cumsum(_sizes)]).tolist()
OFF_RET, OFF_SSD, OFF_GDN, OFF_MERGE, IN_DIM = _off[0], _off[4], _off[7], _off[11], _off[14]

LANES = 128
SUBLANES = 8
VMEM_LIMIT = 56 * 2 ** 20

TM_DENSE = 512
TL_SCAN = 512
CHUNK = 64
FFN_CHUNK = 512
DEC_BT = 8
RET_CHUNK = 128
RET_UNROLL = 2
SSD_UNROLL = 4
GDN_PRE_UNROLL = 4
GDN_SCAN_UNROLL = 4
PAD = LANES

LOG_GAMMA = [math.log1p(-(2.0 ** (-5.0 - h))) for h in range(R_HEADS)]


def _dot(a, b):
    return jnp.dot(a, b, preferred_element_type=f32)


def _dot_nt(a, b):
    return lax.dot_general(a, b, (((1,), (1,)), ((), ())), preferred_element_type=f32)


def _dot_tn(a, b):
    return lax.dot_general(a, b, (((0,), (0,)), ((), ())), preferred_element_type=f32)


def _b(x):
    return x.astype(bf16)


def _layer_norm(y, g, b):
    mu = jnp.mean(y, -1, keepdims=True)
    var = jnp.mean(jnp.square(y - mu), -1, keepdims=True)
    return (y - mu) * lax.rsqrt(var + LN_EPS) * g + b


def _rms(y):
    return y * lax.rsqrt(jnp.mean(jnp.square(y), -1, keepdims=True) + NORM_EPS)


def _const_spec(shape):
    return pl.BlockSpec(shape, lambda *_: (0,) * len(shape), pipeline_mode=pl.Buffered(1))


def _layer_spec(arr, *lead):
    blk = (None,) * len(lead) + arr.shape[len(lead):]
    tail = (0,) * (arr.ndim - len(lead))
    return pl.BlockSpec(blk, lambda *_: tuple(lead) + tail, pipeline_mode=pl.Buffered(1))


W_IN_BLK = 3072


def _w_in_block(wt_all, layer, blk):
    return wt_all, pl.BlockSpec((None, W_IN_BLK, D_MODEL), lambda *_: (layer, blk, 0), pipeline_mode=pl.Buffered(1))


def _weight(w):
    return w if isinstance(w, tuple) else (w, _const_spec(w.shape))


def _params(sem):
    return pltpu.CompilerParams(dimension_semantics=sem, vmem_limit_bytes=VMEM_LIMIT)


def _split3(x):
    a1 = _b(x)
    r1 = x - a1.astype(f32)
    a2 = _b(r1)
    a3 = _b(r1 - a2.astype(f32))
    return a1, a2, a3


def _chunk_iotas(c):
    ii = lax.broadcasted_iota(jnp.int32, (c, c), 0)
    jj = lax.broadcasted_iota(jnp.int32, (c, c), 1)
    return ii, jj


def _swiglu(x, wg_ref, wu_ref, wd_ref):
    xb = _b(x)
    acc = None
    for c in range(FFN_DIM // FFN_CHUNK):
        sl = slice(c * FFN_CHUNK, (c + 1) * FFN_CHUNK)
        a = jax.nn.silu(_dot(xb, wg_ref[:, sl])) * _dot(xb, wu_ref[:, sl])
        part = _dot(_b(a), wd_ref[sl, :])
        acc = part if acc is None else acc + part
    return acc


def _ffn_ln_kernel(x_ref, wg_ref, wu_ref, wd_ref, g_ref, b_ref, o_ref):
    x = x_ref[...]
    y = DN_ALPHA * x + 0.5 * _swiglu(x, wg_ref, wu_ref, wd_ref)
    o_ref[...] = _layer_norm(y, g_ref[0:1, :], b_ref[0:1, :])


def _ffn_ln(x, wg, wu, wd, g, b, layer, tm):
    n = x.shape[0]
    return pl.pallas_call(
        _ffn_ln_kernel,
        out_shape=jax.ShapeDtypeStruct((n, D_MODEL), f32),
        grid=(n // tm,),
        in_specs=[pl.BlockSpec((tm, D_MODEL), lambda i: (i, 0)),
                  _layer_spec(wg, layer, 0), _layer_spec(wu, layer, 0), _layer_spec(wd, layer, 0),
                  _const_spec(g.shape), _const_spec(b.shape)],
        out_specs=pl.BlockSpec((tm, D_MODEL), lambda i: (i, 0)),
        compiler_params=_params(("parallel",)),
        name="ffn_ln",
    )(x, wg, wu, wd, g, b)


def _merge_kernel(x_ref, ar_ref, as_ref, ag_ref, wm_ref, wr_ref, ws_ref, wgd_ref, wo_ref, g_ref, b_ref, o_ref):
    x = x_ref[...]
    m = _dot_nt(_b(x), wm_ref[...])
    yr = _dot(_b(ar_ref[...]), wr_ref[...])
    ys = _dot(_b(as_ref[...]), ws_ref[...])
    yg = _dot(_b(ag_ref[...]), wgd_ref[...])
    mixed = (jax.nn.sigmoid(m[:, 0:D_MODEL]) * yr + jax.nn.sigmoid(m[:, D_MODEL:2 * D_MODEL]) * ys
             + jax.nn.sigmoid(m[:, 2 * D_MODEL:3 * D_MODEL]) * yg)
    y = DN_ALPHA * x + _dot(_b(mixed), wo_ref[...])
    o_ref[...] = _layer_norm(y, g_ref[1:2, :], b_ref[1:2, :])


def _merge(x, ar, a_s, ag, wm, wr, ws, wgd, wo, g, b, layer, tm):
    n = x.shape[0]
    tok = lambda i: (i, 0)
    return pl.pallas_call(
        _merge_kernel,
        out_shape=jax.ShapeDtypeStruct((n, D_MODEL), f32),
        grid=(n // tm,),
        in_specs=[pl.BlockSpec((tm, D_MODEL), tok)] * 4
                 + [_const_spec(wm.shape)] + [_layer_spec(w, layer) for w in (wr, ws, wgd, wo)]
                 + [_const_spec(g.shape), _const_spec(b.shape)],
        out_specs=pl.BlockSpec((tm, D_MODEL), tok),
        compiler_params=_params(("parallel",)),
        name="merge",
    )(x, ar, a_s, ag, wm, wr, ws, wgd, wo, g, b)


def _ffn_pe_kernel(x_ref, p_ref, wg_ref, wu_ref, wd_ref, pg_ref, pp_ref, g_ref, b_ref, o_ref):
    x = x_ref[...]
    x = _layer_norm(DN_ALPHA * x + 0.5 * _swiglu(x, wg_ref, wu_ref, wd_ref), g_ref[2:3, :], b_ref[2:3, :])
    pe = jax.nn.sigmoid(_dot(_b(x), pg_ref[...])) * _dot(_b(p_ref[...]), pp_ref[...])
    o_ref[...] = _layer_norm(DN_ALPHA * x + pe, g_ref[3:4, :], b_ref[3:4, :])


def _ffn_pe(x, p, layer, wg, wu, wd, pg, pp, g, b, tm):
    n = x.shape[0]
    return pl.pallas_call(
        _ffn_pe_kernel,
        out_shape=jax.ShapeDtypeStruct((n, D_MODEL), f32),
        grid=(n // tm,),
        in_specs=[pl.BlockSpec((tm, D_MODEL), lambda i: (i, 0)),
                  pl.BlockSpec((None, tm, PLE_DIM), lambda i: (layer, i, 0)),
                  _layer_spec(wg, layer, 1), _layer_spec(wu, layer, 1), _layer_spec(wd, layer, 1),
                  _layer_spec(pg, layer), _layer_spec(pp, layer),
                  _const_spec(g.shape), _const_spec(b.shape)],
        out_specs=pl.BlockSpec((tm, D_MODEL), lambda i: (i, 0)),
        compiler_params=_params(("parallel",)),
        name="ffn_pe",
    )(x, p, wg, wu, wd, pg, pp, g, b)


def _rope_inplace(proj_ref, off, cos, sin, scale):
    t = proj_ref[:, off:off + R_DK]
    t = t * cos + pltpu.roll(t, R_DK // 2, axis=1) * sin
    if scale != 1.0:
        t = t * scale
    proj_ref[:, off:off + R_DK] = t


def _ret_scan_kernel(x_ref, w_ref, cos_ref, sin_ref, act_ref, st_ref, proj_ref, s_ref, *, tl, c):
    l = pl.program_id(1)

    @pl.when(l == 0)
    def _():
        s_ref[...] = jnp.zeros_like(s_ref)

    proj_ref[...] = _dot_nt(_b(x_ref[...]), w_ref[...])
    cos, sin = cos_ref[...], sin_ref[...]
    for h in range(R_HEADS):
        _rope_inplace(proj_ref, h * R_DK, cos, sin, 1.0)
        _rope_inplace(proj_ref, R_QK + h * R_DK, cos, sin, R_DK ** -0.5)

    ii, jj = _chunk_iotas(c)
    dif = (ii - jj).astype(f32)
    ci = lax.broadcasted_iota(jnp.int32, (c, 1), 0).astype(f32)
    decay = [jnp.where(dif >= 0, jnp.exp(dif * lg), 0.0) for lg in LOG_GAMMA]
    e_col = [jnp.exp((ci + 1.0) * lg) for lg in LOG_GAMMA]
    w_col = [jnp.exp((c - 1.0 - ci) * lg) for lg in LOG_GAMMA]

    def chunk(ck):
        rows = pl.ds(pl.multiple_of(ck * c, c), c)
        hd = range(R_HEADS)
        q = [proj_ref[rows, h * R_DK:(h + 1) * R_DK] for h in hd]
        k = [proj_ref[rows, R_QK + h * R_DK:R_QK + (h + 1) * R_DK] for h in hd]
        v = [_b(proj_ref[rows, 2 * R_QK + h * R_DV:2 * R_QK + (h + 1) * R_DV]) for h in hd]
        s = [s_ref[h] for h in hd]
        scores = [_dot_nt(_b(q[h]), _b(k[h])) * decay[h] for h in hd]
        inter = [_dot(_b(q[h] * e_col[h]), _b(s[h])) for h in hd]
        for h in hd:
            s_ref[h] = s[h] * math.exp(c * LOG_GAMMA[h]) + _dot_tn(_b(k[h] * w_col[h]), v[h])
        o = [_dot(_b(scores[h]), v[h]) + inter[h] for h in hd]
        for h in hd:
            mu = jnp.mean(o[h], -1, keepdims=True)
            var = jnp.mean(jnp.square(o[h] - mu), -1, keepdims=True)
            on = (o[h] - mu) * lax.rsqrt(var + LN_EPS)
            g = proj_ref[rows, 2 * R_QK + R_VAL + h * R_DV:2 * R_QK + R_VAL + (h + 1) * R_DV]
            act_ref[rows, h * R_DV:(h + 1) * R_DV] = _b(on * jax.nn.silu(g))

    def chunks(it, carry):
        for j in range(RET_UNROLL):
            chunk(it * RET_UNROLL + j)
        return carry

    lax.fori_loop(0, tl // (c * RET_UNROLL), chunks, 0)

    @pl.when(l == pl.num_programs(1) - 1)
    def _():
        st_ref[0] = s_ref[...]


def _ret_scan(x, w, cos, sin, nb, nl_tok, tl, c):
    w, wspec = _weight(w)
    nl = nl_tok // tl
    tok = lambda b, l: (b * nl + l, 0)
    return pl.pallas_call(
        functools.partial(_ret_scan_kernel, tl=tl, c=c),
        out_shape=(jax.ShapeDtypeStruct((nb * nl_tok, R_VAL), bf16),
                   jax.ShapeDtypeStruct((nb, R_HEADS, R_DK, R_DV), f32)),
        grid=(nb, nl),
        in_specs=[pl.BlockSpec((tl, D_MODEL), tok), wspec,
                  pl.BlockSpec((tl, R_DK), lambda b, l: (l, 0)),
                  pl.BlockSpec((tl, R_DK), lambda b, l: (l, 0))],
        out_specs=(pl.BlockSpec((tl, R_VAL), tok),
                   pl.BlockSpec((1, R_HEADS, R_DK, R_DV), lambda b, l: (b, 0, 0, 0))),
        scratch_shapes=[pltpu.VMEM((tl, 2 * R_QK + 2 * R_VAL), f32),
                        pltpu.VMEM((R_HEADS, R_DK, R_DV), f32)],
        compiler_params=_params(("parallel", "arbitrary")),
        name="ret_scan",
    )(x, w, cos, sin)


CONV_BLK = 512


def _project_and_conv(xb, w_ref, w_off, width, xbuf_ref, cw_ref, cb_ref, dst_ref, tl, first):
    @pl.when(first)
    def _():
        xbuf_ref[0:SUBLANES, :] = jnp.zeros((SUBLANES, width), f32)

    def project(n):
        cs = slice(n * CONV_BLK, (n + 1) * CONV_BLK)
        xbuf_ref[SUBLANES:SUBLANES + tl, cs] = _dot_nt(xb, w_ref[w_off + cs.start:w_off + cs.stop, :])

    def conv(n):
        cs = slice(n * CONV_BLK, (n + 1) * CONV_BLK)
        acc = xbuf_ref[SUBLANES:SUBLANES + tl, cs] * cw_ref[CONV_W - 1:CONV_W, cs]
        for j in range(CONV_W - 1):
            r0 = SUBLANES - (CONV_W - 1) + j
            acc = acc + xbuf_ref[r0:r0 + tl, cs] * cw_ref[j:j + 1, cs]
        if cb_ref is not None:
            acc = acc + cb_ref[:, cs]
        dst_ref[:, cs] = jax.nn.silu(acc)
        xbuf_ref[0:SUBLANES, cs] = xbuf_ref[tl:tl + SUBLANES, cs]

    nblk = width // CONV_BLK
    project(0)
    for n in range(nblk):
        if n + 1 < nblk:
            project(n + 1)
        conv(n)


def _head_expander(e2_ref, width):
    er = lax.broadcasted_iota(jnp.int32, e2_ref.shape, 0)
    el = lax.broadcasted_iota(jnp.int32, e2_ref.shape, 1)
    e2_ref[...] = _b(((er & (PAD - 1)) == (el >> (width.bit_length() - 1))).astype(f32))


def _chunk_block_mask(tri_ref, tl, c):
    log2c = c.bit_length() - 1
    ti = lax.broadcasted_iota(jnp.int32, (tl, tl), 0)
    tj = lax.broadcasted_iota(jnp.int32, (tl, tl), 1)
    tri_ref[...] = _b(((ti >= tj) & ((ti >> log2c) == (tj >> log2c))).astype(f32))


def _chunk_cumsum(tri_ref, cum_ref, la, c):
    a1, a2, a3 = _split3(la)
    cum = _dot(tri_ref[...], a1) + _dot(tri_ref[...], a2) + _dot(tri_ref[...], a3)
    cum_ref[...] = cum
    tl = la.shape[0]
    tot = jnp.concatenate([jnp.broadcast_to(cum_ref[k * c + c - 1:k * c + c, :], (c, la.shape[1]))
                           for k in range(tl // c)], axis=0)
    return cum, tot


def _expand_heads(v, e2_ref):
    hi = _b(v)
    lo = _b(v - hi.astype(f32))
    return _dot(jnp.concatenate([hi, lo], axis=1), e2_ref[...])


def _expand_heads_exact(v, e3_ref):
    return _dot(jnp.concatenate(_split3(v), axis=1), e3_ref[...])


def _ssd_scan_kernel(x_ref, w_ref, cw_ref, cb_ref, dtb_ref, alog_ref, dexp_ref, nw_ref,
                     act_ref, st_ref, conv_ref,
                     z_ref, xbuf_ref, xc_ref, cum_ref, cumx_ref, xdt_ref, xw_ref, ee_ref,
                     tri_ref, e3_ref, s_ref, *, tl, c):
    l = pl.program_id(1)

    @pl.when(l == 0)
    def _():
        s_ref[...] = jnp.zeros_like(s_ref)
        _chunk_block_mask(tri_ref, tl, c)
        _head_expander(e3_ref, M_HEADDIM)

    xb = _b(x_ref[...])
    dt = jax.nn.softplus(_dot_nt(xb, w_ref[M_INNER + M_CONV_DIM:M_INNER + M_CONV_DIM + PAD, :]) + dtb_ref[...])
    _project_and_conv(xb, w_ref, M_INNER, M_CONV_DIM, xbuf_ref, cw_ref, cb_ref, xc_ref, tl, l == 0)
    z_ref[...] = _dot_nt(xb, w_ref[0:M_INNER, :])
    la = -jnp.exp(alog_ref[...]) * dt
    cum, tot = _chunk_cumsum(tri_ref, cum_ref, la, c)
    cumx_ref[...] = _expand_heads_exact(cum, e3_ref)
    e2_ref = e3_ref.at[0:2 * PAD, :]
    xdt = xc_ref[:, 0:M_INNER] * _expand_heads(dt, e2_ref)
    xdt_ref[...] = xdt
    xw_ref[...] = xdt * _expand_heads(jnp.exp(tot - cum), e2_ref)
    ee_ref[...] = _expand_heads(jnp.exp(cum), e2_ref)

    half = c
    lane = lax.broadcasted_iota(jnp.int32, (c, 2 * half), 1)
    rowi = lax.broadcasted_iota(jnp.int32, (c, 2 * half), 0)
    left = lane < half
    causal2 = (lane & (half - 1)) <= rowi
    b_off, c_off = M_INNER, M_INNER + M_GROUPS * M_STATE
    gr = range(M_GROUPS)
    pairs = [(g, pp) for g in gr for pp in range(M_HPG // 2)]

    def chunk(ck):
        r0 = pl.multiple_of(ck * c, c)
        rows = pl.ds(r0, c)
        cum_c = cum_ref[rows, :]
        cum_t = jnp.concatenate([cum_c, cum_c], axis=0).T
        bb = [_b(xc_ref[rows, b_off + g * M_STATE:b_off + (g + 1) * M_STATE]) for g in gr]
        cb = [_b(xc_ref[rows, c_off + g * M_STATE:c_off + (g + 1) * M_STATE]) for g in gr]
        g2 = [_dot_nt(cb[g], jnp.concatenate([bb[g], bb[g]], axis=0)) for g in gr]
        sg = [s_ref[g] for g in gr]
        inter = [ee_ref[rows, g * M_GW:(g + 1) * M_GW] * _dot(cb[g], _b(sg[g])) for g in gr]
        for g in gr:
            e_last = ee_ref[pl.ds(r0 + c - 1, 1), g * M_GW:(g + 1) * M_GW]
            s_ref[g] = sg[g] * e_last + _dot_tn(bb[g], _b(xw_ref[rows, g * M_GW:(g + 1) * M_GW]))
        a2s, rhs = [], []
        for g, pp in pairs:
            h0 = g * M_HPG + 2 * pp
            ls = slice(h0 * M_HEADDIM, (h0 + 2) * M_HEADDIM)
            colsel = cumx_ref[rows, ls]
            rowsel = jnp.where(left[0:1, :], cum_t[h0:h0 + 1, :], cum_t[h0 + 1:h0 + 2, :])
            d2 = jnp.where(causal2, jnp.exp(colsel - rowsel), 0.0)
            a2s.append(_b(g2[g] * d2))
            xp = xdt_ref[rows, ls]
            rhs.append(_b(jnp.concatenate([jnp.where(left, xp, 0.0), jnp.where(left, 0.0, xp)], axis=0)))
        intra = [_dot(a, r) for a, r in zip(a2s, rhs)]
        ys = []
        for n, (g, pp) in enumerate(pairs):
            h0 = g * M_HPG + 2 * pp
            ls = slice(h0 * M_HEADDIM, (h0 + 2) * M_HEADDIM)
            y = intra[n] + inter[g][:, pp * 2 * M_HEADDIM:(pp + 1) * 2 * M_HEADDIM] + dexp_ref[:, ls] * xc_ref[rows, ls]
            ys.append(y * jax.nn.silu(z_ref[rows, ls]))
        for g in gr:
            mine = [n for n, (gg, _) in enumerate(pairs) if gg == g]
            ms = sum(jnp.sum(jnp.square(ys[n]), -1, keepdims=True) for n in mine) * (1.0 / M_GW)
            r = lax.rsqrt(ms + NORM_EPS)
            for n in mine:
                h0 = g * M_HPG + 2 * pairs[n][1]
                ls = slice(h0 * M_HEADDIM, (h0 + 2) * M_HEADDIM)
                act_ref[rows, ls] = _b(ys[n] * r * nw_ref[:, ls])

    def chunks(it, carry):
        for j in range(SSD_UNROLL):
            chunk(it * SSD_UNROLL + j)
        return carry

    lax.fori_loop(0, tl // (c * SSD_UNROLL), chunks, 0)

    @pl.when(l == pl.num_programs(1) - 1)
    def _():
        for h in range(M_HEADS):
            g, hh = divmod(h, M_HPG)
            st_ref[0, h] = s_ref[g][:, hh * M_HEADDIM:(hh + 1) * M_HEADDIM].T
        conv_ref[0] = xbuf_ref[SUBLANES - (CONV_W - 1):SUBLANES, :]


def _ssd_scan(x, w, cw, cb, dtb, alog, dexp, nw, nb, nl_tok, tl, c):
    assert 2 * c == LANES and 2 * M_HEADDIM == LANES, "head pairs are packed into one 128-lane slab"
    w, wspec = _weight(w)
    nl = nl_tok // tl
    tok = lambda b, l: (b * nl + l, 0)
    return pl.pallas_call(
        functools.partial(_ssd_scan_kernel, tl=tl, c=c),
        out_shape=(jax.ShapeDtypeStruct((nb * nl_tok, M_INNER), bf16),
                   jax.ShapeDtypeStruct((nb, M_HEADS, M_HEADDIM, M_STATE), f32),
                   jax.ShapeDtypeStruct((nb, CONV_W - 1, M_CONV_DIM), f32)),
        grid=(nb, nl),
        in_specs=[pl.BlockSpec((tl, D_MODEL), tok)]
                 + [wspec] + [_const_spec(a.shape) for a in (cw, cb, dtb, alog, dexp, nw)],
        out_specs=(pl.BlockSpec((tl, M_INNER), tok),
                   pl.BlockSpec((1, M_HEADS, M_HEADDIM, M_STATE), lambda b, l: (b, 0, 0, 0)),
                   pl.BlockSpec((1, CONV_W - 1, M_CONV_DIM), lambda b, l: (b, 0, 0))),
        scratch_shapes=[pltpu.VMEM((tl, M_INNER), f32),
                        pltpu.VMEM((tl + SUBLANES, M_CONV_DIM), f32),
                        pltpu.VMEM((tl, M_CONV_DIM), f32),
                        pltpu.VMEM((tl, PAD), f32),
                        pltpu.VMEM((tl, M_INNER), f32),
                        pltpu.VMEM((tl, M_INNER), f32),
                        pltpu.VMEM((tl, M_INNER), f32),
                        pltpu.VMEM((tl, M_INNER), f32),
                        pltpu.VMEM((tl, tl), bf16),
                        pltpu.VMEM((3 * PAD, M_INNER), bf16),
                        pltpu.VMEM((M_GROUPS, M_STATE, M_GW), f32)],
        compiler_params=_params(("parallel", "arbitrary")),
        name="ssd_scan",
    )(x, w, cw, cb, dtb, alog, dexp, nw)


def _pair_blockdiag(x, left):
    return jnp.concatenate([jnp.where(left, x, jnp.zeros_like(x)), jnp.where(left, jnp.zeros_like(x), x)], axis=0)


def _gdn_scan_kernel(x_ref, w_ref, cw_ref, dtb_ref, alog_ref, nw_ref,
                     act_ref, st_ref, conv_ref,
                     gz_ref, xbuf_ref, qkv_ref, cum_ref, ee_ref,
                     q16_ref, k16_ref, kb16_ref, qe16_ref, kbe16_ref, kw16_ref, vb16_ref,
                     wy_ref, u0_ref, attn_ref, tri_ref, e2_ref, s_ref, *, tl, c):
    l = pl.program_id(1)
    n_pairs = G_HEADS // 2
    pw = 2 * G_DK

    @pl.when(l == 0)
    def _():
        s_ref[...] = jnp.zeros_like(s_ref)
        _chunk_block_mask(tri_ref, tl, c)
        _head_expander(e2_ref, G_DK)

    xb = _b(x_ref[...])
    ab = _dot_nt(xb, w_ref[G_QKV + G_VAL:G_QKV + G_VAL + PAD, :])
    _project_and_conv(xb, w_ref, 0, G_QKV, xbuf_ref, cw_ref, None, qkv_ref, tl, l == 0)
    gz_ref[...] = _dot_nt(xb, w_ref[G_QKV:G_QKV + G_VAL, :])
    g = -jnp.exp(alog_ref[...]) * jax.nn.softplus(ab + dtb_ref[...])
    cum, tot = _chunk_cumsum(tri_ref, cum_ref, g, c)
    beta = pltpu.roll(jax.nn.sigmoid(ab), PAD - G_HEADS, axis=1)
    e_c, w_c = jnp.exp(cum), jnp.exp(tot - cum)
    for hb in range(n_pairs):
        ls = slice(hb * pw, (hb + 1) * pw)
        e_x = _expand_heads(e_c, e2_ref.at[:, ls])
        w_x = _expand_heads(w_c, e2_ref.at[:, ls])
        b_x = _expand_heads(beta, e2_ref.at[:, ls])
        ee_ref[:, ls] = e_x
        qn, kn = [], []
        for t in range(2):
            hs = slice((2 * hb + t) * G_DK, (2 * hb + t + 1) * G_DK)
            qt, kt = qkv_ref[:, hs], qkv_ref[:, G_KEY + hs.start:G_KEY + hs.stop]
            qn.append(qt * lax.rsqrt(jnp.sum(jnp.square(qt), -1, keepdims=True) + NORM_EPS) * (G_DK ** -0.5))
            kn.append(kt * lax.rsqrt(jnp.sum(jnp.square(kt), -1, keepdims=True) + NORM_EPS))
        q, k = jnp.concatenate(qn, axis=1), jnp.concatenate(kn, axis=1)
        kb = k * b_x
        q16_ref[:, ls] = _b(q)
        k16_ref[:, ls] = _b(k)
        kb16_ref[:, ls] = _b(kb)
        qe16_ref[:, ls] = _b(q * e_x)
        kbe16_ref[:, ls] = _b(kb * e_x)
        kw16_ref[:, ls] = _b(k * w_x)
        vb16_ref[:, ls] = _b(qkv_ref[:, 2 * G_KEY + hb * pw:2 * G_KEY + (hb + 1) * pw] * b_x)

    lane = lax.broadcasted_iota(jnp.int32, (c, 2 * c), 1)
    rowi = lax.broadcasted_iota(jnp.int32, (c, 2 * c), 0)
    left = lane < c
    jloc = lane & (c - 1)
    causal2, strict2 = jloc <= rowi, jloc < rowi
    eye2 = (jloc == rowi).astype(f32)
    left_w = lax.broadcasted_iota(jnp.int32, (c, pw), 1) < G_DK
    left_s = lax.broadcasted_iota(jnp.int32, (G_DK, pw), 1) < G_DK
    pr = range(n_pairs)

    def precompute(it, carry):
        cks = [it * GDN_PRE_UNROLL + j for j in range(GDN_PRE_UNROLL)]
        rows = [pl.ds(pl.multiple_of(ck * c, c), c) for ck in cks]
        lsl = [slice(p * pw, (p + 1) * pw) for p in pr]
        cp = [(j, p) for j in range(GDN_PRE_UNROLL) for p in pr]
        cum_c = [cum_ref[r, :] for r in rows]
        cum_t = [jnp.concatenate([x, x], axis=0).T for x in cum_c]
        d2 = []
        for j, p in cp:
            colsel = jnp.where(left, cum_c[j][:, 2 * p:2 * p + 1], cum_c[j][:, 2 * p + 1:2 * p + 2])
            rowsel = jnp.where(left[0:1, :], cum_t[j][2 * p:2 * p + 1, :], cum_t[j][2 * p + 1:2 * p + 2, :])
            d2.append(jnp.where(causal2, jnp.exp(colsel - rowsel), 0.0))
        kbd = [_pair_blockdiag(k16_ref[rows[j], lsl[p]], left_w) for j, p in cp]
        kq = [_dot_nt(jnp.concatenate([kb16_ref[rows[j], lsl[p]], q16_ref[rows[j], lsl[p]]], axis=0), kbd[n])
              for n, (j, p) in enumerate(cp)]
        lm = [jnp.where(strict2, kq[n][0:c, :] * d2[n], 0.0) for n in range(len(cp))]
        attn = [kq[n][c:2 * c, :] * d2[n] for n in range(len(cp))]
        for n, (j, p) in enumerate(cp):
            attn_ref[rows[j], p * 2 * c:(p + 1) * 2 * c] = _b(attn[n])
        ps = [eye2 - x for x in lm]
        ms = [_dot(_b(m), _b(_pair_blockdiag(m, left))) for m in lm]
        kpow = 2
        while kpow < c:
            last = 2 * kpow >= c
            bd = [_b(_pair_blockdiag(m, left)) for m in ms]
            lhs = [_b(x) if last else _b(jnp.concatenate([m, x], axis=0)) for m, x in zip(ms, ps)]
            mp = [_dot(a, b) for a, b in zip(lhs, bd)]
            if last:
                ps = [x + y for x, y in zip(ps, mp)]
            else:
                ms = [y[0:c, :] for y in mp]
                ps = [x + y[c:2 * c, :] for x, y in zip(ps, mp)]
            kpow *= 2
        rhs = [jnp.concatenate([_pair_blockdiag(kbe16_ref[rows[j], lsl[p]], left_w),
                                _pair_blockdiag(vb16_ref[rows[j], lsl[p]], left_w)], axis=1) for j, p in cp]
        wu = [_dot(_b(ps[n]), rhs[n]) for n in range(len(cp))]
        for n, (j, p) in enumerate(cp):
            wy_ref[rows[j], lsl[p]] = _b(wu[n][:, 0:pw])
            u0_ref[rows[j], lsl[p]] = wu[n][:, pw:2 * pw]
        return carry

    lax.fori_loop(0, tl // (c * GDN_PRE_UNROLL), precompute, 0)

    def scan_chunk(ck):
        r0 = pl.multiple_of(ck * c, c)
        rows = pl.ds(r0, c)
        lsl = [slice(p * pw, (p + 1) * pw) for p in pr]
        sp = [s_ref[p] for p in pr]
        sbd = [_b(_pair_blockdiag(sp[p], left_s)) for p in pr]
        r = [_dot(jnp.concatenate([wy_ref[rows, lsl[p]], qe16_ref[rows, lsl[p]]], axis=0), sbd[p]) for p in pr]
        u = [u0_ref[rows, lsl[p]] - r[p][0:c, :] for p in pr]
        ubd = [_b(_pair_blockdiag(u[p], left_w)) for p in pr]
        for p in pr:
            kw = kw16_ref[rows, lsl[p]]
            kw_stack = jnp.concatenate([kw[:, 0:G_DK], kw[:, G_DK:pw]], axis=0)
            e_last = ee_ref[pl.ds(r0 + c - 1, 1), lsl[p]]
            s_ref[p] = sp[p] * e_last + _dot_tn(kw_stack, ubd[p])
        o = [r[p][c:2 * c, :] + _dot(attn_ref[rows, p * 2 * c:(p + 1) * 2 * c], ubd[p]) for p in pr]
        for p in pr:
            for t in range(2):
                hs = slice((2 * p + t) * G_DV, (2 * p + t + 1) * G_DV)
                act_ref[rows, hs] = _b(_rms(o[p][:, t * G_DV:(t + 1) * G_DV]) * nw_ref[...]
                                       * jax.nn.silu(gz_ref[rows, hs]))

    def scan(it, carry):
        for j in range(GDN_SCAN_UNROLL):
            scan_chunk(it * GDN_SCAN_UNROLL + j)
        return carry

    lax.fori_loop(0, tl // (c * GDN_SCAN_UNROLL), scan, 0)

    @pl.when(l == pl.num_programs(1) - 1)
    def _():
        for h in range(G_HEADS):
            st_ref[0, h] = s_ref[h // 2][:, (h % 2) * G_DV:(h % 2 + 1) * G_DV]
        conv_ref[0] = xbuf_ref[SUBLANES - (CONV_W - 1):SUBLANES, :]


def _gdn_scan(x, w, cw, dtb, alog, nw, nb, nl_tok, tl, c):
    assert 2 * c == LANES and G_DK == G_DV == LANES, "two heads' (c, c) blocks share one 128-lane slab"
    nl = nl_tok // tl
    tok = lambda b, l: (b * nl + l, 0)
    return pl.pallas_call(
        functools.partial(_gdn_scan_kernel, tl=tl, c=c),
        out_shape=(jax.ShapeDtypeStruct((nb * nl_tok, G_VAL), bf16),
                   jax.ShapeDtypeStruct((nb, G_HEADS, G_DK, G_DV), f32),
                   jax.ShapeDtypeStruct((nb, CONV_W - 1, G_QKV), f32)),
        grid=(nb, nl),
        in_specs=[pl.BlockSpec((tl, D_MODEL), tok)]
                 + [_const_spec(a.shape) for a in (w, cw, dtb, alog, nw)],
        out_specs=(pl.BlockSpec((tl, G_VAL), tok),
                   pl.BlockSpec((1, G_HEADS, G_DK, G_DV), lambda b, l: (b, 0, 0, 0)),
                   pl.BlockSpec((1, CONV_W - 1, G_QKV), lambda b, l: (b, 0, 0))),
        scratch_shapes=[pltpu.VMEM((tl, G_VAL), f32),
                        pltpu.VMEM((tl + SUBLANES, G_QKV), f32),
                        pltpu.VMEM((tl, G_QKV), f32),
                        pltpu.VMEM((tl, PAD), f32),
                        pltpu.VMEM((tl, G_KEY), f32),
                        ] + [pltpu.VMEM((tl, G_KEY), bf16)] * 7 + [
                        pltpu.VMEM((tl, G_KEY), bf16),
                        pltpu.VMEM((tl, G_VAL), f32),
                        pltpu.VMEM((tl, G_HEADS * c), bf16),
                        pltpu.VMEM((tl, tl), bf16),
                        pltpu.VMEM((2 * PAD, G_KEY), bf16),
                        pltpu.VMEM((G_HEADS // 2, G_DK, 2 * G_DV), f32)],
        compiler_params=_params(("parallel", "arbitrary")),
        name="gdn_scan",
    )(x, w, cw, dtb, alog, nw)


def _token_lanes_to_front(src_ref, dst_ref, i):
    n = src_ref.shape[1]
    dst_ref[...] = pltpu.roll(src_ref[...], (n - i * DEC_BT) % n, axis=1)


def _conv_step(cst_ref, x_new, cw_ref, cb_ref, conv_out_ref):
    acc = x_new * cw_ref[CONV_W - 1:CONV_W, :]
    for j in range(CONV_W - 1):
        acc = acc + cst_ref[j] * cw_ref[j:j + 1, :]
    if cb_ref is not None:
        acc = acc + cb_ref[...]
    for j in range(CONV_W - 2):
        conv_out_ref[j] = cst_ref[j + 1]
    conv_out_ref[CONV_W - 2] = x_new
    return jax.nn.silu(acc)


def _state_specs(state, prev, layer, blk):
    zeros = (0,) * (len(blk) - 2)
    spec = pl.BlockSpec(blk, lambda i: (layer, i) + zeros)
    if prev is None:
        prev = jnp.zeros((SUBLANES, LANES), f32)
        alias = {}
    else:
        alias = None
    return spec, prev, alias


def _dec_ret_kernel(x_ref, w_ref, cos_ref, sin_ref, s_in, prev_ref, act_ref, s_out,
                    q_ref, kt_ref, vg_ref, ks_ref, o_ref):
    del prev_ref
    i = pl.program_id(0)

    @pl.when(i == 0)
    def _():
        proj = _dot_nt(_b(x_ref[...]), w_ref[...])
        cos, sin = cos_ref[...], sin_ref[...]
        for h in range(R_HEADS):
            hs = slice(h * R_DK, (h + 1) * R_DK)
            t = proj[:, hs]
            q_ref[:, hs] = t * cos + pltpu.roll(t, R_DK // 2, axis=1) * sin
            t = proj[:, R_QK + h * R_DK:R_QK + (h + 1) * R_DK]
            q_ref[:, R_QK + h * R_DK:R_QK + (h + 1) * R_DK] = t = (
                t * cos + pltpu.roll(t, R_DK // 2, axis=1) * sin) * (R_DK ** -0.5)
            kt_ref[hs, :] = t.T
        vg_ref[...] = proj[:, 2 * R_QK:2 * R_QK + 2 * R_VAL]

    _token_lanes_to_front(kt_ref, ks_ref, i)
    rows = pl.ds(pl.multiple_of(i * DEC_BT, DEC_BT), DEC_BT)
    vg, qk8 = vg_ref[rows, :], q_ref[rows, :]
    for h in range(R_HEADS):
        q8 = qk8[:, h * R_DK:(h + 1) * R_DK]
        k8 = qk8[:, R_QK + h * R_DK:R_QK + (h + 1) * R_DK]
        qk = jnp.sum(q8 * k8, axis=1, keepdims=True)
        lhs = _b(jnp.concatenate([q8, q8], axis=0))
        gam = math.exp(LOG_GAMMA[h])
        for j in range(DEC_BT):
            kc = ks_ref[h * R_DK:(h + 1) * R_DK, j:j + 1]
            v_row = vg[j:j + 1, h * R_DV:(h + 1) * R_DV]
            s = s_in[j, h]
            qs = _dot(lhs, _b(s))[j:j + 1, :]
            o_ref[j:j + 1, h * R_DV:(h + 1) * R_DV] = gam * qs + qk[j:j + 1, :] * v_row
            s_out[j, h] = s * gam + kc * v_row
    for h in range(R_HEADS):
        o = o_ref[:, h * R_DV:(h + 1) * R_DV]
        mu = jnp.mean(o, -1, keepdims=True)
        var = jnp.mean(jnp.square(o - mu), -1, keepdims=True)
        g = vg[:, R_VAL + h * R_DV:R_VAL + (h + 1) * R_DV]
        act_ref[:, h * R_DV:(h + 1) * R_DV] = (o - mu) * lax.rsqrt(var + LN_EPS) * jax.nn.silu(g)


def _dec_ret(x, w, cos, sin, state, layer, prev):
    n = x.shape[0]
    w, wspec = _weight(w)
    consts = (x, w, cos, sin)
    sspec, prev, alias = _state_specs(state, prev, layer, (None, DEC_BT, R_HEADS, R_DK, R_DV))
    return pl.pallas_call(
        _dec_ret_kernel,
        out_shape=(jax.ShapeDtypeStruct((n, R_VAL), f32), jax.ShapeDtypeStruct(state.shape, f32)),
        grid=(n // DEC_BT,),
        in_specs=[_const_spec(x.shape), wspec, _const_spec(cos.shape), _const_spec(sin.shape),
                  sspec, pl.BlockSpec(memory_space=pl.ANY)],
        out_specs=(pl.BlockSpec((DEC_BT, R_VAL), lambda i: (i, 0)), sspec),
        scratch_shapes=[pltpu.VMEM((n, 2 * R_QK), f32),
                        pltpu.VMEM((R_QK, n), f32),
                        pltpu.VMEM((n, 2 * R_VAL), f32),
                        pltpu.VMEM((R_QK, n), f32),
                        pltpu.VMEM((DEC_BT, R_VAL), f32)],
        input_output_aliases={len(consts) + 1: 1} if alias is None else alias,
        compiler_params=_params(("arbitrary",)),
        name="dec_ret",
    )(*consts, state, prev)


def _dec_ssd_kernel(x_ref, w_ref, cst_ref, cw_ref, cb_ref, dtb_ref, alog_ref, dexp_ref, nw_ref, s_in, prev_ref,
                    act_ref, s_out, conv_out_ref,
                    xc_ref, z_ref, xdt_ref, xdtt_ref, ela_ref, elax_ref, xts_ref, e2_ref, o_ref):
    del prev_ref
    i = pl.program_id(0)

    @pl.when(i == 0)
    def _():
        _head_expander(e2_ref, M_HEADDIM)
        proj = _dot_nt(_b(x_ref[...]), w_ref[...])
        xc = _conv_step(cst_ref, proj[:, M_INNER:M_INNER + M_CONV_DIM], cw_ref, cb_ref, conv_out_ref)
        xc_ref[...] = xc
        z_ref[...] = proj[:, 0:M_INNER]
        dt = jax.nn.softplus(proj[:, M_INNER + M_CONV_DIM:M_INNER + M_CONV_DIM + PAD] + dtb_ref[...])
        ela = jnp.exp(-jnp.exp(alog_ref[...]) * dt)
        ela_ref[...] = ela
        elax_ref[...] = _expand_heads(ela, e2_ref)
        xdt = xc[:, 0:M_INNER] * _expand_heads(dt, e2_ref)
        xdt_ref[...] = xdt
        xdtt_ref[...] = xdt.T

    _token_lanes_to_front(xdtt_ref, xts_ref, i)
    rows = pl.ds(pl.multiple_of(i * DEC_BT, DEC_BT), DEC_BT)
    xc8, xdt8, ela8, elax8 = xc_ref[rows, :], xdt_ref[rows, :], ela_ref[rows, :], elax_ref[rows, :]
    b_off, c_off = M_INNER, M_INNER + M_GROUPS * M_STATE
    for g in range(M_GROUPS):
        gs = slice(g * M_GW, (g + 1) * M_GW)
        b8 = xc8[:, b_off + g * M_STATE:b_off + (g + 1) * M_STATE]
        c8 = xc8[:, c_off + g * M_STATE:c_off + (g + 1) * M_STATE]
        cb = jnp.sum(c8 * b8, axis=1, keepdims=True)
        lhs = _b(jnp.concatenate([c8, c8], axis=0))
        for j in range(DEC_BT):
            st = s_in[j, g * M_HPG:(g + 1) * M_HPG]
            cs = _dot_nt(lhs, _b(st.reshape(M_GW, M_STATE)))[j:j + 1, :]
            o_ref[j:j + 1, gs] = cb[j:j + 1, :] * xdt8[j:j + 1, gs] + elax8[j:j + 1, gs] * cs
            b_row = b8[j:j + 1, :]
            for hh in range(M_HPG):
                h = g * M_HPG + hh
                xdt_col = xts_ref[h * M_HEADDIM:(h + 1) * M_HEADDIM, j:j + 1]
                s_out[j, h] = st[hh] * ela8[j:j + 1, h:h + 1] + xdt_col * b_row
    y = (o_ref[...] + dexp_ref[...] * xc8[:, 0:M_INNER]) * jax.nn.silu(z_ref[rows, :])
    for g in range(M_GROUPS):
        gs = slice(g * M_GW, (g + 1) * M_GW)
        act_ref[:, gs] = _rms(y[:, gs]) * nw_ref[:, gs]


def _dec_ssd(x, w, cst, cw, cb, dtb, alog, dexp, nw, state, layer, prev):
    n = x.shape[0]
    w, wspec = _weight(w)
    consts = (x, w, cst, cw, cb, dtb, alog, dexp, nw)
    sspec, prev, alias = _state_specs(state, prev, layer, (None, DEC_BT, M_HEADS, M_HEADDIM, M_STATE))
    cspecs = [_const_spec(a.shape) for a in consts]
    cspecs[1] = wspec
    cspecs[2] = pl.BlockSpec((None,) + cst.shape[1:], lambda i: (layer, 0, 0, 0), pipeline_mode=pl.Buffered(1))
    return pl.pallas_call(
        _dec_ssd_kernel,
        out_shape=(jax.ShapeDtypeStruct((n, M_INNER), f32), jax.ShapeDtypeStruct(state.shape, f32),
                   jax.ShapeDtypeStruct(cst.shape[1:], f32)),
        grid=(n // DEC_BT,),
        in_specs=cspecs + [sspec, pl.BlockSpec(memory_space=pl.ANY)],
        out_specs=(pl.BlockSpec((DEC_BT, M_INNER), lambda i: (i, 0)), sspec,
                   pl.BlockSpec(cst.shape[1:], lambda i: (0, 0, 0))),
        scratch_shapes=[pltpu.VMEM((n, M_CONV_DIM), f32),
                        pltpu.VMEM((n, M_INNER), f32),
                        pltpu.VMEM((n, M_INNER), f32),
                        pltpu.VMEM((M_INNER, n), f32),
                        pltpu.VMEM((n, PAD), f32),
                        pltpu.VMEM((n, M_INNER), f32),
                        pltpu.VMEM((M_INNER, n), f32),
                        pltpu.VMEM((2 * PAD, M_INNER), bf16),
                        pltpu.VMEM((DEC_BT, M_INNER), f32)],
        input_output_aliases={len(consts) + 1: 1} if alias is None else alias,
        compiler_params=_params(("arbitrary",)),
        name="dec_ssd",
    )(*consts, state, prev)


def _dec_gdn_kernel(x_ref, w_ref, cst_ref, cw_ref, dtb_ref, alog_ref, nw_ref, s_in, prev_ref,
                    act_ref, s_out, conv_out_ref,
                    qkv_ref, kt_ref, gz_ref, eg_ref, beta_ref, ks_ref, o_ref):
    del prev_ref
    i = pl.program_id(0)

    @pl.when(i == 0)
    def _():
        proj = _dot_nt(_b(x_ref[...]), w_ref[...])
        qkv = _conv_step(cst_ref, proj[:, 0:G_QKV], cw_ref, None, conv_out_ref)
        for h in range(G_HEADS):
            hs = slice(h * G_DK, (h + 1) * G_DK)
            q = qkv[:, hs]
            qkv_ref[:, hs] = q * lax.rsqrt(jnp.sum(jnp.square(q), -1, keepdims=True) + NORM_EPS) * (G_DK ** -0.5)
            k = qkv[:, G_KEY + h * G_DK:G_KEY + (h + 1) * G_DK]
            k = k * lax.rsqrt(jnp.sum(jnp.square(k), -1, keepdims=True) + NORM_EPS)
            qkv_ref[:, G_KEY + h * G_DK:G_KEY + (h + 1) * G_DK] = k
            kt_ref[hs, :] = k.T
        qkv_ref[:, 2 * G_KEY:] = qkv[:, 2 * G_KEY:]
        gz_ref[...] = proj[:, G_QKV:G_QKV + G_VAL]
        ab = proj[:, G_QKV + G_VAL:G_QKV + G_VAL + PAD]
        eg_ref[...] = jnp.exp(-jnp.exp(alog_ref[...]) * jax.nn.softplus(ab + dtb_ref[...]))
        beta_ref[...] = jax.nn.sigmoid(ab)

    _token_lanes_to_front(kt_ref, ks_ref, i)
    rows = pl.ds(pl.multiple_of(i * DEC_BT, DEC_BT), DEC_BT)
    qkv8, eg8, beta8 = qkv_ref[rows, :], eg_ref[rows, :], beta_ref[rows, :]
    for h in range(G_HEADS):
        hs = slice(h * G_DV, (h + 1) * G_DV)
        q8 = qkv8[:, h * G_DK:(h + 1) * G_DK]
        k8 = qkv8[:, G_KEY + h * G_DK:G_KEY + (h + 1) * G_DK]
        v8 = qkv8[:, 2 * G_KEY + h * G_DV:2 * G_KEY + (h + 1) * G_DV]
        qk = jnp.sum(q8 * k8, axis=1, keepdims=True)
        lhs = _b(jnp.concatenate([q8, k8], axis=0))
        bh = beta8[:, G_HEADS + h:G_HEADS + h + 1]
        eg = eg8[:, h:h + 1]
        for j in range(DEC_BT):
            kc = ks_ref[h * G_DK:(h + 1) * G_DK, j:j + 1]
            s = s_in[j, h]
            qks = _dot(lhs, _b(s))
            bj, ej = bh[j:j + 1, :], eg[j:j + 1, :]
            u = v8[j:j + 1, :] * bj - (bj * ej) * qks[DEC_BT + j:DEC_BT + j + 1, :]
            o_ref[j:j + 1, hs] = ej * qks[j:j + 1, :] + qk[j:j + 1, :] * u
            s_out[j, h] = s * ej + kc * u
    gz8 = gz_ref[rows, :]
    for h in range(G_HEADS):
        hs = slice(h * G_DV, (h + 1) * G_DV)
        act_ref[:, hs] = _rms(o_ref[:, hs]) * nw_ref[...] * jax.nn.silu(gz8[:, hs])


def _dec_gdn(x, w, cst, cw, dtb, alog, nw, state, layer, prev):
    n = x.shape[0]
    consts = (x, w, cst, cw, dtb, alog, nw)
    sspec, prev, alias = _state_specs(state, prev, layer, (None, DEC_BT, G_HEADS, G_DK, G_DV))
    cspecs = [_const_spec(a.shape) for a in consts]
    cspecs[2] = pl.BlockSpec((None,) + cst.shape[1:], lambda i: (layer, 0, 0, 0), pipeline_mode=pl.Buffered(1))
    return pl.pallas_call(
        _dec_gdn_kernel,
        out_shape=(jax.ShapeDtypeStruct((n, G_VAL), f32), jax.ShapeDtypeStruct(state.shape, f32),
                   jax.ShapeDtypeStruct(cst.shape[1:], f32)),
        grid=(n // DEC_BT,),
        in_specs=cspecs + [sspec, pl.BlockSpec(memory_space=pl.ANY)],
        out_specs=(pl.BlockSpec((DEC_BT, G_VAL), lambda i: (i, 0)), sspec,
                   pl.BlockSpec(cst.shape[1:], lambda i: (0, 0, 0))),
        scratch_shapes=[pltpu.VMEM((n, G_QKV), f32),
                        pltpu.VMEM((G_KEY, n), f32),
                        pltpu.VMEM((n, G_VAL), f32),
                        pltpu.VMEM((n, PAD), f32), pltpu.VMEM((n, PAD), f32),
                        pltpu.VMEM((G_KEY, n), f32),
                        pltpu.VMEM((DEC_BT, G_VAL), f32)],
        input_output_aliases={len(consts) + 1: 1} if alias is None else alias,
        compiler_params=_params(("arbitrary",)),
        name="dec_gdn",
    )(*consts, state, prev)


def _rope_tables(pos):
    half = R_DK // 2
    inv = ROPE_BASE ** (-jnp.arange(half, dtype=f32) / half)
    ang = pos[:, None] * inv[None, :]
    cos, sin = jnp.cos(ang), jnp.sin(ang)
    return jnp.concatenate([cos, cos], axis=1), jnp.concatenate([-sin, sin], axis=1)


def _lane_pad(v, start=0):
    return jnp.zeros((1, PAD), f32).at[0, start:start + v.shape[0]].set(v)


def _stacked_weights(prm):
    return {k: _b(prm[k]) for k in ("ffn_wg", "ffn_wu", "ffn_wd", "w_ret_out", "w_ssm_out", "w_gdn_out", "w_o",
                                    "pe_proj", "pe_gate")}


def _layer_weights(i, prm):
    return dict(
        ln_g=prm["ln_g"][i], ln_b=prm["ln_b"][i],
        ssm_cw=prm["ssm_conv_w"][i], ssm_cb=prm["ssm_conv_b"][i][None, :],
        ssm_dtb=_lane_pad(prm["ssm_dt_bias"][i]), ssm_alog=_lane_pad(prm["ssm_a_log"][i]),
        ssm_dexp=jnp.repeat(prm["ssm_d"][i], M_HEADDIM)[None, :], ssm_nw=prm["ssm_norm_w"][i][None, :],
        gdn_cw=prm["gdn_conv_w"][i],
        gdn_dtb=_lane_pad(prm["gdn_dt_bias"][i]), gdn_alog=_lane_pad(prm["gdn_a_log"][i]),
        gdn_nw=prm["gdn_norm_w"][i][None, :],
    )


def _post_mix(x1, acts, p, i, w, sw, tm):
    x2 = _merge(x1, *acts, w["w_merge"], sw["w_ret_out"], sw["w_ssm_out"], sw["w_gdn_out"], sw["w_o"],
                w["ln_g"], w["ln_b"], i, tm)
    return _ffn_pe(x2, p, i, sw["ffn_wg"], sw["ffn_wu"], sw["ffn_wd"], sw["pe_gate"], sw["pe_proj"],
                   w["ln_g"], w["ln_b"], tm)


def kernel(x_prompt, x_sample, state_ret, state_ssm, state_ssm_conv, state_gdn, state_gdn_conv,
           p_prompt, p_sample, ln_g, ln_b, ffn_wg, ffn_wu, ffn_wd, w_in,
           ssm_conv_w, ssm_conv_b, ssm_dt_bias, ssm_a_log, ssm_d, ssm_norm_w,
           gdn_conv_w, gdn_dt_bias, gdn_a_log, gdn_norm_w,
           w_ret_out, w_ssm_out, w_gdn_out, w_o, pe_proj, pe_gate):
    prm = dict(ln_g=ln_g, ln_b=ln_b, ffn_wg=ffn_wg, ffn_wu=ffn_wu, ffn_wd=ffn_wd, w_in=w_in,
               ssm_conv_w=ssm_conv_w, ssm_conv_b=ssm_conv_b, ssm_dt_bias=ssm_dt_bias,
               ssm_a_log=ssm_a_log, ssm_d=ssm_d, ssm_norm_w=ssm_norm_w,
               gdn_conv_w=gdn_conv_w, gdn_dt_bias=gdn_dt_bias, gdn_a_log=gdn_a_log,
               gdn_norm_w=gdn_norm_w, w_ret_out=w_ret_out, w_ssm_out=w_ssm_out,
               w_gdn_out=w_gdn_out, w_o=w_o, pe_proj=pe_proj, pe_gate=pe_gate)
    nb, seq, _ = x_prompt.shape
    ns = x_sample.shape[0]
    depth = w_in.shape[0]
    tl = min(TL_SCAN, seq)
    tm = min(TM_DENSE, nb * seq)
    chunk = CHUNK if seq % CHUNK == 0 else seq

    cos_p, sin_p = _rope_tables(jnp.arange(seq, dtype=f32))
    cos_s, sin_s = _rope_tables(jnp.full((1,), PAST_LEN, f32))
    xp = x_prompt.reshape(nb * seq, D_MODEL)
    xs = x_sample.reshape(ns, D_MODEL)
    pp = p_prompt.reshape(depth, nb * seq, PLE_DIM)
    ps = p_sample.reshape(depth, ns, PLE_DIM)
    ssm_state_t = jnp.swapaxes(state_ssm, 3, 4)
    ssm_conv_t = jnp.transpose(state_ssm_conv, (0, 2, 1, 3))
    gdn_conv_t = jnp.transpose(state_gdn_conv, (0, 2, 1, 3))

    sw = _stacked_weights(prm)
    w_in_t = _b(jnp.swapaxes(w_in, 1, 2))
    gdn_rows = G_QKV + G_VAL + PAD
    assert OFF_RET == 0 and OFF_SSD == W_IN_BLK and OFF_SSD + M_INNER + M_CONV_DIM + PAD <= 2 * W_IN_BLK
    assert OFF_GDN + gdn_rows <= IN_DIM
    prompt_states, sample_convs = [], []
    t_r = t_s = t_g = None
    for i in range(depth):
        w = _layer_weights(i, prm)
        w["w_ret"], w["w_ssd"] = _w_in_block(w_in_t, i, 0), _w_in_block(w_in_t, i, 1)
        w["w_gdn"] = w_in_t[i, OFF_GDN:OFF_GDN + gdn_rows]
        w["w_merge"] = w_in_t[i, OFF_MERGE:]
        x1 = _ffn_ln(xp, sw["ffn_wg"], sw["ffn_wu"], sw["ffn_wd"], w["ln_g"], w["ln_b"], i, tm)
        a_r, s_r = _ret_scan(x1, w["w_ret"], cos_p, sin_p, nb, seq, tl, RET_CHUNK if seq % RET_CHUNK == 0 else chunk)
        a_s, s_s, c_s = _ssd_scan(x1, w["w_ssd"], w["ssm_cw"], w["ssm_cb"], w["ssm_dtb"], w["ssm_alog"],
                                  w["ssm_dexp"], w["ssm_nw"], nb, seq, tl, chunk)
        a_g, s_g, c_g = _gdn_scan(x1, w["w_gdn"], w["gdn_cw"], w["gdn_dtb"], w["gdn_alog"], w["gdn_nw"],
                                  nb, seq, tl, chunk)
        xp = _post_mix(x1, (a_r, a_s, a_g), pp, i, w, sw, tm)
        prompt_states.append((s_r, s_s, c_s, s_g, c_g))
        y1 = _ffn_ln(xs, sw["ffn_wg"], sw["ffn_wu"], sw["ffn_wd"], w["ln_g"], w["ln_b"], i, ns)
        b_r, t_r = _dec_ret(y1, w["w_ret"], cos_s, sin_s, state_ret, i, t_r)
        b_s, t_s, d_s = _dec_ssd(y1, w["w_ssd"], ssm_conv_t, w["ssm_cw"], w["ssm_cb"], w["ssm_dtb"],
                                 w["ssm_alog"], w["ssm_dexp"], w["ssm_nw"], ssm_state_t, i, t_s)
        b_g, t_g, d_g = _dec_gdn(y1, w["w_gdn"], gdn_conv_t, w["gdn_cw"], w["gdn_dtb"], w["gdn_alog"],
                                 w["gdn_nw"], state_gdn, i, t_g)
        xs = _post_mix(y1, (b_r, b_s, b_g), ps, i, w, sw, ns)
        sample_convs.append((d_s, d_g))

    r_p, s_p, sc_p, g_p, gc_p = (jnp.stack([s[j] for s in prompt_states]) for j in range(5))
    sc_s, gc_s = (jnp.transpose(jnp.stack([c[j] for c in sample_convs]), (0, 2, 1, 3)) for j in range(2))
    return (xp.reshape(nb, seq, D_MODEL), xs.reshape(ns, 1, D_MODEL),
            r_p, jnp.swapaxes(s_p, 3, 4), sc_p, g_p, gc_p,
            t_r, jnp.swapaxes(t_s, 3, 4), sc_s, t_g, gc_s)
```

```python
import functools
import math

import numpy as np
import jax
import jax.numpy as jnp
from jax import lax
from jax.experimental import pallas as pl
from jax.experimental.pallas import tpu as pltpu

f32, bf16 = jnp.float32, jnp.bfloat16

D_MODEL = 1024
DEPTH = 2
PAST_LEN = 16384
R_HEADS, R_DK, R_DV = 4, 128, 256
R_QK, R_VAL = R_HEADS * R_DK, R_HEADS * R_DV
ROPE_BASE = 10000.0
M_HEADS, M_HEADDIM, M_GROUPS, M_STATE = 16, 64, 2, 128
M_INNER = M_HEADS * M_HEADDIM
M_CONV_DIM = M_INNER + 2 * M_GROUPS * M_STATE
M_HPG = M_HEADS // M_GROUPS
M_GW = M_HPG * M_HEADDIM
G_HEADS, G_DK, G_DV = 8, 128, 128
G_KEY, G_VAL = G_HEADS * G_DK, G_HEADS * G_DV
G_QKV = 2 * G_KEY + G_VAL
CONV_W = 4
FFN_DIM = 2048
PLE_DIM = 256
DN_ALPHA = (2 * DEPTH) ** 0.25
LN_EPS = 1e-5
NORM_EPS = 1e-6

_sizes = (R_QK, R_QK, R_VAL, R_VAL, M_INNER, M_CONV_DIM, M_HEADS, G_QKV, G_VAL, G_HEADS, G_HEADS,
          D_MODEL, D_MODEL, D_MODEL)
_off = np.concatenate([[0], np.cumsum(_sizes)]).tolist()
OFF_RET, OFF_SSD, OFF_GDN, OFF_MERGE, IN_DIM = _off[0], _off[4], _off[7], _off[11], _off[14]

LANES = 128
SUBLANES = 8
VMEM_LIMIT = 56 * 2 ** 20

TM_DENSE = 512
TM_FFN = 1024
TL_SCAN = 512
CHUNK = 64
FFN_CHUNK = 512
DEC_BT = 8
RET_CHUNK = 128
RET_UNROLL = 4
SSD_UNROLL = 4
GDN_PRE_UNROLL = 4
GDN_SCAN_UNROLL = 4
PAD = LANES

LOG_GAMMA = [math.log1p(-(2.0 ** (-5.0 - h))) for h in range(R_HEADS)]


def _dot(a, b):
    return jnp.dot(a, b, preferred_element_type=f32)


def _dot_nt(a, b):
    return lax.dot_general(a, b, (((1,), (1,)), ((), ())), preferred_element_type=f32)


def _dot_tn(a, b):
    return lax.dot_general(a, b, (((0,), (0,)), ((), ())), preferred_element_type=f32)


def _b(x):
    return x.astype(bf16)


def _layer_norm(y, g, b):
    mu = jnp.mean(y, -1, keepdims=True)
    var = jnp.mean(jnp.square(y - mu), -1, keepdims=True)
    return (y - mu) * lax.rsqrt(var + LN_EPS) * g + b


def _rms(y):
    return y * lax.rsqrt(jnp.mean(jnp.square(y), -1, keepdims=True) + NORM_EPS)


def _const_spec(shape):
    return pl.BlockSpec(shape, lambda *_: (0,) * len(shape), pipeline_mode=pl.Buffered(1))


def _layer_spec(arr, *lead):
    blk = (None,) * len(lead) + arr.shape[len(lead):]
    tail = (0,) * (arr.ndim - len(lead))
    return pl.BlockSpec(blk, lambda *_: tuple(lead) + tail, pipeline_mode=pl.Buffered(1))


W_IN_BLK = 3072


def _w_in_block(wt_all, layer, blk):
    return wt_all, pl.BlockSpec((None, W_IN_BLK, D_MODEL), lambda *_: (layer, blk, 0), pipeline_mode=pl.Buffered(1))


def _weight(w):
    return w if isinstance(w, tuple) else (w, _const_spec(w.shape))


def _params(sem):
    return pltpu.CompilerParams(dimension_semantics=sem, vmem_limit_bytes=VMEM_LIMIT)


def _split3(x):
    a1 = _b(x)
    r1 = x - a1.astype(f32)
    a2 = _b(r1)
    a3 = _b(r1 - a2.astype(f32))
    return a1, a2, a3


def _chunk_iotas(c):
    ii = lax.broadcasted_iota(jnp.int32, (c, c), 0)
    jj = lax.broadcasted_iota(jnp.int32, (c, c), 1)
    return ii, jj


def _swiglu(x, wg_ref, wu_ref, wd_ref):
    xb = _b(x)
    acc = None
    for c in range(FFN_DIM // FFN_CHUNK):
        sl = slice(c * FFN_CHUNK, (c + 1) * FFN_CHUNK)
        a = jax.nn.silu(_dot(xb, wg_ref[:, sl])) * _dot(xb, wu_ref[:, sl])
        part = _dot(_b(a), wd_ref[sl, :])
        acc = part if acc is None else acc + part
    return acc


def _ffn_ln_kernel(x_ref, wg_ref, wu_ref, wd_ref, g_ref, b_ref, o_ref):
    x = x_ref[...]
    y = DN_ALPHA * x + 0.5 * _swiglu(x, wg_ref, wu_ref, wd_ref)
    o_ref[...] = _layer_norm(y, g_ref[0:1, :], b_ref[0:1, :])


def _ffn_ln(x, wg, wu, wd, g, b, layer, tm):
    n = x.shape[0]
    return pl.pallas_call(
        _ffn_ln_kernel,
        out_shape=jax.ShapeDtypeStruct((n, D_MODEL), f32),
        grid=(n // tm,),
        in_specs=[pl.BlockSpec((tm, D_MODEL), lambda i: (i, 0)),
                  _layer_spec(wg, layer, 0), _layer_spec(wu, layer, 0), _layer_spec(wd, layer, 0),
                  _const_spec(g.shape), _const_spec(b.shape)],
        out_specs=pl.BlockSpec((tm, D_MODEL), lambda i: (i, 0)),
        compiler_params=_params(("parallel",)),
        name="ffn_ln",
    )(x, wg, wu, wd, g, b)


def _merge_kernel(x_ref, ar_ref, as_ref, ag_ref, wm_ref, wr_ref, ws_ref, wgd_ref, wo_ref, g_ref, b_ref, o_ref):
    x = x_ref[...]
    m = _dot_nt(_b(x), wm_ref[...])
    yr = _dot(_b(ar_ref[...]), wr_ref[...])
    ys = _dot(_b(as_ref[...]), ws_ref[...])
    yg = _dot(_b(ag_ref[...]), wgd_ref[...])
    mixed = (jax.nn.sigmoid(m[:, 0:D_MODEL]) * yr + jax.nn.sigmoid(m[:, D_MODEL:2 * D_MODEL]) * ys
             + jax.nn.sigmoid(m[:, 2 * D_MODEL:3 * D_MODEL]) * yg)
    y = DN_ALPHA * x + _dot(_b(mixed), wo_ref[...])
    o_ref[...] = _layer_norm(y, g_ref[1:2, :], b_ref[1:2, :])


def _merge(x, ar, a_s, ag, wm, wr, ws, wgd, wo, g, b, layer, tm):
    n = x.shape[0]
    tok = lambda i: (i, 0)
    return pl.pallas_call(
        _merge_kernel,
        out_shape=jax.ShapeDtypeStruct((n, D_MODEL), f32),
        grid=(n // tm,),
        in_specs=[pl.BlockSpec((tm, D_MODEL), tok)] * 4
                 + [_const_spec(wm.shape)] + [_layer_spec(w, layer) for w in (wr, ws, wgd, wo)]
                 + [_const_spec(g.shape), _const_spec(b.shape)],
        out_specs=pl.BlockSpec((tm, D_MODEL), tok),
        compiler_params=_params(("parallel",)),
        name="merge",
    )(x, ar, a_s, ag, wm, wr, ws, wgd, wo, g, b)


def _ffn_pe_kernel(x_ref, p_ref, wg_ref, wu_ref, wd_ref, pg_ref, pp_ref, g_ref, b_ref, o_ref):
    x = x_ref[...]
    x = _layer_norm(DN_ALPHA * x + 0.5 * _swiglu(x, wg_ref, wu_ref, wd_ref), g_ref[2:3, :], b_ref[2:3, :])
    pe = jax.nn.sigmoid(_dot(_b(x), pg_ref[...])) * _dot(_b(p_ref[...]), pp_ref[...])
    o_ref[...] = _layer_norm(DN_ALPHA * x + pe, g_ref[3:4, :], b_ref[3:4, :])


def _ffn_pe(x, p, layer, wg, wu, wd, pg, pp, g, b, tm):
    n = x.shape[0]
    return pl.pallas_call(
        _ffn_pe_kernel,
        out_shape=jax.ShapeDtypeStruct((n, D_MODEL), f32),
        grid=(n // tm,),
        in_specs=[pl.BlockSpec((tm, D_MODEL), lambda i: (i, 0)),
                  pl.BlockSpec((None, tm, PLE_DIM), lambda i: (layer, i, 0)),
                  _layer_spec(wg, layer, 1), _layer_spec(wu, layer, 1), _layer_spec(wd, layer, 1),
                  _layer_spec(pg, layer), _layer_spec(pp, layer),
                  _const_spec(g.shape), _const_spec(b.shape)],
        out_specs=pl.BlockSpec((tm, D_MODEL), lambda i: (i, 0)),
        compiler_params=_params(("parallel",)),
        name="ffn_pe",
    )(x, p, wg, wu, wd, pg, pp, g, b)


def _rope_inplace(proj_ref, off, cos, sin, scale):
    t = proj_ref[:, off:off + R_DK]
    t = t * cos + pltpu.roll(t, R_DK // 2, axis=1) * sin
    if scale != 1.0:
        t = t * scale
    proj_ref[:, off:off + R_DK] = t


def _ret_scan_kernel(x_ref, w_ref, cos_ref, sin_ref, act_ref, st_ref, proj_ref, s_ref, *, tl, c):
    l = pl.program_id(1)

    @pl.when(l == 0)
    def _():
        s_ref[...] = jnp.zeros_like(s_ref)

    proj_ref[...] = _dot_nt(_b(x_ref[...]), w_ref[...])
    cos, sin = cos_ref[...], sin_ref[...]
    for h in range(R_HEADS):
        _rope_inplace(proj_ref, h * R_DK, cos, sin, 1.0)
        _rope_inplace(proj_ref, R_QK + h * R_DK, cos, sin, R_DK ** -0.5)

    ii, jj = _chunk_iotas(c)
    dif = (ii - jj).astype(f32)
    ci = lax.broadcasted_iota(jnp.int32, (c, 1), 0).astype(f32)
    decay = [jnp.where(dif >= 0, jnp.exp(dif * lg), 0.0) for lg in LOG_GAMMA]
    e_col = [jnp.exp((ci + 1.0) * lg) for lg in LOG_GAMMA]
    w_col = [jnp.exp((c - 1.0 - ci) * lg) for lg in LOG_GAMMA]

    def chunk(ck):
        rows = pl.ds(pl.multiple_of(ck * c, c), c)
        hd = range(R_HEADS)
        q = [proj_ref[rows, h * R_DK:(h + 1) * R_DK] for h in hd]
        k = [proj_ref[rows, R_QK + h * R_DK:R_QK + (h + 1) * R_DK] for h in hd]
        v = [_b(proj_ref[rows, 2 * R_QK + h * R_DV:2 * R_QK + (h + 1) * R_DV]) for h in hd]
        s = [s_ref[h] for h in hd]
        scores = [_dot_nt(_b(q[h]), _b(k[h])) * decay[h] for h in hd]
        for h in hd:
            s_ref[h] = s[h] * math.exp(c * LOG_GAMMA[h]) + _dot_tn(_b(k[h] * w_col[h]), v[h])
        o = [_dot(_b(jnp.concatenate([scores[h], q[h] * e_col[h]], axis=1)),
                  jnp.concatenate([v[h], _b(s[h])], axis=0)) for h in hd]
        for h in hd:
            mu = jnp.mean(o[h], -1, keepdims=True)
            var = jnp.mean(jnp.square(o[h] - mu), -1, keepdims=True)
            on = (o[h] - mu) * lax.rsqrt(var + LN_EPS)
            g = proj_ref[rows, 2 * R_QK + R_VAL + h * R_DV:2 * R_QK + R_VAL + (h + 1) * R_DV]
            act_ref[rows, h * R_DV:(h + 1) * R_DV] = _b(on * jax.nn.silu(g))

    def chunks(it, carry):
        for j in range(RET_UNROLL):
            chunk(it * RET_UNROLL + j)
        return carry

    lax.fori_loop(0, tl // (c * RET_UNROLL), chunks, 0)

    @pl.when(l == pl.num_programs(1) - 1)
    def _():
        st_ref[0] = s_ref[...]


def _ret_scan(x, w, cos, sin, nb, nl_tok, tl, c):
    w, wspec = _weight(w)
    nl = nl_tok // tl
    tok = lambda b, l: (b * nl + l, 0)
    return pl.pallas_call(
        functools.partial(_ret_scan_kernel, tl=tl, c=c),
        out_shape=(jax.ShapeDtypeStruct((nb * nl_tok, R_VAL), bf16),
                   jax.ShapeDtypeStruct((nb, R_HEADS, R_DK, R_DV), f32)),
        grid=(nb, nl),
        in_specs=[pl.BlockSpec((tl, D_MODEL), tok), wspec,
                  pl.BlockSpec((tl, R_DK), lambda b, l: (l, 0)),
                  pl.BlockSpec((tl, R_DK), lambda b, l: (l, 0))],
        out_specs=(pl.BlockSpec((tl, R_VAL), tok),
                   pl.BlockSpec((1, R_HEADS, R_DK, R_DV), lambda b, l: (b, 0, 0, 0))),
        scratch_shapes=[pltpu.VMEM((tl, 2 * R_QK + 2 * R_VAL), f32),
                        pltpu.VMEM((R_HEADS, R_DK, R_DV), f32)],
        compiler_params=_params(("parallel", "arbitrary")),
        name="ret_scan",
    )(x, w, cos, sin)


CONV_BLK = 512


def _project_and_conv(xb, w_ref, w_off, width, xbuf_ref, cw_ref, cb_ref, dst_ref, tl, first):
    @pl.when(first)
    def _():
        xbuf_ref[0:SUBLANES, :] = jnp.zeros((SUBLANES, width), f32)

    def project(n):
        cs = slice(n * CONV_BLK, (n + 1) * CONV_BLK)
        xbuf_ref[SUBLANES:SUBLANES + tl, cs] = _dot_nt(xb, w_ref[w_off + cs.start:w_off + cs.stop, :])

    def conv(n):
        cs = slice(n * CONV_BLK, (n + 1) * CONV_BLK)
        acc = xbuf_ref[SUBLANES:SUBLANES + tl, cs] * cw_ref[CONV_W - 1:CONV_W, cs]
        for j in range(CONV_W - 1):
            r0 = SUBLANES - (CONV_W - 1) + j
            acc = acc + xbuf_ref[r0:r0 + tl, cs] * cw_ref[j:j + 1, cs]
        if cb_ref is not None:
            acc = acc + cb_ref[:, cs]
        dst_ref[:, cs] = jax.nn.silu(acc)
        xbuf_ref[0:SUBLANES, cs] = xbuf_ref[tl:tl + SUBLANES, cs]

    nblk = width // CONV_BLK
    project(0)
    for n in range(nblk):
        if n + 1 < nblk:
            project(n + 1)
        conv(n)


def _head_expander(e2_ref, width):
    er = lax.broadcasted_iota(jnp.int32, e2_ref.shape, 0)
    el = lax.broadcasted_iota(jnp.int32, e2_ref.shape, 1)
    e2_ref[...] = _b(((er & (PAD - 1)) == (el >> (width.bit_length() - 1))).astype(f32))


def _chunk_block_mask(tri_ref, tl, c):
    log2c = c.bit_length() - 1
    ti = lax.broadcasted_iota(jnp.int32, (tl, tl), 0)
    tj = lax.broadcasted_iota(jnp.int32, (tl, tl), 1)
    tri_ref[...] = _b(((ti >= tj) & ((ti >> log2c) == (tj >> log2c))).astype(f32))


def _chunk_cumsum(tri_ref, cum_ref, la, c):
    a1, a2, a3 = _split3(la)
    cum = _dot(tri_ref[...], a1) + _dot(tri_ref[...], a2) + _dot(tri_ref[...], a3)
    cum_ref[...] = cum
    tl = la.shape[0]
    tot = jnp.concatenate([jnp.broadcast_to(cum_ref[k * c + c - 1:k * c + c, :], (c, la.shape[1]))
                           for k in range(tl // c)], axis=0)
    return cum, tot


def _expand_heads(v, e2_ref):
    hi = _b(v)
    lo = _b(v - hi.astype(f32))
    return _dot(jnp.concatenate([hi, lo], axis=1), e2_ref[...])


def _expand_heads_exact(v, e3_ref):
    return _dot(jnp.concatenate(_split3(v), axis=1), e3_ref[...])


def _ssd_scan_kernel(x_ref, w_ref, cw_ref, cb_ref, dtb_ref, alog_ref, dexp_ref, nw_ref,
                     act_ref, st_ref, conv_ref,
                     z_ref, xbuf_ref, xc_ref, cum_ref, cumx_ref, xdt_ref, xw_ref, ee_ref,
                     tri_ref, e3_ref, s_ref, *, tl, c):
    l = pl.program_id(1)

    @pl.when(l == 0)
    def _():
        s_ref[...] = jnp.zeros_like(s_ref)
        _chunk_block_mask(tri_ref, tl, c)
        _head_expander(e3_ref, M_HEADDIM)

    xb = _b(x_ref[...])
    dt = jax.nn.softplus(_dot_nt(xb, w_ref[M_INNER + M_CONV_DIM:M_INNER + M_CONV_DIM + PAD, :]) + dtb_ref[...])
    _project_and_conv(xb, w_ref, M_INNER, M_CONV_DIM, xbuf_ref, cw_ref, cb_ref, xc_ref, tl, l == 0)
    z_ref[...] = _dot_nt(xb, w_ref[0:M_INNER, :])
    la = -jnp.exp(alog_ref[...]) * dt
    cum, tot = _chunk_cumsum(tri_ref, cum_ref, la, c)
    cumx_ref[...] = _expand_heads_exact(cum, e3_ref)
    e2_ref = e3_ref.at[0:2 * PAD, :]
    xdt = xc_ref[:, 0:M_INNER] * _expand_heads(dt, e2_ref)
    xdt_ref[...] = xdt
    xw_ref[...] = xdt * _expand_heads(jnp.exp(tot - cum), e2_ref)
    ee_ref[...] = _expand_heads(jnp.exp(cum), e2_ref)

    half = c
    lane = lax.broadcasted_iota(jnp.int32, (c, 2 * half), 1)
    rowi = lax.broadcasted_iota(jnp.int32, (c, 2 * half), 0)
    left = lane < half
    causal2 = (lane & (half - 1)) <= rowi
    b_off, c_off = M_INNER, M_INNER + M_GROUPS * M_STATE
    gr = range(M_GROUPS)
    pairs = [(g, pp) for g in gr for pp in range(M_HPG // 2)]

    def chunk(ck):
        r0 = pl.multiple_of(ck * c, c)
        rows = pl.ds(r0, c)
        cum_c = cum_ref[rows, :]
        cum_t = jnp.concatenate([cum_c, cum_c], axis=0).T
        bb = [_b(xc_ref[rows, b_off + g * M_STATE:b_off + (g + 1) * M_STATE]) for g in gr]
        cb = [_b(xc_ref[rows, c_off + g * M_STATE:c_off + (g + 1) * M_STATE]) for g in gr]
        g2 = [_dot_nt(cb[g], jnp.concatenate([bb[g], bb[g]], axis=0)) for g in gr]
        sg = [s_ref[g] for g in gr]
        inter = [ee_ref[rows, g * M_GW:(g + 1) * M_GW] * _dot(cb[g], _b(sg[g])) for g in gr]
        for g in gr:
            e_last = ee_ref[pl.ds(r0 + c - 1, 1), g * M_GW:(g + 1) * M_GW]
            s_ref[g] = sg[g] * e_last + _dot_tn(bb[g], _b(xw_ref[rows, g * M_GW:(g + 1) * M_GW]))
        a2s, rhs = [], []
        for g, pp in pairs:
            h0 = g * M_HPG + 2 * pp
            ls = slice(h0 * M_HEADDIM, (h0 + 2) * M_HEADDIM)
            colsel = cumx_ref[rows, ls]
            rowsel = jnp.where(left[0:1, :], cum_t[h0:h0 + 1, :], cum_t[h0 + 1:h0 + 2, :])
            d2 = jnp.where(causal2, jnp.exp(colsel - rowsel), 0.0)
            a2s.append(_b(g2[g] * d2))
            xp = xdt_ref[rows, ls]
            rhs.append(_b(jnp.concatenate([jnp.where(left, xp, 0.0), jnp.where(left, 0.0, xp)], axis=0)))
        intra = [_dot(a, r) for a, r in zip(a2s, rhs)]
        ys = []
        for n, (g, pp) in enumerate(pairs):
            h0 = g * M_HPG + 2 * pp
            ls = slice(h0 * M_HEADDIM, (h0 + 2) * M_HEADDIM)
            y = intra[n] + inter[g][:, pp * 2 * M_HEADDIM:(pp + 1) * 2 * M_HEADDIM] + dexp_ref[:, ls] * xc_ref[rows, ls]
            ys.append(y * jax.nn.silu(z_ref[rows, ls]))
        for g in gr:
            mine = [n for n, (gg, _) in enumerate(pairs) if gg == g]
            ms = sum(jnp.sum(jnp.square(ys[n]), -1, keepdims=True) for n in mine) * (1.0 / M_GW)
            r = lax.rsqrt(ms + NORM_EPS)
            for n in mine:
                h0 = g * M_HPG + 2 * pairs[n][1]
                ls = slice(h0 * M_HEADDIM, (h0 + 2) * M_HEADDIM)
                act_ref[rows, ls] = _b(ys[n] * r * nw_ref[:, ls])

    def chunks(it, carry):
        for j in range(SSD_UNROLL):
            chunk(it * SSD_UNROLL + j)
        return carry

    lax.fori_loop(0, tl // (c * SSD_UNROLL), chunks, 0)

    @pl.when(l == pl.num_programs(1) - 1)
    def _():
        for h in range(M_HEADS):
            g, hh = divmod(h, M_HPG)
            st_ref[0, h] = s_ref[g][:, hh * M_HEADDIM:(hh + 1) * M_HEADDIM].T
        conv_ref[0] = xbuf_ref[SUBLANES - (CONV_W - 1):SUBLANES, :]


def _ssd_scan(x, w, cw, cb, dtb, alog, dexp, nw, nb, nl_tok, tl, c):
    assert 2 * c == LANES and 2 * M_HEADDIM == LANES, "head pairs are packed into one 128-lane slab"
    w, wspec = _weight(w)
    nl = nl_tok // tl
    tok = lambda b, l: (b * nl + l, 0)
    return pl.pallas_call(
        functools.partial(_ssd_scan_kernel, tl=tl, c=c),
        out_shape=(jax.ShapeDtypeStruct((nb * nl_tok, M_INNER), bf16),
                   jax.ShapeDtypeStruct((nb, M_HEADS, M_HEADDIM, M_STATE), f32),
                   jax.ShapeDtypeStruct((nb, CONV_W - 1, M_CONV_DIM), f32)),
        grid=(nb, nl),
        in_specs=[pl.BlockSpec((tl, D_MODEL), tok)]
                 + [wspec] + [_const_spec(a.shape) for a in (cw, cb, dtb, alog, dexp, nw)],
        out_specs=(pl.BlockSpec((tl, M_INNER), tok),
                   pl.BlockSpec((1, M_HEADS, M_HEADDIM, M_STATE), lambda b, l: (b, 0, 0, 0)),
                   pl.BlockSpec((1, CONV_W - 1, M_CONV_DIM), lambda b, l: (b, 0, 0))),
        scratch_shapes=[pltpu.VMEM((tl, M_INNER), f32),
                        pltpu.VMEM((tl + SUBLANES, M_CONV_DIM), f32),
                        pltpu.VMEM((tl, M_CONV_DIM), f32),
                        pltpu.VMEM((tl, PAD), f32),
                        pltpu.VMEM((tl, M_INNER), f32),
                        pltpu.VMEM((tl, M_INNER), f32),
                        pltpu.VMEM((tl, M_INNER), f32),
                        pltpu.VMEM((tl, M_INNER), f32),
                        pltpu.VMEM((tl, tl), bf16),
                        pltpu.VMEM((3 * PAD, M_INNER), bf16),
                        pltpu.VMEM((M_GROUPS, M_STATE, M_GW), f32)],
        compiler_params=_params(("parallel", "arbitrary")),
        name="ssd_scan",
    )(x, w, cw, cb, dtb, alog, dexp, nw)


def _pair_blockdiag(x, left):
    return jnp.concatenate([jnp.where(left, x, jnp.zeros_like(x)), jnp.where(left, jnp.zeros_like(x), x)], axis=0)


def _gdn_scan_kernel(x_ref, w_ref, cw_ref, dtb_ref, alog_ref, nw_ref,
                     act_ref, st_ref, conv_ref,
                     gz_ref, xbuf_ref, qkv_ref, cum_ref, ee_ref,
                     q16_ref, k16_ref, kb16_ref, qe16_ref, kbe16_ref, kw16_ref, vb16_ref,
                     wy_ref, u0_ref, attn_ref, tri_ref, e2_ref, s_ref, *, tl, c):
    l = pl.program_id(1)
    n_pairs = G_HEADS // 2
    pw = 2 * G_DK

    @pl.when(l == 0)
    def _():
        s_ref[...] = jnp.zeros_like(s_ref)
        _chunk_block_mask(tri_ref, tl, c)
        _head_expander(e2_ref, G_DK)

    xb = _b(x_ref[...])
    ab = _dot_nt(xb, w_ref[G_QKV + G_VAL:G_QKV + G_VAL + PAD, :])
    _project_and_conv(xb, w_ref, 0, G_QKV, xbuf_ref, cw_ref, None, qkv_ref, tl, l == 0)
    gz_ref[...] = _dot_nt(xb, w_ref[G_QKV:G_QKV + G_VAL, :])
    g = -jnp.exp(alog_ref[...]) * jax.nn.softplus(ab + dtb_ref[...])
    cum, tot = _chunk_cumsum(tri_ref, cum_ref, g, c)
    beta = pltpu.roll(jax.nn.sigmoid(ab), PAD - G_HEADS, axis=1)
    e_c, w_c = jnp.exp(cum), jnp.exp(tot - cum)
    for hb in range(n_pairs):
        ls = slice(hb * pw, (hb + 1) * pw)
        e_x = _expand_heads(e_c, e2_ref.at[:, ls])
        w_x = _expand_heads(w_c, e2_ref.at[:, ls])
        b_x = _expand_heads(beta, e2_ref.at[:, ls])
        ee_ref[:, ls] = e_x
        qn, kn = [], []
        for t in range(2):
            hs = slice((2 * hb + t) * G_DK, (2 * hb + t + 1) * G_DK)
            qt, kt = qkv_ref[:, hs], qkv_ref[:, G_KEY + hs.start:G_KEY + hs.stop]
            qn.append(qt * lax.rsqrt(jnp.sum(jnp.square(qt), -1, keepdims=True) + NORM_EPS) * (G_DK ** -0.5))
            kn.append(kt * lax.rsqrt(jnp.sum(jnp.square(kt), -1, keepdims=True) + NORM_EPS))
        q, k = jnp.concatenate(qn, axis=1), jnp.concatenate(kn, axis=1)
        kb = k * b_x
        q16_ref[:, ls] = _b(q)
        k16_ref[:, ls] = _b(k)
        kb16_ref[:, ls] = _b(kb)
        qe16_ref[:, ls] = _b(q * e_x)
        kbe16_ref[:, ls] = _b(kb * e_x)
        kw16_ref[:, ls] = _b(k * w_x)
        vb16_ref[:, ls] = _b(qkv_ref[:, 2 * G_KEY + hb * pw:2 * G_KEY + (hb + 1) * pw] * b_x)

    lane = lax.broadcasted_iota(jnp.int32, (c, 2 * c), 1)
    rowi = lax.broadcasted_iota(jnp.int32, (c, 2 * c), 0)
    left = lane < c
    jloc = lane & (c - 1)
    causal2, strict2 = jloc <= rowi, jloc < rowi
    eye2 = (jloc == rowi).astype(f32)
    left_w = lax.broadcasted_iota(jnp.int32, (c, pw), 1) < G_DK
    left_s = lax.broadcasted_iota(jnp.int32, (G_DK, pw), 1) < G_DK
    pr = range(n_pairs)

    def precompute(it, carry):
        cks = [it * GDN_PRE_UNROLL + j for j in range(GDN_PRE_UNROLL)]
        rows = [pl.ds(pl.multiple_of(ck * c, c), c) for ck in cks]
        lsl = [slice(p * pw, (p + 1) * pw) for p in pr]
        cp = [(j, p) for j in range(GDN_PRE_UNROLL) for p in pr]
        cum_c = [cum_ref[r, :] for r in rows]
        cum_t = [jnp.concatenate([x, x], axis=0).T for x in cum_c]
        d2 = []
        for j, p in cp:
            colsel = jnp.where(left, cum_c[j][:, 2 * p:2 * p + 1], cum_c[j][:, 2 * p + 1:2 * p + 2])
            rowsel = jnp.where(left[0:1, :], cum_t[j][2 * p:2 * p + 1, :], cum_t[j][2 * p + 1:2 * p + 2, :])
            d2.append(jnp.where(causal2, jnp.exp(colsel - rowsel), 0.0))
        kbd = [_pair_blockdiag(k16_ref[rows[j], lsl[p]], left_w) for j, p in cp]
        kq = [_dot_nt(jnp.concatenate([kb16_ref[rows[j], lsl[p]], q16_ref[rows[j], lsl[p]]], axis=0), kbd[n])
              for n, (j, p) in enumerate(cp)]
        lm = [jnp.where(strict2, kq[n][0:c, :] * d2[n], 0.0) for n in range(len(cp))]
        attn = [kq[n][c:2 * c, :] * d2[n] for n in range(len(cp))]
        for n, (j, p) in enumerate(cp):
            attn_ref[rows[j], p * 2 * c:(p + 1) * 2 * c] = _b(attn[n])
        ps = [eye2 - x for x in lm]
        ms = [_dot(_b(m), _b(_pair_blockdiag(m, left))) for m in lm]
        kpow = 2
        while kpow < c:
            last = 2 * kpow >= c
            bd = [_b(_pair_blockdiag(m, left)) for m in ms]
            lhs = [_b(x) if last else _b(jnp.concatenate([m, x], axis=0)) for m, x in zip(ms, ps)]
            mp = [_dot(a, b) for a, b in zip(lhs, bd)]
            if last:
                ps = [x + y for x, y in zip(ps, mp)]
            else:
                ms = [y[0:c, :] for y in mp]
                ps = [x + y[c:2 * c, :] for x, y in zip(ps, mp)]
            kpow *= 2
        rhs = [jnp.concatenate([_pair_blockdiag(kbe16_ref[rows[j], lsl[p]], left_w),
                                _pair_blockdiag(vb16_ref[rows[j], lsl[p]], left_w)], axis=1) for j, p in cp]
        wu = [_dot(_b(ps[n]), rhs[n]) for n in range(len(cp))]
        for n, (j, p) in enumerate(cp):
            wy_ref[rows[j], lsl[p]] = _b(wu[n][:, 0:pw])
            u0_ref[rows[j], lsl[p]] = wu[n][:, pw:2 * pw]
        return carry

    lax.fori_loop(0, tl // (c * GDN_PRE_UNROLL), precompute, 0)

    def scan_chunk(ck):
        r0 = pl.multiple_of(ck * c, c)
        rows = pl.ds(r0, c)
        lsl = [slice(p * pw, (p + 1) * pw) for p in pr]
        sp = [s_ref[p] for p in pr]
        sbd = [_b(_pair_blockdiag(sp[p], left_s)) for p in pr]
        r = [_dot(jnp.concatenate([wy_ref[rows, lsl[p]], qe16_ref[rows, lsl[p]]], axis=0), sbd[p]) for p in pr]
        u = [u0_ref[rows, lsl[p]] - r[p][0:c, :] for p in pr]
        ubd = [_b(_pair_blockdiag(u[p], left_w)) for p in pr]
        for p in pr:
            kw = kw16_ref[rows, lsl[p]]
            kw_stack = jnp.concatenate([kw[:, 0:G_DK], kw[:, G_DK:pw]], axis=0)
            e_last = ee_ref[pl.ds(r0 + c - 1, 1), lsl[p]]
            s_ref[p] = sp[p] * e_last + _dot_tn(kw_stack, ubd[p])
        o = [r[p][c:2 * c, :] + _dot(attn_ref[rows, p * 2 * c:(p + 1) * 2 * c], ubd[p]) for p in pr]
        for p in pr:
            for t in range(2):
                hs = slice((2 * p + t) * G_DV, (2 * p + t + 1) * G_DV)
                act_ref[rows, hs] = _b(_rms(o[p][:, t * G_DV:(t + 1) * G_DV]) * nw_ref[...]
                                       * jax.nn.silu(gz_ref[rows, hs]))

    def scan(it, carry):
        for j in range(GDN_SCAN_UNROLL):
            scan_chunk(it * GDN_SCAN_UNROLL + j)
        return carry

    lax.fori_loop(0, tl // (c * GDN_SCAN_UNROLL), scan, 0)

    @pl.when(l == pl.num_programs(1) - 1)
    def _():
        for h in range(G_HEADS):
            st_ref[0, h] = s_ref[h // 2][:, (h % 2) * G_DV:(h % 2 + 1) * G_DV]
        conv_ref[0] = xbuf_ref[SUBLANES - (CONV_W - 1):SUBLANES, :]


def _gdn_scan(x, w, cw, dtb, alog, nw, nb, nl_tok, tl, c):
    assert 2 * c == LANES and G_DK == G_DV == LANES, "two heads' (c, c) blocks share one 128-lane slab"
    nl = nl_tok // tl
    tok = lambda b, l: (b * nl + l, 0)
    return pl.pallas_call(
        functools.partial(_gdn_scan_kernel, tl=tl, c=c),
        out_shape=(jax.ShapeDtypeStruct((nb * nl_tok, G_VAL), bf16),
                   jax.ShapeDtypeStruct((nb, G_HEADS, G_DK, G_DV), f32),
                   jax.ShapeDtypeStruct((nb, CONV_W - 1, G_QKV), f32)),
        grid=(nb, nl),
        in_specs=[pl.BlockSpec((tl, D_MODEL), tok)]
                 + [_const_spec(a.shape) for a in (w, cw, dtb, alog, nw)],
        out_specs=(pl.BlockSpec((tl, G_VAL), tok),
                   pl.BlockSpec((1, G_HEADS, G_DK, G_DV), lambda b, l: (b, 0, 0, 0)),
                   pl.BlockSpec((1, CONV_W - 1, G_QKV), lambda b, l: (b, 0, 0))),
        scratch_shapes=[pltpu.VMEM((tl, G_VAL), f32),
                        pltpu.VMEM((tl + SUBLANES, G_QKV), f32),
                        pltpu.VMEM((tl, G_QKV), f32),
                        pltpu.VMEM((tl, PAD), f32),
                        pltpu.VMEM((tl, G_KEY), f32),
                        ] + [pltpu.VMEM((tl, G_KEY), bf16)] * 7 + [
                        pltpu.VMEM((tl, G_KEY), bf16),
                        pltpu.VMEM((tl, G_VAL), f32),
                        pltpu.VMEM((tl, G_HEADS * c), bf16),
                        pltpu.VMEM((tl, tl), bf16),
                        pltpu.VMEM((2 * PAD, G_KEY), bf16),
                        pltpu.VMEM((G_HEADS // 2, G_DK, 2 * G_DV), f32)],
        compiler_params=_params(("parallel", "arbitrary")),
        name="gdn_scan",
    )(x, w, cw, dtb, alog, nw)


def _token_lanes_to_front(src_ref, dst_ref, i):
    n = src_ref.shape[1]
    dst_ref[...] = pltpu.roll(src_ref[...], (n - i * DEC_BT) % n, axis=1)


def _conv_step(cst_ref, x_new, cw_ref, cb_ref, conv_out_ref):
    acc = x_new * cw_ref[CONV_W - 1:CONV_W, :]
    for j in range(CONV_W - 1):
        acc = acc + cst_ref[j] * cw_ref[j:j + 1, :]
    if cb_ref is not None:
        acc = acc + cb_ref[...]
    for j in range(CONV_W - 2):
        conv_out_ref[j] = cst_ref[j + 1]
    conv_out_ref[CONV_W - 2] = x_new
    return jax.nn.silu(acc)


def _state_specs(state, prev, layer, blk):
    zeros = (0,) * (len(blk) - 2)
    spec = pl.BlockSpec(blk, lambda i: (layer, i) + zeros)
    if prev is None:
        prev = jnp.zeros((SUBLANES, LANES), f32)
        alias = {}
    else:
        alias = None
    return spec, prev, alias


def _dec_ret_kernel(x_ref, w_ref, cos_ref, sin_ref, s_in, prev_ref, act_ref, s_out,
                    q_ref, kt_ref, vg_ref, ks_ref, o_ref):
    del prev_ref
    i = pl.program_id(0)

    @pl.when(i == 0)
    def _():
        proj = _dot_nt(_b(x_ref[...]), w_ref[...])
        cos, sin = cos_ref[...], sin_ref[...]
        for h in range(R_HEADS):
            hs = slice(h * R_DK, (h + 1) * R_DK)
            t = proj[:, hs]
            q_ref[:, hs] = t * cos + pltpu.roll(t, R_DK // 2, axis=1) * sin
            t = proj[:, R_QK + h * R_DK:R_QK + (h + 1) * R_DK]
            q_ref[:, R_QK + h * R_DK:R_QK + (h + 1) * R_DK] = t = (
                t * cos + pltpu.roll(t, R_DK // 2, axis=1) * sin) * (R_DK ** -0.5)
            kt_ref[hs, :] = t.T
        vg_ref[...] = proj[:, 2 * R_QK:2 * R_QK + 2 * R_VAL]

    _token_lanes_to_front(kt_ref, ks_ref, i)
    rows = pl.ds(pl.multiple_of(i * DEC_BT, DEC_BT), DEC_BT)
    vg, qk8 = vg_ref[rows, :], q_ref[rows, :]
    for h in range(R_HEADS):
        q8 = qk8[:, h * R_DK:(h + 1) * R_DK]
        k8 = qk8[:, R_QK + h * R_DK:R_QK + (h + 1) * R_DK]
        qk = jnp.sum(q8 * k8, axis=1, keepdims=True)
        lhs = _b(jnp.concatenate([q8, q8], axis=0))
        gam = math.exp(LOG_GAMMA[h])
        for j in range(DEC_BT):
            kc = ks_ref[h * R_DK:(h + 1) * R_DK, j:j + 1]
            v_row = vg[j:j + 1, h * R_DV:(h + 1) * R_DV]
            s = s_in[j, h]
            qs = _dot(lhs, _b(s))[j:j + 1, :]
            o_ref[j:j + 1, h * R_DV:(h + 1) * R_DV] = gam * qs + qk[j:j + 1, :] * v_row
            s_out[j, h] = s * gam + kc * v_row
    for h in range(R_HEADS):
        o = o_ref[:, h * R_DV:(h + 1) * R_DV]
        mu = jnp.mean(o, -1, keepdims=True)
        var = jnp.mean(jnp.square(o - mu), -1, keepdims=True)
        g = vg[:, R_VAL + h * R_DV:R_VAL + (h + 1) * R_DV]
        act_ref[:, h * R_DV:(h + 1) * R_DV] = (o - mu) * lax.rsqrt(var + LN_EPS) * jax.nn.silu(g)


def _dec_ret(x, w, cos, sin, state, layer, prev):
    n = x.shape[0]
    w, wspec = _weight(w)
    consts = (x, w, cos, sin)
    sspec, prev, alias = _state_specs(state, prev, layer, (None, DEC_BT, R_HEADS, R_DK, R_DV))
    return pl.pallas_call(
        _dec_ret_kernel,
        out_shape=(jax.ShapeDtypeStruct((n, R_VAL), f32), jax.ShapeDtypeStruct(state.shape, f32)),
        grid=(n // DEC_BT,),
        in_specs=[_const_spec(x.shape), wspec, _const_spec(cos.shape), _const_spec(sin.shape),
                  sspec, pl.BlockSpec(memory_space=pl.ANY)],
        out_specs=(pl.BlockSpec((DEC_BT, R_VAL), lambda i: (i, 0)), sspec),
        scratch_shapes=[pltpu.VMEM((n, 2 * R_QK), f32),
                        pltpu.VMEM((R_QK, n), f32),
                        pltpu.VMEM((n, 2 * R_VAL), f32),
                        pltpu.VMEM((R_QK, n), f32),
                        pltpu.VMEM((DEC_BT, R_VAL), f32)],
        input_output_aliases={len(consts) + 1: 1} if alias is None else alias,
        compiler_params=_params(("arbitrary",)),
        name="dec_ret",
    )(*consts, state, prev)


def _dec_ssd_kernel(x_ref, w_ref, cst_ref, cw_ref, cb_ref, dtb_ref, alog_ref, dexp_ref, nw_ref, s_in, prev_ref,
                    act_ref, s_out, conv_out_ref,
                    xc_ref, z_ref, xdt_ref, xdtt_ref, ela_ref, elax_ref, xts_ref, e2_ref, o_ref):
    del prev_ref
    i = pl.program_id(0)

    @pl.when(i == 0)
    def _():
        _head_expander(e2_ref, M_HEADDIM)
        proj = _dot_nt(_b(x_ref[...]), w_ref[...])
        xc = _conv_step(cst_ref, proj[:, M_INNER:M_INNER + M_CONV_DIM], cw_ref, cb_ref, conv_out_ref)
        xc_ref[...] = xc
        z_ref[...] = proj[:, 0:M_INNER]
        dt = jax.nn.softplus(proj[:, M_INNER + M_CONV_DIM:M_INNER + M_CONV_DIM + PAD] + dtb_ref[...])
        ela = jnp.exp(-jnp.exp(alog_ref[...]) * dt)
        ela_ref[...] = ela
        elax_ref[...] = _expand_heads(ela, e2_ref)
        xdt = xc[:, 0:M_INNER] * _expand_heads(dt, e2_ref)
        xdt_ref[...] = xdt
        xdtt_ref[...] = xdt.T

    _token_lanes_to_front(xdtt_ref, xts_ref, i)
    rows = pl.ds(pl.multiple_of(i * DEC_BT, DEC_BT), DEC_BT)
    xc8, xdt8, ela8, elax8 = xc_ref[rows, :], xdt_ref[rows, :], ela_ref[rows, :], elax_ref[rows, :]
    b_off, c_off = M_INNER, M_INNER + M_GROUPS * M_STATE
    for g in range(M_GROUPS):
        gs = slice(g * M_GW, (g + 1) * M_GW)
        b8 = xc8[:, b_off + g * M_STATE:b_off + (g + 1) * M_STATE]
        c8 = xc8[:, c_off + g * M_STATE:c_off + (g + 1) * M_STATE]
        cb = jnp.sum(c8 * b8, axis=1, keepdims=True)
        lhs = _b(jnp.concatenate([c8, c8], axis=0))
        for j in range(DEC_BT):
            st = s_in[j, g * M_HPG:(g + 1) * M_HPG]
            cs = _dot_nt(lhs, _b(st.reshape(M_GW, M_STATE)))[j:j + 1, :]
            o_ref[j:j + 1, gs] = cb[j:j + 1, :] * xdt8[j:j + 1, gs] + elax8[j:j + 1, gs] * cs
            b_row = b8[j:j + 1, :]
            for hh in range(M_HPG):
                h = g * M_HPG + hh
                xdt_col = xts_ref[h * M_HEADDIM:(h + 1) * M_HEADDIM, j:j + 1]
                s_out[j, h] = st[hh] * ela8[j:j + 1, h:h + 1] + xdt_col * b_row
    y = (o_ref[...] + dexp_ref[...] * xc8[:, 0:M_INNER]) * jax.nn.silu(z_ref[rows, :])
    for g in range(M_GROUPS):
        gs = slice(g * M_GW, (g + 1) * M_GW)
        act_ref[:, gs] = _rms(y[:, gs]) * nw_ref[:, gs]


def _dec_ssd(x, w, cst, cw, cb, dtb, alog, dexp, nw, state, layer, prev):
    n = x.shape[0]
    w, wspec = _weight(w)
    consts = (x, w, cst, cw, cb, dtb, alog, dexp, nw)
    sspec, prev, alias = _state_specs(state, prev, layer, (None, DEC_BT, M_HEADS, M_HEADDIM, M_STATE))
    cspecs = [_const_spec(a.shape) for a in consts]
    cspecs[1] = wspec
    cspecs[2] = pl.BlockSpec((None,) + cst.shape[1:], lambda i: (layer, 0, 0, 0), pipeline_mode=pl.Buffered(1))
    return pl.pallas_call(
        _dec_ssd_kernel,
        out_shape=(jax.ShapeDtypeStruct((n, M_INNER), f32), jax.ShapeDtypeStruct(state.shape, f32),
                   jax.ShapeDtypeStruct(cst.shape[1:], f32)),
        grid=(n // DEC_BT,),
        in_specs=cspecs + [sspec, pl.BlockSpec(memory_space=pl.ANY)],
        out_specs=(pl.BlockSpec((DEC_BT, M_INNER), lambda i: (i, 0)), sspec,
                   pl.BlockSpec(cst.shape[1:], lambda i: (0, 0, 0))),
        scratch_shapes=[pltpu.VMEM((n, M_CONV_DIM), f32),
                        pltpu.VMEM((n, M_INNER), f32),
                        pltpu.VMEM((n, M_INNER), f32),
                        pltpu.VMEM((M_INNER, n), f32),
                        pltpu.VMEM((n, PAD), f32),
                        pltpu.VMEM((n, M_INNER), f32),
                        pltpu.VMEM((M_INNER, n), f32),
                        pltpu.VMEM((2 * PAD, M_INNER), bf16),
                        pltpu.VMEM((DEC_BT, M_INNER), f32)],
        input_output_aliases={len(consts) + 1: 1} if alias is None else alias,
        compiler_params=_params(("arbitrary",)),
        name="dec_ssd",
    )(*consts, state, prev)


def _dec_gdn_kernel(x_ref, w_ref, cst_ref, cw_ref, dtb_ref, alog_ref, nw_ref, s_in, prev_ref,
                    act_ref, s_out, conv_out_ref,
                    qkv_ref, kt_ref, gz_ref, eg_ref, beta_ref, ks_ref, o_ref):
    del prev_ref
    i = pl.program_id(0)

    @pl.when(i == 0)
    def _():
        proj = _dot_nt(_b(x_ref[...]), w_ref[...])
        qkv = _conv_step(cst_ref, proj[:, 0:G_QKV], cw_ref, None, conv_out_ref)
        for h in range(G_HEADS):
            hs = slice(h * G_DK, (h + 1) * G_DK)
            q = qkv[:, hs]
            qkv_ref[:, hs] = q * lax.rsqrt(jnp.sum(jnp.square(q), -1, keepdims=True) + NORM_EPS) * (G_DK ** -0.5)
            k = qkv[:, G_KEY + h * G_DK:G_KEY + (h + 1) * G_DK]
            k = k * lax.rsqrt(jnp.sum(jnp.square(k), -1, keepdims=True) + NORM_EPS)
            qkv_ref[:, G_KEY + h * G_DK:G_KEY + (h + 1) * G_DK] = k
            kt_ref[hs, :] = k.T
        qkv_ref[:, 2 * G_KEY:] = qkv[:, 2 * G_KEY:]
        gz_ref[...] = proj[:, G_QKV:G_QKV + G_VAL]
        ab = proj[:, G_QKV + G_VAL:G_QKV + G_VAL + PAD]
        eg_ref[...] = jnp.exp(-jnp.exp(alog_ref[...]) * jax.nn.softplus(ab + dtb_ref[...]))
        beta_ref[...] = jax.nn.sigmoid(ab)

    _token_lanes_to_front(kt_ref, ks_ref, i)
    rows = pl.ds(pl.multiple_of(i * DEC_BT, DEC_BT), DEC_BT)
    qkv8, eg8, beta8 = qkv_ref[rows, :], eg_ref[rows, :], beta_ref[rows, :]
    for h in range(G_HEADS):
        hs = slice(h * G_DV, (h + 1) * G_DV)
        q8 = qkv8[:, h * G_DK:(h + 1) * G_DK]
        k8 = qkv8[:, G_KEY + h * G_DK:G_KEY + (h + 1) * G_DK]
        v8 = qkv8[:, 2 * G_KEY + h * G_DV:2 * G_KEY + (h + 1) * G_DV]
        qk = jnp.sum(q8 * k8, axis=1, keepdims=True)
        lhs = _b(jnp.concatenate([q8, k8], axis=0))
        bh = beta8[:, G_HEADS + h:G_HEADS + h + 1]
        eg = eg8[:, h:h + 1]
        for j in range(DEC_BT):
            kc = ks_ref[h * G_DK:(h + 1) * G_DK, j:j + 1]
            s = s_in[j, h]
            qks = _dot(lhs, _b(s))
            bj, ej = bh[j:j + 1, :], eg[j:j + 1, :]
            u = v8[j:j + 1, :] * bj - (bj * ej) * qks[DEC_BT + j:DEC_BT + j + 1, :]
            o_ref[j:j + 1, hs] = ej * qks[j:j + 1, :] + qk[j:j + 1, :] * u
            s_out[j, h] = s * ej + kc * u
    gz8 = gz_ref[rows, :]
    for h in range(G_HEADS):
        hs = slice(h * G_DV, (h + 1) * G_DV)
        act_ref[:, hs] = _rms(o_ref[:, hs]) * nw_ref[...] * jax.nn.silu(gz8[:, hs])


def _dec_gdn(x, w, cst, cw, dtb, alog, nw, state, layer, prev):
    n = x.shape[0]
    consts = (x, w, cst, cw, dtb, alog, nw)
    sspec, prev, alias = _state_specs(state, prev, layer, (None, DEC_BT, G_HEADS, G_DK, G_DV))
    cspecs = [_const_spec(a.shape) for a in consts]
    cspecs[2] = pl.BlockSpec((None,) + cst.shape[1:], lambda i: (layer, 0, 0, 0), pipeline_mode=pl.Buffered(1))
    return pl.pallas_call(
        _dec_gdn_kernel,
        out_shape=(jax.ShapeDtypeStruct((n, G_VAL), f32), jax.ShapeDtypeStruct(state.shape, f32),
                   jax.ShapeDtypeStruct(cst.shape[1:], f32)),
        grid=(n // DEC_BT,),
        in_specs=cspecs + [sspec, pl.BlockSpec(memory_space=pl.ANY)],
        out_specs=(pl.BlockSpec((DEC_BT, G_VAL), lambda i: (i, 0)), sspec,
                   pl.BlockSpec(cst.shape[1:], lambda i: (0, 0, 0))),
        scratch_shapes=[pltpu.VMEM((n, G_QKV), f32),
                        pltpu.VMEM((G_KEY, n), f32),
                        pltpu.VMEM((n, G_VAL), f32),
                        pltpu.VMEM((n, PAD), f32), pltpu.VMEM((n, PAD), f32),
                        pltpu.VMEM((G_KEY, n), f32),
                        pltpu.VMEM((DEC_BT, G_VAL), f32)],
        input_output_aliases={len(consts) + 1: 1} if alias is None else alias,
        compiler_params=_params(("arbitrary",)),
        name="dec_gdn",
    )(*consts, state, prev)


def _rope_tables(pos):
    half = R_DK // 2
    inv = ROPE_BASE ** (-jnp.arange(half, dtype=f32) / half)
    ang = pos[:, None] * inv[None, :]
    cos, sin = jnp.cos(ang), jnp.sin(ang)
    return jnp.concatenate([cos, cos], axis=1), jnp.concatenate([-sin, sin], axis=1)


def _lane_pad(v, start=0):
    return jnp.zeros((1, PAD), f32).at[0, start:start + v.shape[0]].set(v)


def _stacked_weights(prm):
    return {k: _b(prm[k]) for k in ("ffn_wg", "ffn_wu", "ffn_wd", "w_ret_out", "w_ssm_out", "w_gdn_out", "w_o",
                                    "pe_proj", "pe_gate")}


def _layer_weights(i, prm):
    return dict(
        ln_g=prm["ln_g"][i], ln_b=prm["ln_b"][i],
        ssm_cw=prm["ssm_conv_w"][i], ssm_cb=prm["ssm_conv_b"][i][None, :],
        ssm_dtb=_lane_pad(prm["ssm_dt_bias"][i]), ssm_alog=_lane_pad(prm["ssm_a_log"][i]),
        ssm_dexp=jnp.repeat(prm["ssm_d"][i], M_HEADDIM)[None, :], ssm_nw=prm["ssm_norm_w"][i][None, :],
        gdn_cw=prm["gdn_conv_w"][i],
        gdn_dtb=_lane_pad(prm["gdn_dt_bias"][i]), gdn_alog=_lane_pad(prm["gdn_a_log"][i]),
        gdn_nw=prm["gdn_norm_w"][i][None, :],
    )


def _post_mix(x1, acts, p, i, w, sw, tm):
    x2 = _merge(x1, *acts, w["w_merge"], sw["w_ret_out"], sw["w_ssm_out"], sw["w_gdn_out"], sw["w_o"],
                w["ln_g"], w["ln_b"], i, tm)
    return _ffn_pe(x2, p, i, sw["ffn_wg"], sw["ffn_wu"], sw["ffn_wd"], sw["pe_gate"], sw["pe_proj"],
                   w["ln_g"], w["ln_b"], min(TM_FFN, x2.shape[0]))


def kernel(x_prompt, x_sample, state_ret, state_ssm, state_ssm_conv, state_gdn, state_gdn_conv,
           p_prompt, p_sample, ln_g, ln_b, ffn_wg, ffn_wu, ffn_wd, w_in,
           ssm_conv_w, ssm_conv_b, ssm_dt_bias, ssm_a_log, ssm_d, ssm_norm_w,
           gdn_conv_w, gdn_dt_bias, gdn_a_log, gdn_norm_w,
           w_ret_out, w_ssm_out, w_gdn_out, w_o, pe_proj, pe_gate):
    prm = dict(ln_g=ln_g, ln_b=ln_b, ffn_wg=ffn_wg, ffn_wu=ffn_wu, ffn_wd=ffn_wd, w_in=w_in,
               ssm_conv_w=ssm_conv_w, ssm_conv_b=ssm_conv_b, ssm_dt_bias=ssm_dt_bias,
               ssm_a_log=ssm_a_log, ssm_d=ssm_d, ssm_norm_w=ssm_norm_w,
               gdn_conv_w=gdn_conv_w, gdn_dt_bias=gdn_dt_bias, gdn_a_log=gdn_a_log,
               gdn_norm_w=gdn_norm_w, w_ret_out=w_ret_out, w_ssm_out=w_ssm_out,
               w_gdn_out=w_gdn_out, w_o=w_o, pe_proj=pe_proj, pe_gate=pe_gate)
    nb, seq, _ = x_prompt.shape
    ns = x_sample.shape[0]
    depth = w_in.shape[0]
    tl = min(TL_SCAN, seq)
    tm = min(TM_DENSE, nb * seq)
    chunk = CHUNK if seq % CHUNK == 0 else seq

    cos_p, sin_p = _rope_tables(jnp.arange(seq, dtype=f32))
    cos_s, sin_s = _rope_tables(jnp.full((1,), PAST_LEN, f32))
    xp = x_prompt.reshape(nb * seq, D_MODEL)
    xs = x_sample.reshape(ns, D_MODEL)
    pp = p_prompt.reshape(depth, nb * seq, PLE_DIM)
    ps = p_sample.reshape(depth, ns, PLE_DIM)
    ssm_state_t = jnp.swapaxes(state_ssm, 3, 4)
    ssm_conv_t = jnp.transpose(state_ssm_conv, (0, 2, 1, 3))
    gdn_conv_t = jnp.transpose(state_gdn_conv, (0, 2, 1, 3))

    sw = _stacked_weights(prm)
    w_in_t = _b(jnp.swapaxes(w_in, 1, 2))
    gdn_rows = G_QKV + G_VAL + PAD
    assert OFF_RET == 0 and OFF_SSD == W_IN_BLK and OFF_SSD + M_INNER + M_CONV_DIM + PAD <= 2 * W_IN_BLK
    assert OFF_GDN + gdn_rows <= IN_DIM
    prompt_states, sample_convs = [], []
    t_r = t_s = t_g = None
    for i in range(depth):
        w = _layer_weights(i, prm)
        w["w_ret"], w["w_ssd"] = _w_in_block(w_in_t, i, 0), _w_in_block(w_in_t, i, 1)
        w["w_gdn"] = w_in_t[i, OFF_GDN:OFF_GDN + gdn_rows]
        w["w_merge"] = w_in_t[i, OFF_MERGE:]
        x1 = _ffn_ln(xp, sw["ffn_wg"], sw["ffn_wu"], sw["ffn_wd"], w["ln_g"], w["ln_b"], i, min(TM_FFN, nb * seq))
        a_r, s_r = _ret_scan(x1, w["w_ret"], cos_p, sin_p, nb, seq, tl, RET_CHUNK if seq % RET_CHUNK == 0 else chunk)
        a_s, s_s, c_s = _ssd_scan(x1, w["w_ssd"], w["ssm_cw"], w["ssm_cb"], w["ssm_dtb"], w["ssm_alog"],
                                  w["ssm_dexp"], w["ssm_nw"], nb, seq, tl, chunk)
        a_g, s_g, c_g = _gdn_scan(x1, w["w_gdn"], w["gdn_cw"], w["gdn_dtb"], w["gdn_alog"], w["gdn_nw"],
                                  nb, seq, tl, chunk)
        xp = _post_mix(x1, (a_r, a_s, a_g), pp, i, w, sw, tm)
        prompt_states.append((s_r, s_s, c_s, s_g, c_g))
        y1 = _ffn_ln(xs, sw["ffn_wg"], sw["ffn_wu"], sw["ffn_wd"], w["ln_g"], w["ln_b"], i, ns)
        b_r, t_r = _dec_ret(y1, w["w_ret"], cos_s, sin_s, state_ret, i, t_r)
        b_s, t_s, d_s = _dec_ssd(y1, w["w_ssd"], ssm_conv_t, w["ssm_cw"], w["ssm_cb"], w["ssm_dtb"],
                                 w["ssm_alog"], w["ssm_dexp"], w["ssm_nw"], ssm_state_t, i, t_s)
        b_g, t_g, d_g = _dec_gdn(y1, w["w_gdn"], gdn_conv_t, w["gdn_cw"], w["gdn_dtb"], w["gdn_alog"],
                                 w["gdn_nw"], state_gdn, i, t_g)
        xs = _post_mix(y1, (b_r, b_s, b_g), ps, i, w, sw, ns)
        sample_convs.append((d_s, d_g))

    r_p, s_p, sc_p, g_p, gc_p = (jnp.stack([s[j] for s in prompt_states]) for j in range(5))
    sc_s, gc_s = (jnp.transpose(jnp.stack([c[j] for c in sample_convs]), (0, 2, 1, 3)) for j in range(2))
    return (xp.reshape(nb, seq, D_MODEL), xs.reshape(ns, 1, D_MODEL),
            r_p, jnp.swapaxes(s_p, 3, 4), sc_p, g_p, gc_p,
            t_r, jnp.swapaxes(t_s, 3, 4), sc_s, t_g, gc_s)
```
